```python
import jax, jax.numpy as jnp
from jax import lax
import numpy as np

D_MODEL = 2048
BATCH = 2
SEQ = 8192
DEPTH = 1
DEC_BATCH = 32
DEC_SEQ = 8
PAST_LEN = 16384
PAGE_SIZE = 128

HEAD_DIM = 128
N_ATT_HEADS = 8
ATT_WIDTH = N_ATT_HEADS * HEAD_DIM
MOBA_BLOCK = 256
MOBA_TOPK = 3
MOBA_QBLOCK = 64
N_GLA_HEADS = 4
GLA_DK = 128
GLA_DV = 256
GLA_KW = N_GLA_HEADS * GLA_DK
GLA_VW = N_GLA_HEADS * GLA_DV
GLA_GATE_RANK = 16
GLA_TAU = 16.0
GLA_CHUNK = 64
MIX_WIDTH = ATT_WIDTH + GLA_VW
IN_SPLITS = (ATT_WIDTH, ATT_WIDTH, ATT_WIDTH, GLA_KW, GLA_KW, GLA_VW, GLA_GATE_RANK, GLA_VW)
IN_WIDTH = 3 * ATT_WIDTH + 2 * GLA_KW + 2 * GLA_VW + GLA_GATE_RANK
N_GROUPS = 4
EXPERTS_PER_GROUP = 8
N_EXPERTS = N_GROUPS * EXPERTS_PER_GROUP
EXPERT_TOPK = 2
D_FF_EXPERT = 512
MOE_BLOCK = 128
RMS_EPS = 1e-6
NEG_INF = -1e30

kernel_name = 'hymba_moba_gla_hiermoe_step'

F32 = jnp.float32


def _rmsnorm(x, w):
    xf = x.astype(F32)
    y = xf * lax.rsqrt(jnp.mean(xf * xf, axis=-1, keepdims=True) + RMS_EPS)
    return (y * w.astype(F32)).astype(x.dtype)


def _alibi_slopes():
    return 2.0 ** (-8.0 * jnp.arange(1, N_ATT_HEADS + 1, dtype=F32) / N_ATT_HEADS)


def _project(x, norm_w, w_in, q_norm_w, k_norm_w):
    B, T, _ = x.shape
    z = jnp.einsum('btd,de->bte', _rmsnorm(x, norm_w), w_in)
    offsets = [int(o) for o in np.cumsum(IN_SPLITS)[:-1]]
    q, k, v, gq, gk, gv, ga, gg = jnp.split(z, offsets, axis=-1)
    q = _rmsnorm(q.reshape(B, T, N_ATT_HEADS, HEAD_DIM), q_norm_w)
    k = _rmsnorm(k.reshape(B, T, N_ATT_HEADS, HEAD_DIM), k_norm_w)
    v = v.reshape(B, T, N_ATT_HEADS, HEAD_DIM)
    return q, k, v, (gq, gk, gv, ga, gg)


def _moba_select(q, means, pos_q):
    gate = jnp.einsum('...hqd,...hnd->...hqn', q.astype(F32), means)
    n_full = pos_q // MOBA_BLOCK
    full = jnp.arange(means.shape[-2])[None, :] < n_full[:, None]
    gate = jnp.where(full, gate, NEG_INF)
    _, idx = lax.top_k(gate, MOBA_TOPK)
    ok = jnp.arange(MOBA_TOPK)[None, :] < n_full[:, None]
    return idx, ok


def _moba_prompt(q, k, v, slopes):
    B, T, H, D = q.shape
    n_blk = max(-(-T // MOBA_BLOCK), MOBA_TOPK)
    pad = n_blk * MOBA_BLOCK - T

    def blocks(a):
        a = jnp.pad(a, ((0, 0), (0, pad), (0, 0), (0, 0)))
        return a.transpose(0, 2, 1, 3).reshape(B, H, n_blk, MOBA_BLOCK, D)

    kb, vb = blocks(k), blocks(v)
    means = jnp.mean(kb.astype(F32), axis=3)
    qh = q.transpose(0, 2, 1, 3)
    bi = jnp.arange(B)[:, None, None, None]
    hi = jnp.arange(H)[None, :, None, None]
    offs = jnp.arange(MOBA_BLOCK)
    scale = HEAD_DIM ** -0.5
    n_sel = MOBA_TOPK * MOBA_BLOCK

    def one_block(i):
        start = i * MOBA_QBLOCK
        qb = lax.dynamic_slice_in_dim(qh, start, MOBA_QBLOCK, axis=2)
        pos_q = start + jnp.arange(MOBA_QBLOCK)
        idx, ok = _moba_select(qb, means, pos_q)
        k_sel = kb[bi, hi, idx]
        v_sel = vb[bi, hi, idx]
        pos_sel = idx[..., None] * MOBA_BLOCK + offs
        s_sel = jnp.einsum('bhqd,bhqnsd->bhqns', qb, k_sel, preferred_element_type=F32) * scale
        s_sel = s_sel - slopes[:, None, None, None] * (pos_q[:, None, None] - pos_sel).astype(F32)
        s_sel = jnp.where(ok[:, :, None], s_sel, NEG_INF)
        own = start // MOBA_BLOCK
        k_own = lax.dynamic_index_in_dim(kb, own, axis=2, keepdims=False)
        v_own = lax.dynamic_index_in_dim(vb, own, axis=2, keepdims=False)
        pos_own = own * MOBA_BLOCK + offs
        s_own = jnp.einsum('bhqd,bhsd->bhqs', qb, k_own, preferred_element_type=F32) * scale
        s_own = s_own - slopes[:, None, None] * (pos_q[:, None] - pos_own[None, :]).astype(F32)
        s_own = jnp.where(pos_own[None, :] <= pos_q[:, None], s_own, NEG_INF)
        s = jnp.concatenate([s_sel.reshape(B, H, MOBA_QBLOCK, n_sel), s_own], axis=-1)
        p = jax.nn.softmax(s, axis=-1).astype(v.dtype)
        o = jnp.einsum('bhqs,bhqsd->bhqd', p[..., :n_sel], v_sel.reshape(B, H, MOBA_QBLOCK, n_sel, D), preferred_element_type=F32)
        o = o + jnp.einsum('bhqs,bhsd->bhqd', p[..., n_sel:], v_own, preferred_element_type=F32)
        return o.astype(q.dtype)

    out = lax.map(one_block, jnp.arange(T // MOBA_QBLOCK))
    return out.transpose(1, 0, 3, 2, 4).reshape(B, T, H * D)


def _moba_sample(q, k, v, pool_k, pool_v, page_table, slopes):
    DB, Q, H, D = q.shape
    n_pages = page_table.shape[1]
    past = n_pages * PAGE_SIZE
    n_blk = max(-(-(past + Q) // MOBA_BLOCK), MOBA_TOPK)
    pos_q = past + jnp.arange(Q)
    blk = jnp.arange(n_blk)
    page_sum = jnp.sum(pool_k, axis=1, dtype=F32)[page_table]
    m_page = (((jnp.arange(n_pages) * PAGE_SIZE) // MOBA_BLOCK)[:, None] == blk[None, :]).astype(F32)
    m_new = ((pos_q // MOBA_BLOCK)[:, None] == blk[None, :]).astype(F32)
    means = (jnp.einsum('bphd,pn->bhnd', page_sum, m_page)
             + jnp.einsum('bqhd,qn->bhnd', k.astype(F32), m_new)) / MOBA_BLOCK
    qh = q.transpose(0, 2, 1, 3)
    idx, ok = _moba_select(qh, means, pos_q)
    own = pos_q // MOBA_BLOCK
    offs = jnp.arange(MOBA_BLOCK)
    hi = jnp.arange(H)[:, None, None, None]
    valid_sel = jnp.broadcast_to(ok[None, :, :, None], (H, Q, MOBA_TOPK, MOBA_BLOCK))
    scale = HEAD_DIM ** -0.5

    def one_seq(args):
        pt, kn, vn, qb, ib = args
        blocks = jnp.concatenate([ib, jnp.broadcast_to(own[None, :, None], (H, Q, 1)).astype(ib.dtype)], axis=-1)
        pos = blocks[..., None] * MOBA_BLOCK + offs
        in_cache = (pos < past)[..., None]
        pc = jnp.clip(pos, 0, past - 1)
        phys = pt[pc // PAGE_SIZE]
        off = pc % PAGE_SIZE
        pn = jnp.clip(pos - past, 0, Q - 1)
        kk = jnp.where(in_cache, pool_k[phys, off, hi], kn[pn, hi])
        vv = jnp.where(in_cache, pool_v[phys, off, hi], vn[pn, hi])
        s = jnp.einsum('hqd,hqnsd->hqns', qb, kk, preferred_element_type=F32) * scale
        s = s - slopes[:, None, None, None] * (pos_q[None, :, None, None] - pos).astype(F32)
        valid = jnp.concatenate([valid_sel, pos[:, :, MOBA_TOPK:, :] <= pos_q[None, :, None, None]], axis=2)
        s = jnp.where(valid, s, NEG_INF).reshape(H, Q, -1)
        p = jax.nn.softmax(s, axis=-1).astype(vv.dtype)
        o = jnp.einsum('hqs,hqsd->hqd', p, vv.reshape(H, Q, -1, D), preferred_element_type=F32)
        return o.astype(q.dtype)

    out = lax.map(one_seq, (page_table, k, v, qh, idx))
    return out.transpose(0, 2, 1, 3).reshape(DB, Q, H * D)


def _gla_chunked(q, k, v, log_a, s0):
    B, T, H, DK = q.shape
    DV = v.shape[-1]
    C = GLA_CHUNK
    n_c = -(-T // C)
    pad = n_c * C - T

    def chunks(a):
        a = jnp.pad(a.astype(F32), ((0, 0), (0, pad), (0, 0), (0, 0)))
        return a.reshape(B, n_c, C, H, a.shape[-1])

    qc, kc, vc, lc = chunks(q), chunks(k), chunks(v), chunks(log_a)
    b = jnp.cumsum(lc, axis=2)
    b_end = b[:, :, -1]
    q_dec = qc * jnp.exp(b)
    k_inv = kc * jnp.exp(-b)
    k_end = kc * jnp.exp(b_end[:, :, None] - b)
    causal = jnp.tril(jnp.ones((C, C), dtype=bool))
    a_intra = jnp.where(causal, jnp.einsum('bnthk,bnshk->bnhts', q_dec, k_inv), 0.0)
    o = jnp.einsum('bnhts,bnshv->bnthv', a_intra, vc)
    ds = jnp.einsum('bnshk,bnshv->bnhkv', k_end, vc)

    def step(s, inp):
        decay, d_s = inp
        return decay[..., None] * s + d_s, s

    s_fin, s_prev = lax.scan(step, s0.astype(F32), (jnp.moveaxis(jnp.exp(b_end), 1, 0), jnp.moveaxis(ds, 1, 0)))
    o = o + jnp.einsum('bnthk,nbhkv->bnthv', q_dec, s_prev)
    return o.reshape(B, n_c * C, H, DV)[:, :T], s_fin


def _gla_mixer(g, s0, w_a2, b_a, norm_w):
    gq, gk, gv, ga, gg = g
    B, T = gq.shape[:2]
    q = gq.reshape(B, T, N_GLA_HEADS, GLA_DK) * (GLA_DK ** -0.5)
    k = gk.reshape(B, T, N_GLA_HEADS, GLA_DK)
    v = gv.reshape(B, T, N_GLA_HEADS, GLA_DV)
    log_a = jax.nn.log_sigmoid(jnp.einsum('btr,rk->btk', ga, w_a2).astype(F32) + b_a.astype(F32)) / GLA_TAU
    log_a = log_a.reshape(B, T, N_GLA_HEADS, GLA_DK)
    o, s = _gla_chunked(q, k, v, log_a, s0)
    o = _rmsnorm(o, norm_w).astype(gg.dtype) * jax.nn.silu(gg.reshape(B, T, N_GLA_HEADS, GLA_DV))
    return o.reshape(B, T, GLA_VW), s.astype(s0.dtype)


def _expert_blocks(h, expert, gate, w_e_gate, w_e_up, w_e_down):
    N, D = h.shape
    A = expert.shape[0]
    tok = jnp.arange(A) // EXPERT_TOPK
    order = jnp.argsort(expert)
    e_sorted = expert[order]
    counts = jnp.bincount(expert, length=N_EXPERTS)
    padded = (counts + MOE_BLOCK - 1) // MOE_BLOCK * MOE_BLOCK
    start = jnp.cumsum(counts) - counts
    ends_p = jnp.cumsum(padded)
    pstart = ends_p - padded
    dest = pstart[e_sorted] + jnp.arange(A) - start[e_sorted]
    n_blocks = -(-A // MOE_BLOCK) + N_EXPERTS
    cap = n_blocks * MOE_BLOCK
    slot_tok = jnp.full((cap,), N, jnp.int32).at[dest].set(tok[order])
    slot_gate = jnp.zeros((cap,), F32).at[dest].set(gate[order].astype(F32))
    block_expert = jnp.minimum(jnp.searchsorted(ends_p, jnp.arange(n_blocks) * MOE_BLOCK, side='right'), N_EXPERTS - 1)
    h_pad = jnp.concatenate([h, jnp.zeros((1, D), h.dtype)], axis=0)
    xs = h_pad[slot_tok].reshape(n_blocks, MOE_BLOCK, D)

    def run(args):
        xb, e = args
        a = xb @ w_e_gate[e]
        u = xb @ w_e_up[e]
        return (jax.nn.silu(a) * u) @ w_e_down[e]

    ys = lax.map(run, (xs, block_expert)).reshape(cap, D)
    out = jnp.zeros((N + 1, D), F32).at[slot_tok].add(ys.astype(F32) * slot_gate[:, None])
    return out[:N].astype(h.dtype)


def _hier_moe(x, norm_w, w_r1, b_r1, w_r2, b_r2, w_e_gate, w_e_up, w_e_down):
    B, T, D = x.shape
    h = _rmsnorm(x, norm_w).reshape(B * T, D)
    N = B * T
    p_group = jax.nn.softmax((h @ w_r1 + b_r1).astype(F32), axis=-1)
    g_top, g_idx = lax.top_k(p_group, 1)
    lg2 = (h @ w_r2 + b_r2).astype(F32).reshape(N, N_GROUPS, EXPERTS_PER_GROUP)
    lg2 = jnp.take_along_axis(lg2, g_idx[:, :, None], axis=1)[:, 0]
    l_top, e_idx = lax.top_k(lg2, EXPERT_TOPK)
    gate = g_top * jax.nn.softmax(l_top, axis=-1)
    expert = g_idx * EXPERTS_PER_GROUP + e_idx
    y = _expert_blocks(h, expert.reshape(-1), gate.reshape(-1), w_e_gate, w_e_up, w_e_down)
    return y.reshape(B, T, D)


def _block_tail(x, att, gla, w_out, norm2_w, w_r1, b_r1, w_r2, b_r2, w_e_gate, w_e_up, w_e_down):
    h = x + jnp.einsum('bte,ed->btd', jnp.concatenate([att, gla], axis=-1), w_out)
    return h + _hier_moe(h, norm2_w, w_r1, b_r1, w_r2, b_r2, w_e_gate, w_e_up, w_e_down)


def setup_inputs(seed: int = 0) -> dict:
    key = jax.random.key(seed)
    ks = jax.random.split(key, 24)
    n_pages = PAST_LEN // PAGE_SIZE
    n_pool = (5 * DEC_BATCH * n_pages + 3) // 4

    def nrm(k, shape, scale):
        return jax.random.normal(k, shape, F32) * scale

    def gain(k, shape):
        return 1.0 + 0.02 * jax.random.normal(k, shape, F32)

    page_table = jax.random.permutation(ks[5], n_pool)[:DEC_BATCH * n_pages].reshape(DEC_BATCH, n_pages).astype(jnp.int32)
    return {
        'x_prompt': nrm(ks[0], (BATCH, SEQ, D_MODEL), 1.0),
        'x_sample': nrm(ks[1], (DEC_BATCH, DEC_SEQ, D_MODEL), 1.0),
        'cache_k': nrm(ks[2], (DEPTH, n_pool, PAGE_SIZE, N_ATT_HEADS, HEAD_DIM), 1.0),
        'cache_v': nrm(ks[3], (DEPTH, n_pool, PAGE_SIZE, N_ATT_HEADS, HEAD_DIM), 1.0),
        'state_gla': nrm(ks[4], (DEPTH, DEC_BATCH, N_GLA_HEADS, GLA_DK, GLA_DV), 2.0),
        'page_table': page_table,
        'norm1_w': gain(ks[6], (DEPTH, D_MODEL)),
        'w_in': nrm(ks[7], (DEPTH, D_MODEL, IN_WIDTH), D_MODEL ** -0.5),
        'q_norm_w': gain(ks[8], (DEPTH, HEAD_DIM)),
        'k_norm_w': gain(ks[9], (DEPTH, HEAD_DIM)),
        'w_gla_a2': nrm(ks[10], (DEPTH, GLA_GATE_RANK, GLA_KW), GLA_GATE_RANK ** -0.5),
        'b_gla_a': nrm(ks[11], (DEPTH, GLA_KW), 0.1),
        'gla_norm_w': gain(ks[12], (DEPTH, GLA_DV)),
        'w_out': nrm(ks[13], (DEPTH, MIX_WIDTH, D_MODEL), MIX_WIDTH ** -0.5),
        'norm2_w': gain(ks[14], (DEPTH, D_MODEL)),
        'w_r1': nrm(ks[15], (DEPTH, D_MODEL, N_GROUPS), D_MODEL ** -0.5),
        'b_r1': nrm(ks[16], (DEPTH, N_GROUPS), 0.01),
        'w_r2': nrm(ks[17], (DEPTH, D_MODEL, N_EXPERTS), D_MODEL ** -0.5),
        'b_r2': nrm(ks[18], (DEPTH, N_EXPERTS), 0.01),
        'w_e_gate': nrm(ks[19], (DEPTH, N_EXPERTS, D_MODEL, D_FF_EXPERT), D_MODEL ** -0.5),
        'w_e_up': nrm(ks[20], (DEPTH, N_EXPERTS, D_MODEL, D_FF_EXPERT), D_MODEL ** -0.5),
        'w_e_down': nrm(ks[21], (DEPTH, N_EXPERTS, D_FF_EXPERT, D_MODEL), D_FF_EXPERT ** -0.5),
    }


def reference(x_prompt, x_sample, cache_k, cache_v, state_gla, page_table,
              norm1_w, w_in, q_norm_w, k_norm_w, w_gla_a2, b_gla_a, gla_norm_w, w_out,
              norm2_w, w_r1, b_r1, w_r2, b_r2, w_e_gate, w_e_up, w_e_down):
    slopes = _alibi_slopes()
    xp, xs = x_prompt, x_sample
    kp_l, vp_l, sp_l, ks_l, vs_l, ss_l = [], [], [], [], [], []
    for l in range(DEPTH):
        ffn = (norm2_w[l], w_r1[l], b_r1[l], w_r2[l], b_r2[l], w_e_gate[l], w_e_up[l], w_e_down[l])
        q, k, v, g = _project(xp, norm1_w[l], w_in[l], q_norm_w[l], k_norm_w[l])
        att = _moba_prompt(q, k, v, slopes)
        s0 = jnp.zeros((xp.shape[0], N_GLA_HEADS, GLA_DK, GLA_DV), state_gla.dtype)
        gla, s_p = _gla_mixer(g, s0, w_gla_a2[l], b_gla_a[l], gla_norm_w[l])
        xp = _block_tail(xp, att, gla, w_out[l], *ffn)
        kp_l.append(k)
        vp_l.append(v)
        sp_l.append(s_p)
        q, k, v, g = _project(xs, norm1_w[l], w_in[l], q_norm_w[l], k_norm_w[l])
        att = _moba_sample(q, k, v, cache_k[l], cache_v[l], page_table, slopes)
        gla, s_s = _gla_mixer(g, state_gla[l], w_gla_a2[l], b_gla_a[l], gla_norm_w[l])
        xs = _block_tail(xs, att, gla, w_out[l], *ffn)
        ks_l.append(k)
        vs_l.append(v)
        ss_l.append(s_s)
    return (xp, xs, jnp.stack(kp_l), jnp.stack(vp_l), jnp.stack(sp_l), jnp.stack(ks_l), jnp.stack(vs_l), jnp.stack(ss_l))
```

```python
import functools

import jax
import jax.numpy as jnp
from jax import lax
from jax.experimental import pallas as pl
from jax.experimental.pallas import tpu as pltpu

F32 = jnp.float32
BF16 = jnp.bfloat16
I32 = jnp.int32

HEAD_DIM = 128
N_ATT_HEADS = 8
ATT_WIDTH = N_ATT_HEADS * HEAD_DIM
MOBA_BLOCK = 256
MOBA_TOPK = 3
PAGE_SIZE = 128
PAGES_PER_BLOCK = MOBA_BLOCK // PAGE_SIZE
N_GLA_HEADS = 4
GLA_DK = 128
GLA_DV = 256
GLA_KW = N_GLA_HEADS * GLA_DK
GLA_VW = N_GLA_HEADS * GLA_DV
GLA_GATE_RANK = 16
GLA_TAU = 16.0
GLA_CHUNK = 64
N_GROUPS = 4
EXPERTS_PER_GROUP = 8
N_EXPERTS = N_GROUPS * EXPERTS_PER_GROUP
EXPERT_TOPK = 2
RMS_EPS = 1e-6
NEG_INF = -1e30
LANES = 128
EXPERT_ROWS = 256
VMEM_LIMIT = 56 * 1024 * 1024

_NT = (((1,), (1,)), ((), ()))
_TN = (((0,), (0,)), ((), ()))
_HI = lax.Precision.HIGHEST


def _dot(a, b):
    return jnp.dot(a, b, preferred_element_type=F32)


def _dot_nt(a, b):
    return lax.dot_general(a, b, _NT, preferred_element_type=F32)


def _dot_tn(a, b):
    return lax.dot_general(a, b, _TN, preferred_element_type=F32)


def _params(*sem):
    return pltpu.CompilerParams(dimension_semantics=sem, vmem_limit_bytes=VMEM_LIMIT)


def _top3(gate, col, n_valid):
    picks = []
    g = gate
    for r in range(MOBA_TOPK):
        m = jnp.max(g, axis=-1, keepdims=True)
        idx = jnp.min(jnp.where(g == m, col, jnp.int32(2 ** 30)), axis=-1, keepdims=True)
        g = jnp.where(col == idx, -jnp.inf, g)
        picks.append(jnp.where(r < n_valid, idx, -1))
    return picks


def _proj_kernel(x_ref, n1_ref, w_ref, wga_ref, qn_ref, kn_ref,
                 q_ref, k_ref, kb_ref, v_ref, vb_ref, gqk_ref, gv_ref, gg_ref, ga_ref, xn_ref):
    j = pl.program_id(1)

    @pl.when(j == 0)
    def _():
        x = x_ref[...]
        y = x * lax.rsqrt(jnp.mean(x * x, axis=-1, keepdims=True) + RMS_EPS) * n1_ref[...]
        xn_ref[...] = y.astype(BF16)
        ga_ref[...] = _dot(xn_ref[...], wga_ref[...])

    z = _dot(xn_ref[...], w_ref[...])

    def head_norm(w):
        outs = []
        for h in range(N_ATT_HEADS):
            zh = z[:, h * HEAD_DIM:(h + 1) * HEAD_DIM]
            outs.append(zh * lax.rsqrt(jnp.mean(zh * zh, axis=-1, keepdims=True) + RMS_EPS) * w)
        return outs

    @pl.when(j == 0)
    def _():
        for h, y in enumerate(head_norm(qn_ref[...])):
            q_ref[:, h * HEAD_DIM:(h + 1) * HEAD_DIM] = y

    @pl.when(j == 1)
    def _():
        for h, y in enumerate(head_norm(kn_ref[...])):
            k_ref[:, h * HEAD_DIM:(h + 1) * HEAD_DIM] = y
            kb_ref[:, h * HEAD_DIM:(h + 1) * HEAD_DIM] = y.astype(BF16)

    @pl.when(j == 2)
    def _():
        v_ref[...] = z
        vb_ref[...] = z.astype(BF16)

    @pl.when(j == 3)
    def _():
        gqk_ref[...] = z

    @pl.when(j == 4)
    def _():
        gv_ref[...] = z

    @pl.when(j == 5)
    def _():
        gg_ref[...] = z


def _project(x, n1, w_main, w_ga, qn, kn, tm):
    n, d = x.shape
    wide = ATT_WIDTH
    row = lambda i, j: (i, 0)
    out_shape = [
        jax.ShapeDtypeStruct((n, wide), F32),
        jax.ShapeDtypeStruct((n, wide), F32),
        jax.ShapeDtypeStruct((n, wide), BF16),
        jax.ShapeDtypeStruct((n, wide), F32),
        jax.ShapeDtypeStruct((n, wide), BF16),
        jax.ShapeDtypeStruct((n, wide), F32),
        jax.ShapeDtypeStruct((n, wide), F32),
        jax.ShapeDtypeStruct((n, wide), F32),
        jax.ShapeDtypeStruct((n, LANES), F32),
    ]
    out_specs = [pl.BlockSpec((tm, s.shape[1]), row) for s in out_shape]
    return pl.pallas_call(
        _proj_kernel,
        grid=(n // tm, 6),
        in_specs=[
            pl.BlockSpec((tm, d), row),
            pl.BlockSpec((1, d), lambda i, j: (0, 0)),
            pl.BlockSpec((d, wide), lambda i, j: (0, j)),
            pl.BlockSpec((d, LANES), lambda i, j: (0, 0)),
            pl.BlockSpec((1, HEAD_DIM), lambda i, j: (0, 0)),
            pl.BlockSpec((1, HEAD_DIM), lambda i, j: (0, 0)),
        ],
        out_specs=out_specs,
        out_shape=out_shape,
        scratch_shapes=[pltpu.VMEM((tm, d), BF16)],
        compiler_params=_params("parallel", "arbitrary"),
        name="projection",
    )(x, n1, w_main, w_ga, qn, kn)


def _kmeans_kernel(k_ref, o_ref):
    rows = k_ref.shape[0]
    k = k_ref[...].reshape(rows // MOBA_BLOCK, MOBA_BLOCK, k_ref.shape[1])
    o_ref[...] = jnp.sum(k, axis=1) * (1.0 / MOBA_BLOCK)


def _block_means(k):
    n, w = k.shape
    rows = 8 * MOBA_BLOCK
    return pl.pallas_call(
        _kmeans_kernel,
        grid=(n // rows,),
        in_specs=[pl.BlockSpec((rows, w), lambda i: (i, 0))],
        out_specs=pl.BlockSpec((8, w), lambda i: (i, 0)),
        out_shape=jax.ShapeDtypeStruct((n // MOBA_BLOCK, w), F32),
        compiler_params=_params("parallel"),
        name="block_means",
    )(k)


def _moba_prompt_kernel(slopes_ref, q_ref, k_ref, v_ref, km_ref, o_ref):
    h = pl.program_id(1)
    i = pl.program_id(2)
    slope = slopes_ref[h]
    scale = HEAD_DIM ** -0.5
    bs = MOBA_BLOCK
    q = q_ref[...]
    qb = q.astype(BF16)
    nb = km_ref.shape[0]

    gate = lax.dot_general(q, km_ref[...], _NT, precision=_HI, preferred_element_type=F32)
    col = lax.broadcasted_iota(I32, (bs, nb), 1)
    gate = jnp.where(col < i, gate, NEG_INF)
    i0, i1, i2 = _top3(gate, col, i)

    rel = (lax.broadcasted_iota(I32, (bs, bs), 0) - lax.broadcasted_iota(I32, (bs, bs), 1))
    relf = rel.astype(F32)

    r0 = pl.multiple_of(i * bs, bs)
    s = _dot_nt(qb, k_ref[pl.ds(r0, bs), :]) * scale - slope * relf
    s = jnp.where(rel >= 0, s, NEG_INF)
    m = jnp.max(s, axis=-1, keepdims=True)
    p = jnp.exp(s - m)
    l = jnp.sum(p, axis=-1, keepdims=True)
    acc = _dot(p.astype(BF16), v_ref[pl.ds(r0, bs), :])

    def body(j, carry):
        m, l, acc = carry
        rj = pl.multiple_of(j * bs, bs)
        dist = relf + ((i - j) * bs).astype(F32)
        s = _dot_nt(qb, k_ref[pl.ds(rj, bs), :]) * scale - slope * dist
        sel = (i0 == j) | (i1 == j) | (i2 == j)
        s = jnp.where(sel, s, NEG_INF)
        m_new = jnp.maximum(m, jnp.max(s, axis=-1, keepdims=True))
        alpha = jnp.exp(m - m_new)
        p = jnp.exp(s - m_new)
        l = alpha * l + jnp.sum(p, axis=-1, keepdims=True)
        acc = alpha * acc + _dot(p.astype(BF16), v_ref[pl.ds(rj, bs), :])
        return m_new, l, acc

    m, l, acc = lax.fori_loop(0, i, body, (m, l, acc))
    o_ref[...] = (acc / l).astype(o_ref.dtype)


def _moba_prompt(slopes, q, kb, vb, kmeans, batch, seq):
    n = q.shape[0]
    nb = seq // MOBA_BLOCK
    bs = MOBA_BLOCK
    return pl.pallas_call(
        _moba_prompt_kernel,
        grid=(batch, N_ATT_HEADS, nb),
        in_specs=[
            pl.BlockSpec(memory_space=pltpu.SMEM),
            pl.BlockSpec((bs, HEAD_DIM), lambda b, h, i: (b * nb + i, h)),
            pl.BlockSpec((seq, HEAD_DIM), lambda b, h, i: (b, h)),
            pl.BlockSpec((seq, HEAD_DIM), lambda b, h, i: (b, h)),
            pl.BlockSpec((nb, HEAD_DIM), lambda b, h, i: (b, h)),
        ],
        out_specs=pl.BlockSpec((bs, HEAD_DIM), lambda b, h, i: (b * nb + i, h)),
        out_shape=jax.ShapeDtypeStruct((n, ATT_WIDTH), BF16),
        compiler_params=_params("parallel", "parallel", "arbitrary"),
        name="moba_prompt",
    )(slopes, q, kb, vb, kmeans)


def _page_sum_kernel(pt_ref, kc_hbm, o_ref, buf, sem, *, group, n_chunks):
    c = pl.program_id(0)

    def copy(cc, slot, g):
        page = pt_ref[cc * group + g]
        return pltpu.make_async_copy(kc_hbm.at[page], buf.at[slot, g], sem.at[slot])

    def start(cc, slot):
        for g in range(group):
            copy(cc, slot, g).start()

    @pl.when(c == 0)
    def _():
        start(0, 0)

    @pl.when(c + 1 < n_chunks)
    def _():
        start(c + 1, (c + 1) % 2)

    slot = c % 2
    for g in range(group):
        copy(c, slot, g).wait()
    for bk in range(group // PAGES_PER_BLOCK):
        acc = jnp.sum(buf[slot, PAGES_PER_BLOCK * bk], axis=0)
        for p in range(1, PAGES_PER_BLOCK):
            acc = acc + jnp.sum(buf[slot, PAGES_PER_BLOCK * bk + p], axis=0)
        o_ref[bk] = acc


def _page_block_sums(page_table, cache_k):
    db, n_pages = page_table.shape
    group = 8
    n_chunks = db * n_pages // group
    bpc = group // PAGES_PER_BLOCK
    grid_spec = pltpu.PrefetchScalarGridSpec(
        num_scalar_prefetch=1,
        grid=(n_chunks,),
        in_specs=[pl.BlockSpec(memory_space=pl.ANY)],
        out_specs=pl.BlockSpec((bpc, N_ATT_HEADS, HEAD_DIM), lambda c, pt: (c, 0, 0)),
        scratch_shapes=[
            pltpu.VMEM((2, group, PAGE_SIZE, N_ATT_HEADS, HEAD_DIM), F32),
            pltpu.SemaphoreType.DMA((2,)),
        ],
    )
    return pl.pallas_call(
        functools.partial(_page_sum_kernel, group=group, n_chunks=n_chunks),
        grid_spec=grid_spec,
        out_shape=jax.ShapeDtypeStruct((n_chunks * bpc, N_ATT_HEADS, HEAD_DIM), F32),
        compiler_params=_params("arbitrary"),
        name="page_block_sums",
    )(page_table.reshape(-1), cache_k)


def _sample_select_kernel(q_ref, bs_ref, kn_ref, o_ref, *, n_past_blocks, past):
    nq = q_ref.shape[0]
    col = lax.broadcasted_iota(I32, (nq, LANES), 1)
    lane = col
    n_full = (past + lax.broadcasted_iota(I32, (nq, 1), 0)) // MOBA_BLOCK
    for h in range(N_ATT_HEADS):
        sl = slice(h * HEAD_DIM, (h + 1) * HEAD_DIM)
        m_past = bs_ref[:, sl] * (1.0 / MOBA_BLOCK)
        m_new = jnp.sum(kn_ref[:, sl], axis=0, keepdims=True) * (1.0 / MOBA_BLOCK)
        row8 = lax.broadcasted_iota(I32, (8, HEAD_DIM), 0)
        new_rows = jnp.where(row8 == 0, jnp.broadcast_to(m_new, (8, HEAD_DIM)), 0.0)
        means = jnp.concatenate(
            [m_past, new_rows, jnp.zeros((LANES - n_past_blocks - 8, HEAD_DIM), F32)], axis=0)
        gate = lax.dot_general(q_ref[:, sl], means, _NT, precision=_HI, preferred_element_type=F32)
        gate = jnp.where(col < n_full, gate, NEG_INF)
        i0, i1, i2 = _top3(gate, col, n_full)
        o_ref[0, h] = jnp.where(lane == 0, i0, jnp.where(lane == 1, i1, jnp.where(lane == 2, i2, 0)))


def _sample_select(q, block_sums, k_new, db, nq, n_past_blocks):
    return pl.pallas_call(
        functools.partial(_sample_select_kernel, n_past_blocks=n_past_blocks,
                          past=n_past_blocks * MOBA_BLOCK),
        grid=(db,),
        in_specs=[
            pl.BlockSpec((nq, ATT_WIDTH), lambda s: (s, 0)),
            pl.BlockSpec((n_past_blocks, ATT_WIDTH), lambda s: (s, 0)),
            pl.BlockSpec((nq, ATT_WIDTH), lambda s: (s, 0)),
        ],
        out_specs=pl.BlockSpec((1, N_ATT_HEADS, nq, LANES), lambda s: (s, 0, 0, 0)),
        out_shape=jax.ShapeDtypeStruct((db, N_ATT_HEADS, nq, LANES), I32),
        compiler_params=_params("parallel"),
        name="sample_select",
    )(q, block_sums, k_new)


def _moba_sample_kernel(pt_ref, idx_ref, slopes_ref, q_ref, kn_ref, vn_ref, kc_hbm, vc_hbm, o_ref,
                        kbuf, vbuf, kown, vown, sem, *, n_steps, n_pages, nq, past):
    t = pl.program_id(0)
    nh = N_ATT_HEADS
    n_sel = nq * MOBA_TOPK
    n_slots = n_sel * PAGES_PER_BLOCK

    def copies(tt, slot, g):
        s = tt // nh
        h = tt % nh
        blk = idx_ref[tt * n_sel + g // PAGES_PER_BLOCK]
        page = pt_ref[s * n_pages + blk * PAGES_PER_BLOCK + g % PAGES_PER_BLOCK]
        return (pltpu.make_async_copy(kc_hbm.at[page, :, h, :], kbuf.at[slot, g], sem.at[0, slot]),
                pltpu.make_async_copy(vc_hbm.at[page, :, h, :], vbuf.at[slot, g], sem.at[1, slot]))

    def start(tt, slot):
        for g in range(n_slots):
            ck, cv = copies(tt, slot, g)
            ck.start()
            cv.start()

    @pl.when(t == 0)
    def _():
        start(0, 0)
        kown[...] = jnp.zeros_like(kown)
        vown[...] = jnp.zeros_like(vown)

    @pl.when(t + 1 < n_steps)
    def _():
        start(t + 1, (t + 1) % 2)

    slot = t % 2
    h = t % nh
    slope = slopes_ref[h]
    scale = HEAD_DIM ** -0.5
    qb = q_ref[...].astype(BF16)
    n_keys = n_slots * PAGE_SIZE
    keys_per_q = MOBA_TOPK * MOBA_BLOCK

    kown[0:nq, :] = kn_ref[...]
    vown[0:nq, :] = vn_ref[...]
    row = lax.broadcasted_iota(I32, (nq, LANES), 0)
    colo = lax.broadcasted_iota(I32, (nq, LANES), 1)
    s_own = _dot_nt(qb, kown[...].astype(BF16)) * scale - slope * (row - colo).astype(F32)
    s_own = jnp.where(colo <= row, s_own, NEG_INF)

    colk = lax.broadcasted_iota(I32, (1, n_keys), 1)
    grp = colk // MOBA_BLOCK
    blk_of_col = jnp.zeros((1, n_keys), I32)
    for g in range(n_sel):
        blk_of_col = jnp.where(grp == g, idx_ref[t * n_sel + g], blk_of_col)
    pos = blk_of_col * MOBA_BLOCK + colk % MOBA_BLOCK
    rowk = lax.broadcasted_iota(I32, (nq, n_keys), 0)
    mine = (lax.broadcasted_iota(I32, (nq, n_keys), 1) // keys_per_q) == rowk
    n_full = (past + rowk) // MOBA_BLOCK
    pick = (lax.broadcasted_iota(I32, (nq, n_keys), 1) // MOBA_BLOCK) % MOBA_TOPK
    valid = mine & (pick < n_full)

    for g in range(n_slots):
        ck, cv = copies(t, slot, g)
        ck.wait()
        cv.wait()

    kall = kbuf[slot].reshape(n_keys, HEAD_DIM).astype(BF16)
    s_sel = _dot_nt(qb, kall) * scale - slope * ((past + rowk) - pos).astype(F32)
    s_sel = jnp.where(valid, s_sel, NEG_INF)
    m = jnp.maximum(jnp.max(s_sel, axis=-1, keepdims=True), jnp.max(s_own, axis=-1, keepdims=True))
    p_sel = jnp.exp(s_sel - m)
    p_own = jnp.exp(s_own - m)
    l = jnp.sum(p_sel, axis=-1, keepdims=True) + jnp.sum(p_own, axis=-1, keepdims=True)
    vall = vbuf[slot].reshape(n_keys, HEAD_DIM).astype(BF16)
    o = _dot(p_sel.astype(BF16), vall) + _dot(p_own.astype(BF16), vown[...].astype(BF16))
    o_ref[...] = o / l


def _moba_sample(page_table, idx, slopes, q, k_new, v_new, cache_k, cache_v, db, nq):
    n_pages = page_table.shape[1]
    n_steps = db * N_ATT_HEADS
    n_slots = nq * MOBA_TOPK * PAGES_PER_BLOCK
    qspec = pl.BlockSpec((nq, HEAD_DIM), lambda t, pt, ix: (t // N_ATT_HEADS, t % N_ATT_HEADS))
    grid_spec = pltpu.PrefetchScalarGridSpec(
        num_scalar_prefetch=2,
        grid=(n_steps,),
        in_specs=[
            pl.BlockSpec(memory_space=pltpu.SMEM),
            qspec, qspec, qspec,
            pl.BlockSpec(memory_space=pl.ANY),
            pl.BlockSpec(memory_space=pl.ANY),
        ],
        out_specs=qspec,
        scratch_shapes=[
            pltpu.VMEM((2, n_slots, PAGE_SIZE, HEAD_DIM), F32),
            pltpu.VMEM((2, n_slots, PAGE_SIZE, HEAD_DIM), F32),
            pltpu.VMEM((LANES, HEAD_DIM), F32),
            pltpu.VMEM((LANES, HEAD_DIM), F32),
            pltpu.SemaphoreType.DMA((2, 2)),
        ],
    )
    return pl.pallas_call(
        functools.partial(_moba_sample_kernel, n_steps=n_steps, n_pages=n_pages, nq=nq,
                          past=n_pages * PAGE_SIZE),
        grid_spec=grid_spec,
        out_shape=jax.ShapeDtypeStruct((db * nq, ATT_WIDTH), F32),
        compiler_params=_params("arbitrary"),
        name="moba_sample",
    )(page_table.reshape(-1), idx.reshape(-1), slopes, q, k_new, v_new, cache_k, cache_v)


def _gla_kernel(gq_ref, gk_ref, gv_ref, gg_ref, ga_ref, wa_ref, ba_ref, nw_ref, s0_ref,
                o_ref, sfin_ref, s_scr, *, chunk, n_sub, t_valid):
    t = pl.program_id(2)

    @pl.when(t == 0)
    def _():
        s_scr[...] = s0_ref[0, 0]

    c = chunk
    rowc = lax.broadcasted_iota(I32, (c, c), 0)
    colc = lax.broadcasted_iota(I32, (c, c), 1)
    tri = rowc >= colc
    trif = tri.astype(F32)
    ones = jnp.ones((c, GLA_DK), F32)
    wa = wa_ref[...]
    ba = ba_ref[...]
    nw = nw_ref[...]

    def step(ci, carry):
        r0 = pl.multiple_of(ci * c, c)
        q = gq_ref[pl.ds(r0, c), :] * (GLA_DK ** -0.5)
        k = gk_ref[pl.ds(r0, c), :]
        v = gv_ref[pl.ds(r0, c), :].astype(BF16)
        pre = jnp.dot(ga_ref[pl.ds(r0, c), :], wa, precision=_HI, preferred_element_type=F32) + ba
        la = (jnp.minimum(pre, 0.0) - jnp.log1p(jnp.exp(-jnp.abs(pre)))) * (1.0 / GLA_TAU)
        if t_valid < c:
            la = jnp.where(lax.broadcasted_iota(I32, la.shape, 0) < t_valid, la, 0.0)
        b = jnp.dot(trif, la, precision=_HI, preferred_element_type=F32)
        b_end = b[c - 1:c, :]
        b_end_col = lax.dot_general(la, ones, _TN, precision=_HI, preferred_element_type=F32)
        q_dec = (q * jnp.exp(b)).astype(BF16)
        k_inv = (k * jnp.exp(-b)).astype(BF16)
        k_end = (k * jnp.exp(b_end - b)).astype(BF16)
        a = jnp.where(tri, _dot_nt(q_dec, k_inv), 0.0)
        s_prev = s_scr[...]
        o = _dot(a.astype(BF16), v) + _dot(q_dec, s_prev.astype(BF16))
        ds = _dot_tn(k_end, v)
        decay = jnp.exp(b_end_col)
        s_scr[...] = jnp.concatenate([decay] * (GLA_DV // GLA_DK), axis=1) * s_prev + ds
        on = o * lax.rsqrt(jnp.mean(o * o, axis=-1, keepdims=True) + RMS_EPS) * nw
        g = gg_ref[pl.ds(r0, c), :]
        o_ref[pl.ds(r0, c), :] = (on * (g * jax.nn.sigmoid(g))).astype(o_ref.dtype)
        return carry

    lax.fori_loop(0, n_sub, step, 0)

    @pl.when(t == pl.num_programs(2) - 1)
    def _():
        sfin_ref[0, 0] = s_scr[...]


def _gla(gqk, gv, gg, ga, wa, ba, nw, s0, batch, seq, t_valid):
    n = gqk.shape[0]
    c = GLA_CHUNK
    n_sub = min(8, seq // c)
    rows = c * n_sub
    steps = seq // rows
    nh = N_GLA_HEADS
    return pl.pallas_call(
        functools.partial(_gla_kernel, chunk=c, n_sub=n_sub, t_valid=t_valid),
        grid=(batch, nh, steps),
        in_specs=[
            pl.BlockSpec((rows, GLA_DK), lambda b, h, t: (b * steps + t, h)),
            pl.BlockSpec((rows, GLA_DK), lambda b, h, t: (b * steps + t, nh + h)),
            pl.BlockSpec((rows, GLA_DV), lambda b, h, t: (b * steps + t, h)),
            pl.BlockSpec((rows, GLA_DV), lambda b, h, t: (b * steps + t, h)),
            pl.BlockSpec((rows, LANES), lambda b, h, t: (b * steps + t, 0)),
            pl.BlockSpec((LANES, GLA_DK), lambda b, h, t: (0, h)),
            pl.BlockSpec((1, GLA_DK), lambda b, h, t: (0, h)),
            pl.BlockSpec((1, GLA_DV), lambda b, h, t: (0, 0)),
            pl.BlockSpec((1, 1, GLA_DK, GLA_DV), lambda b, h, t: (b, h, 0, 0)),
        ],
        out_specs=[
            pl.BlockSpec((rows, GLA_DV), lambda b, h, t: (b * steps + t, h)),
            pl.BlockSpec((1, 1, GLA_DK, GLA_DV), lambda b, h, t: (b, h, 0, 0)),
        ],
        out_shape=[
            jax.ShapeDtypeStruct((n, GLA_VW), BF16),
            jax.ShapeDtypeStruct(s0.shape, F32),
        ],
        scratch_shapes=[pltpu.VMEM((GLA_DK, GLA_DV), F32)],
        compiler_params=_params("parallel", "parallel", "arbitrary"),
        name="gla",
    )(gqk, gqk, gv, gg, ga, wa, ba, nw, s0)


def _out_router_kernel(att_ref, gla_ref, x_ref, woa_ref, wog_ref, n2_ref, wr_ref, br_ref,
                       h_ref, hn_ref, re_ref, rg_ref):
    h = x_ref[...] + _dot(att_ref[...], woa_ref[...]) + _dot(gla_ref[...], wog_ref[...])
    h_ref[...] = h
    hn = h * lax.rsqrt(jnp.mean(h * h, axis=-1, keepdims=True) + RMS_EPS) * n2_ref[...]
    hn_ref[...] = hn
    logits = jnp.dot(hn, wr_ref[...], precision=_HI, preferred_element_type=F32) + br_ref[...]
    col = lax.broadcasted_iota(I32, logits.shape, 1)
    big = jnp.int32(2 ** 30)
    in_g = col < N_GROUPS
    lg = jnp.where(in_g, logits, -jnp.inf)
    m1 = jnp.max(lg, axis=-1, keepdims=True)
    g_idx = jnp.min(jnp.where(lg == m1, col, big), axis=-1, keepdims=True)
    g_top = 1.0 / jnp.sum(jnp.exp(lg - m1), axis=-1, keepdims=True)
    ecol = col - N_GROUPS
    in_e = (ecol >= g_idx * EXPERTS_PER_GROUP) & (ecol < (g_idx + 1) * EXPERTS_PER_GROUP)
    le = jnp.where(in_e, logits, -jnp.inf)
    l1 = jnp.max(le, axis=-1, keepdims=True)
    e1 = jnp.min(jnp.where(le == l1, ecol, big), axis=-1, keepdims=True)
    le2 = jnp.where(ecol == e1, -jnp.inf, le)
    l2 = jnp.max(le2, axis=-1, keepdims=True)
    e2 = jnp.min(jnp.where(le2 == l2, ecol, big), axis=-1, keepdims=True)
    w2 = jnp.exp(l2 - l1)
    den = 1.0 + w2
    re_ref[...] = jnp.where(col == 0, e1, jnp.where(col == 1, e2, 0))
    rg_ref[...] = jnp.where(col == 0, g_top / den, jnp.where(col == 1, g_top * w2 / den, 0.0))


def _out_router(att, gla, x, wo_att, wo_gla, n2, wr, br, tm):
    n, d = x.shape
    row = lambda i: (i, 0)
    fix = lambda i: (0, 0)
    return pl.pallas_call(
        _out_router_kernel,
        grid=(n // tm,),
        in_specs=[
            pl.BlockSpec((tm, ATT_WIDTH), row),
            pl.BlockSpec((tm, GLA_VW), row),
            pl.BlockSpec((tm, d), row),
            pl.BlockSpec((ATT_WIDTH, d), fix),
            pl.BlockSpec((GLA_VW, d), fix),
            pl.BlockSpec((1, d), fix),
            pl.BlockSpec((d, LANES), fix),
            pl.BlockSpec((1, LANES), fix),
        ],
        out_specs=[
            pl.BlockSpec((tm, d), row),
            pl.BlockSpec((tm, d), row),
            pl.BlockSpec((tm, LANES), row),
            pl.BlockSpec((tm, LANES), row),
        ],
        out_shape=[
            jax.ShapeDtypeStruct((n, d), F32),
            jax.ShapeDtypeStruct((n, d), F32),
            jax.ShapeDtypeStruct((n, LANES), I32),
            jax.ShapeDtypeStruct((n, LANES), F32),
        ],
        compiler_params=_params("parallel"),
        name="out_router",
    )(att, gla, x, wo_att, wo_gla, n2, wr, br)


def _expert_kernel(be_ref, tok_ref, nused_ref, hn_hbm, wg_ref, wu_ref, wd_ref, o_ref, xbuf, sem, *, rows, n_blocks):
    i = pl.program_id(0)

    def copy(blk, slot, r):
        tok = tok_ref[blk * rows + r]
        return pltpu.make_async_copy(hn_hbm.at[pl.ds(tok, 1), :], xbuf.at[slot, pl.ds(r, 1), :], sem.at[slot])

    def start(blk, slot):
        def body(r, carry):
            copy(blk, slot, r).start()
            return carry
        lax.fori_loop(0, rows, body, 0)

    @pl.when(i == 0)
    def _():
        start(0, 0)

    @pl.when(i + 1 < n_blocks)
    def _():
        start(i + 1, (i + 1) % 2)

    slot = i % 2

    def wbody(r, carry):
        copy(i, slot, r).wait()
        return carry
    lax.fori_loop(0, rows, wbody, 0)

    @pl.when(i < nused_ref[0])
    def _():
        x = xbuf[slot].astype(BF16)
        a = _dot(x, wg_ref[0])
        u = _dot(x, wu_ref[0])
        act = (a * jax.nn.sigmoid(a) * u).astype(BF16)
        o_ref[...] = _dot(act, wd_ref[0])

    @pl.when(i >= nused_ref[0])
    def _():
        o_ref[...] = jnp.zeros_like(o_ref)


def _experts(block_expert, slot_tok, n_used, hn, wg, wu, wd):
    n_blocks = block_expert.shape[0]
    rows = EXPERT_ROWS
    d = hn.shape[1]
    f = wg.shape[2]
    grid_spec = pltpu.PrefetchScalarGridSpec(
        num_scalar_prefetch=3,
        grid=(n_blocks,),
        in_specs=[
            pl.BlockSpec(memory_space=pl.ANY),
            pl.BlockSpec((1, d, f), lambda i, be, tk, nu: (be[i], 0, 0)),
            pl.BlockSpec((1, d, f), lambda i, be, tk, nu: (be[i], 0, 0)),
            pl.BlockSpec((1, f, d), lambda i, be, tk, nu: (be[i], 0, 0)),
        ],
        out_specs=pl.BlockSpec((rows, d), lambda i, be, tk, nu: (i, 0)),
        scratch_shapes=[
            pltpu.VMEM((2, rows, d), F32),
            pltpu.SemaphoreType.DMA((2,)),
        ],
    )
    return pl.pallas_call(
        functools.partial(_expert_kernel, rows=rows, n_blocks=n_blocks),
        grid_spec=grid_spec,
        out_shape=jax.ShapeDtypeStruct((n_blocks * rows, d), F32),
        compiler_params=_params("arbitrary"),
        name="experts",
    )(block_expert, slot_tok, n_used, hn, wg, wu, wd)


def _combine_kernel(pos_ref, h_ref, rg_ref, ys_hbm, y_ref, buf, sem, *, rows, n_steps, tok0):
    i = pl.program_id(0)

    def copy(step, slot, r, k):
        p = pos_ref[(tok0 + step * rows + r) * EXPERT_TOPK + k]
        return pltpu.make_async_copy(ys_hbm.at[pl.ds(p, 1), :], buf.at[slot, k, pl.ds(r, 1), :], sem.at[slot])

    def start(step, slot):
        def body(r, carry):
            for k in range(EXPERT_TOPK):
                copy(step, slot, r, k).start()
            return carry
        lax.fori_loop(0, rows, body, 0)

    @pl.when(i == 0)
    def _():
        start(0, 0)

    @pl.when(i + 1 < n_steps)
    def _():
        start(i + 1, (i + 1) % 2)

    slot = i % 2

    def wbody(r, carry):
        for k in range(EXPERT_TOPK):
            copy(i, slot, r, k).wait()
        return carry
    lax.fori_loop(0, rows, wbody, 0)

    rg = rg_ref[...]
    y = h_ref[...]
    for k in range(EXPERT_TOPK):
        y = y + buf[slot, k] * rg[:, k:k + 1]
    y_ref[...] = y


def _combine(pos, h, rg, ys, tok0, n_tok, rows):
    d = h.shape[1]
    n_steps = n_tok // rows
    b0 = tok0 // rows
    grid_spec = pltpu.PrefetchScalarGridSpec(
        num_scalar_prefetch=1,
        grid=(n_steps,),
        in_specs=[
            pl.BlockSpec((rows, d), lambda i, p: (b0 + i, 0)),
            pl.BlockSpec((rows, LANES), lambda i, p: (b0 + i, 0)),
            pl.BlockSpec(memory_space=pl.ANY),
        ],
        out_specs=pl.BlockSpec((rows, d), lambda i, p: (i, 0)),
        scratch_shapes=[
            pltpu.VMEM((2, EXPERT_TOPK, rows, d), F32),
            pltpu.SemaphoreType.DMA((2,)),
        ],
    )
    return pl.pallas_call(
        functools.partial(_combine_kernel, rows=rows, n_steps=n_steps, tok0=tok0),
        grid_spec=grid_spec,
        out_shape=jax.ShapeDtypeStruct((n_tok, d), F32),
        compiler_params=_params("arbitrary"),
        name="combine",
    )(pos, h, rg, ys)


def _dispatch(expert):
    a = expert.shape[0]
    rows = EXPERT_ROWS
    onehot = (expert[:, None] == jnp.arange(N_EXPERTS, dtype=I32)[None, :]).astype(I32)
    csum = jnp.cumsum(onehot, axis=0)
    counts = csum[-1]
    rank = jnp.take_along_axis(csum, expert[:, None], axis=1)[:, 0] - 1
    padded = (counts + rows - 1) // rows * rows
    ends_p = jnp.cumsum(padded)
    pstart = ends_p - padded
    pos = (pstart[expert] + rank).astype(I32)
    n_blocks = -(-a // rows) + N_EXPERTS
    slot_tok = jnp.zeros((n_blocks * rows,), I32).at[pos].set(jnp.arange(a, dtype=I32) // EXPERT_TOPK)
    block_expert = jnp.minimum(
        jnp.searchsorted(ends_p, jnp.arange(n_blocks, dtype=I32) * rows, side='right'), N_EXPERTS - 1).astype(I32)
    n_used = (ends_p[-1:] // rows).astype(I32)
    return pos, slot_tok, block_expert, n_used


def kernel(x_prompt, x_sample, cache_k, cache_v, state_gla, page_table, norm1_w, w_in, q_norm_w, k_norm_w,
           w_gla_a2, b_gla_a, gla_norm_w, w_out, norm2_w, w_r1, b_r1, w_r2, b_r2, w_e_gate, w_e_up, w_e_down):
    depth = w_in.shape[0]
    assert depth == 1
    batch, seq, d = x_prompt.shape
    db, nq, _ = x_sample.shape
    n_pages = page_table.shape[1]
    past = n_pages * PAGE_SIZE
    assert past % MOBA_BLOCK == 0 and nq <= GLA_CHUNK and seq % (8 * MOBA_BLOCK) == 0
    n_past_blocks = past // MOBA_BLOCK
    assert n_past_blocks + 8 <= LANES
    n_p, n_s = batch * seq, db * nq
    l = 0

    w = w_in[l]
    o_gq = 3 * ATT_WIDTH
    o_gv = o_gq + 2 * GLA_KW
    o_ga = o_gv + GLA_VW
    o_gg = o_ga + GLA_GATE_RANK
    w_main = jnp.concatenate([w[:, :o_ga], w[:, o_gg:]], axis=1).astype(BF16)
    w_ga = jnp.pad(w[:, o_ga:o_gg], ((0, 0), (0, LANES - GLA_GATE_RANK))).astype(BF16)
    n1 = norm1_w[l][None, :]
    qn = q_norm_w[l][None, :]
    kn = k_norm_w[l][None, :]
    wa = jnp.pad(w_gla_a2[l], ((0, LANES - GLA_GATE_RANK), (0, 0)))
    ba = b_gla_a[l][None, :]
    gnw = gla_norm_w[l][None, :]
    wo = w_out[l].astype(BF16)
    wo_att, wo_gla = wo[:ATT_WIDTH], wo[ATT_WIDTH:]
    n2 = norm2_w[l][None, :]
    wr = jnp.pad(jnp.concatenate([w_r1[l], w_r2[l]], axis=1), ((0, 0), (0, LANES - N_GROUPS - N_EXPERTS)))
    br = jnp.pad(jnp.concatenate([b_r1[l], b_r2[l]]), (0, LANES - N_GROUPS - N_EXPERTS))[None, :]
    wg = w_e_gate[l].astype(BF16)
    wu = w_e_up[l].astype(BF16)
    wd = w_e_down[l].astype(BF16)
    slopes = 2.0 ** (-8.0 * jnp.arange(1, N_ATT_HEADS + 1, dtype=F32) / N_ATT_HEADS)

    xp = x_prompt.reshape(n_p, d)
    q_p, k_p, kb_p, v_p, vb_p, gqk_p, gv_p, gg_p, ga_p = _project(xp, n1, w_main, w_ga, qn, kn, 512)
    kmeans = _block_means(k_p)
    att_p = _moba_prompt(slopes, q_p, kb_p, vb_p, kmeans, batch, seq)
    s0_p = jnp.zeros((batch, N_GLA_HEADS, GLA_DK, GLA_DV), F32)
    gla_p, s_p = _gla(gqk_p, gv_p, gg_p, ga_p, wa, ba, gnw, s0_p, batch, seq, seq)

    xs = x_sample.reshape(n_s, d)
    q_s, k_s, _, v_s, _, gqk_s, gv_s, gg_s, ga_s = _project(xs, n1, w_main, w_ga, qn, kn, n_s)
    block_sums = _page_block_sums(page_table, cache_k[l])
    idx = _sample_select(q_s, block_sums.reshape(db * n_past_blocks, ATT_WIDTH), k_s, db, nq, n_past_blocks)
    att_s = _moba_sample(page_table, idx[..., :MOBA_TOPK], slopes, q_s, k_s, v_s, cache_k[l], cache_v[l], db, nq)

    def pad_seq(a):
        return jnp.pad(a.reshape(db, nq, -1), ((0, 0), (0, GLA_CHUNK - nq), (0, 0))).reshape(db * GLA_CHUNK, -1)

    gla_s, s_s = _gla(pad_seq(gqk_s), pad_seq(gv_s), pad_seq(gg_s), pad_seq(ga_s), wa, ba, gnw,
                      state_gla[l], db, GLA_CHUNK, nq)
    gla_s = gla_s.reshape(db, GLA_CHUNK, GLA_VW)[:, :nq].reshape(n_s, GLA_VW)

    x_all = jnp.concatenate([xp, xs], axis=0)
    att_all = jnp.concatenate([att_p, att_s.astype(BF16)], axis=0)
    gla_all = jnp.concatenate([gla_p, gla_s], axis=0)
    h, hn, re, rg = _out_router(att_all, gla_all, x_all, wo_att, wo_gla, n2, wr, br, 256)
    expert = re[:, :EXPERT_TOPK].reshape(-1)
    pos, slot_tok, block_expert, n_used = _dispatch(expert)
    ys = _experts(block_expert, slot_tok, n_used, hn, wg, wu, wd)
    y_p = _combine(pos, h, rg, ys, 0, n_p, 128)
    y_s = _combine(pos, h, rg, ys, n_p, n_s, 128)

    hd = (N_ATT_HEADS, HEAD_DIM)
    return (y_p.reshape(batch, seq, d), y_s.reshape(db, nq, d),
            k_p.reshape(1, batch, seq, *hd), v_p.reshape(1, batch, seq, *hd), s_p[None],
            k_s.reshape(1, db, nq, *hd), v_s.reshape(1, db, nq, *hd), s_s[None])
```

```python
import functools

import jax
import jax.numpy as jnp
from jax import lax
from jax.experimental import pallas as pl
from jax.experimental.pallas import tpu as pltpu

F32 = jnp.float32
BF16 = jnp.bfloat16
I32 = jnp.int32

HEAD_DIM = 128
N_ATT_HEADS = 8
ATT_WIDTH = N_ATT_HEADS * HEAD_DIM
MOBA_BLOCK = 256
MOBA_TOPK = 3
MOBA_HALF = 2
SUM_ROWS = 16
LOG2E = 1.4426950408889634
PAGE_SIZE = 128
PAGES_PER_BLOCK = MOBA_BLOCK // PAGE_SIZE
N_GLA_HEADS = 4
GLA_DK = 128
GLA_DV = 256
GLA_KW = N_GLA_HEADS * GLA_DK
GLA_VW = N_GLA_HEADS * GLA_DV
GLA_GATE_RANK = 16
GLA_TAU = 16.0
GLA_CHUNK = 64
N_GROUPS = 4
EXPERTS_PER_GROUP = 8
N_EXPERTS = N_GROUPS * EXPERTS_PER_GROUP
EXPERT_TOPK = 2
RMS_EPS = 1e-6
NEG_INF = -1e30
LANES = 128
EXPERT_ROWS = 256
DMA_UNROLL = 8
VMEM_LIMIT = 56 * 1024 * 1024

_NT = (((1,), (1,)), ((), ()))
_TN = (((0,), (0,)), ((), ()))
_HI = lax.Precision.HIGHEST


def _dot(a, b):
    return jnp.dot(a, b, preferred_element_type=F32)


def _dot_nt(a, b):
    return lax.dot_general(a, b, _NT, preferred_element_type=F32)


def _dot_tn(a, b):
    return lax.dot_general(a, b, _TN, preferred_element_type=F32)


def _params(*sem):
    return pltpu.CompilerParams(dimension_semantics=sem, vmem_limit_bytes=VMEM_LIMIT)


def _top3(gate, col, n_valid, axis=-1):
    picks = []
    g = gate
    for r in range(MOBA_TOPK):
        m = jnp.max(g, axis=axis, keepdims=True)
        idx = jnp.min(jnp.where(g == m, col, jnp.int32(2 ** 30)), axis=axis, keepdims=True)
        g = jnp.where(col == idx, -jnp.inf, g)
        picks.append(jnp.where(r < n_valid, idx, -1))
    return picks


def _proj_kernel(x_ref, n1_ref, w_ref, wga_ref, qn_ref, kn_ref,
                 q_ref, k_ref, kb_ref, v_ref, vt_ref, gqk_ref, gv_ref, gg_ref, ga_ref, xn_ref):
    j = pl.program_id(1)

    @pl.when(j == 0)
    def _():
        x = x_ref[...]
        y = x * lax.rsqrt(jnp.mean(x * x, axis=-1, keepdims=True) + RMS_EPS) * n1_ref[...]
        xn_ref[...] = y.astype(BF16)
        ga_ref[...] = _dot(xn_ref[...], wga_ref[...])

    z = _dot(xn_ref[...], w_ref[...])

    def head_norm(w):
        outs = []
        for h in range(N_ATT_HEADS):
            zh = z[:, h * HEAD_DIM:(h + 1) * HEAD_DIM]
            outs.append(zh * lax.rsqrt(jnp.mean(zh * zh, axis=-1, keepdims=True) + RMS_EPS) * w)
        return outs

    @pl.when(j == 0)
    def _():
        for h, y in enumerate(head_norm(qn_ref[...])):
            q_ref[:, h * HEAD_DIM:(h + 1) * HEAD_DIM] = y

    @pl.when(j == 1)
    def _():
        for h, y in enumerate(head_norm(kn_ref[...])):
            k_ref[:, h * HEAD_DIM:(h + 1) * HEAD_DIM] = y
            kb_ref[:, h * HEAD_DIM:(h + 1) * HEAD_DIM] = y.astype(BF16)

    @pl.when(j == 2)
    def _():
        v_ref[...] = z
        zt = z.T.astype(BF16)
        for c in range(vt_ref.shape[0]):
            vt_ref[c] = zt[:, c * MOBA_BLOCK:(c + 1) * MOBA_BLOCK]

    @pl.when(j == 3)
    def _():
        gqk_ref[...] = z

    @pl.when(j == 4)
    def _():
        gv_ref[...] = z

    @pl.when(j == 5)
    def _():
        gg_ref[...] = z


def _project(x, n1, w_main, w_ga, qn, kn, tm):
    n, d = x.shape
    wide = ATT_WIDTH
    row = lambda i, j: (i, 0)
    out_shape = [
        jax.ShapeDtypeStruct((n, wide), F32),
        jax.ShapeDtypeStruct((n, wide), F32),
        jax.ShapeDtypeStruct((n, wide), BF16),
        jax.ShapeDtypeStruct((n, wide), F32),
        jax.ShapeDtypeStruct((n // MOBA_BLOCK, wide, MOBA_BLOCK), BF16),
        jax.ShapeDtypeStruct((n, wide), F32),
        jax.ShapeDtypeStruct((n, wide), F32),
        jax.ShapeDtypeStruct((n, wide), F32),
        jax.ShapeDtypeStruct((n, LANES), F32),
    ]
    out_specs = [pl.BlockSpec((tm, s.shape[1]), row) if len(s.shape) == 2
                 else pl.BlockSpec((tm // MOBA_BLOCK, wide, MOBA_BLOCK), lambda i, j: (i, 0, 0)) for s in out_shape]
    return pl.pallas_call(
        _proj_kernel,
        grid=(n // tm, 6),
        in_specs=[
            pl.BlockSpec((tm, d), row),
            pl.BlockSpec((1, d), lambda i, j: (0, 0)),
            pl.BlockSpec((d, wide), lambda i, j: (0, j)),
            pl.BlockSpec((d, LANES), lambda i, j: (0, 0)),
            pl.BlockSpec((1, HEAD_DIM), lambda i, j: (0, 0)),
            pl.BlockSpec((1, HEAD_DIM), lambda i, j: (0, 0)),
        ],
        out_specs=out_specs,
        out_shape=out_shape,
        scratch_shapes=[pltpu.VMEM((tm, d), BF16)],
        compiler_params=_params("parallel", "arbitrary"),
        name="projection",
    )(x, n1, w_main, w_ga, qn, kn)


def _kmeans_kernel(k_ref, o_ref):
    rows = k_ref.shape[0]
    k = k_ref[...].reshape(rows // MOBA_BLOCK, MOBA_BLOCK, k_ref.shape[1])
    o_ref[...] = jnp.sum(k, axis=1) * (1.0 / MOBA_BLOCK)


def _block_means(k):
    n, w = k.shape
    rows = 8 * MOBA_BLOCK
    return pl.pallas_call(
        _kmeans_kernel,
        grid=(n // rows,),
        in_specs=[pl.BlockSpec((rows, w), lambda i: (i, 0))],
        out_specs=pl.BlockSpec((8, w), lambda i: (i, 0)),
        out_shape=jax.ShapeDtypeStruct((n // MOBA_BLOCK, w), F32),
        compiler_params=_params("parallel"),
        name="block_means",
    )(k)


def _moba_prompt_kernel(slopes_ref, q_ref, k_ref, vt_ref, km_ref, o_ref, sa_scr, sb_scr, bias_scr):
    h = pl.program_id(1)
    i = pl.program_id(2)
    bs = MOBA_BLOCK
    nb = km_ref.shape[0]
    slope2 = slopes_ref[h] * LOG2E
    q = q_ref[...]
    q2 = (q * (HEAD_DIM ** -0.5 * LOG2E)).astype(BF16)

    gate = lax.dot_general(km_ref[...], q, _NT, precision=_HI, preferred_element_type=F32)
    blk = lax.broadcasted_iota(I32, (nb, bs), 0)
    gate = jnp.where(blk < i, gate, NEG_INF)
    i0, i1, i2 = _top3(gate, blk, i, axis=0)

    krow = lax.broadcasted_iota(I32, (bs, bs), 0)
    qcol = lax.broadcasted_iota(I32, (bs, bs), 1)
    bias_scr[...] = slope2 * krow.astype(F32)

    ones_rows = jnp.ones((SUM_ROWS, bs), BF16)

    def pv_dot(jc, p):
        return _dot(jnp.concatenate([vt_ref[jc], ones_rows], axis=0), p.astype(BF16))

    def selected(j):
        return (i0 == j) | (i1 == j) | (i2 == j)

    def block_offset(j):
        return slope2 * ((j - i) * bs).astype(F32)

    r_own = pl.multiple_of(i * bs, bs)
    s = jnp.where(krow <= qcol, _dot_nt(k_ref[pl.ds(r_own, bs), :], q2) + bias_scr[...], NEG_INF)
    m = jnp.max(s, axis=0, keepdims=True)
    pv = pv_dot(i, jnp.exp2(s - m))
    acc = pv[:HEAD_DIM]
    l = pv[HEAD_DIM:HEAD_DIM + 1]

    hb = MOBA_HALF

    def sweep1(jb, s_ref):
        r0 = pl.multiple_of(jnp.minimum(jb, nb - hb) * bs, bs)
        s = _dot_nt(k_ref[pl.ds(r0, hb * bs), :], q2)
        cmax = jnp.full((1, bs), NEG_INF, F32)
        for u in range(hb):
            su = s[u * bs:(u + 1) * bs] + bias_scr[...]
            s_ref[u] = su
            cmax = jnp.maximum(cmax, jnp.where(selected(jb + u),
                                               jnp.max(su, axis=0, keepdims=True) + block_offset(jb + u), NEG_INF))
        return cmax

    def sweep2(jb, s_ref, m_prev, m_cur, l, acc):
        pv = jnp.zeros((HEAD_DIM + SUM_ROWS, bs), F32)
        for u in range(hb):
            j = jb + u
            ref = jnp.where(selected(j), m_cur - block_offset(j), -NEG_INF)
            pv = pv + pv_dot(jnp.minimum(j, nb - 1), jnp.exp2(s_ref[u] - ref))
        alpha = jnp.exp2(m_prev - m_cur)
        return alpha * l + pv[HEAD_DIM:HEAD_DIM + 1], alpha * acc + pv[:HEAD_DIM]

    m_a = jnp.maximum(m, sweep1(0, sa_scr))

    def chunk(c, carry):
        m_prev, m_a, l, acc = carry
        jb = c * (2 * hb)
        m_b = jnp.maximum(m_a, sweep1(jb + hb, sb_scr))
        l, acc = sweep2(jb, sa_scr, m_prev, m_a, l, acc)
        m_a2 = jnp.maximum(m_b, sweep1(jb + 2 * hb, sa_scr))
        l, acc = sweep2(jb + hb, sb_scr, m_a, m_b, l, acc)
        return m_b, m_a2, l, acc

    n_chunks = (i + 2 * hb - 1) // (2 * hb)
    _, _, l, acc = lax.fori_loop(0, n_chunks, chunk, (m, m_a, l, acc))
    o_ref[...] = (acc / l).T.astype(o_ref.dtype)


def _moba_prompt(slopes, q, kb, vt, kmeans, batch, seq):
    n = q.shape[0]
    nb = seq // MOBA_BLOCK
    bs = MOBA_BLOCK
    return pl.pallas_call(
        _moba_prompt_kernel,
        grid=(batch, N_ATT_HEADS, nb),
        in_specs=[
            pl.BlockSpec(memory_space=pltpu.SMEM),
            pl.BlockSpec((bs, HEAD_DIM), lambda b, h, i: (b * nb + i, h)),
            pl.BlockSpec((seq, HEAD_DIM), lambda b, h, i: (b, h)),
            pl.BlockSpec((nb, HEAD_DIM, bs), lambda b, h, i: (b, h, 0)),
            pl.BlockSpec((nb, HEAD_DIM), lambda b, h, i: (b, h)),
        ],
        out_specs=pl.BlockSpec((bs, HEAD_DIM), lambda b, h, i: (b * nb + i, h)),
        out_shape=jax.ShapeDtypeStruct((n, ATT_WIDTH), BF16),
        scratch_shapes=[pltpu.VMEM((MOBA_HALF, bs, bs), F32), pltpu.VMEM((MOBA_HALF, bs, bs), F32),
                        pltpu.VMEM((bs, bs), F32)],
        compiler_params=_params("parallel", "parallel", "arbitrary"),
        name="moba_prompt",
    )(slopes, q, kb, vt, kmeans)


def _page_sum_kernel(pt_ref, kc_hbm, o_ref, buf, sem, *, group, n_chunks):
    c = pl.program_id(0)

    def copy(cc, slot, g):
        page = pt_ref[cc * group + g]
        return pltpu.make_async_copy(kc_hbm.at[page], buf.at[slot, g], sem.at[slot])

    def start(cc, slot):
        for g in range(group):
            copy(cc, slot, g).start()

    @pl.when(c == 0)
    def _():
        start(0, 0)

    @pl.when(c + 1 < n_chunks)
    def _():
        start(c + 1, (c + 1) % 2)

    slot = c % 2
    for g in range(group):
        copy(c, slot, g).wait()
    for bk in range(group // PAGES_PER_BLOCK):
        acc = jnp.sum(buf[slot, PAGES_PER_BLOCK * bk], axis=0)
        for p in range(1, PAGES_PER_BLOCK):
            acc = acc + jnp.sum(buf[slot, PAGES_PER_BLOCK * bk + p], axis=0)
        o_ref[bk] = acc


def _page_block_sums(page_table, cache_k):
    db, n_pages = page_table.shape
    group = 8
    n_chunks = db * n_pages // group
    bpc = group // PAGES_PER_BLOCK
    grid_spec = pltpu.PrefetchScalarGridSpec(
        num_scalar_prefetch=1,
        grid=(n_chunks,),
        in_specs=[pl.BlockSpec(memory_space=pl.ANY)],
        out_specs=pl.BlockSpec((bpc, N_ATT_HEADS, HEAD_DIM), lambda c, pt: (c, 0, 0)),
        scratch_shapes=[
            pltpu.VMEM((2, group, PAGE_SIZE, N_ATT_HEADS, HEAD_DIM), F32),
            pltpu.SemaphoreType.DMA((2,)),
        ],
    )
    return pl.pallas_call(
        functools.partial(_page_sum_kernel, group=group, n_chunks=n_chunks),
        grid_spec=grid_spec,
        out_shape=jax.ShapeDtypeStruct((n_chunks * bpc, N_ATT_HEADS, HEAD_DIM), F32),
        compiler_params=_params("arbitrary"),
        name="page_block_sums",
    )(page_table.reshape(-1), cache_k)


def _sample_select_kernel(q_ref, bs_ref, kn_ref, o_ref, *, n_past_blocks, past):
    nq = q_ref.shape[0]
    col = lax.broadcasted_iota(I32, (nq, LANES), 1)
    lane = col
    n_full = (past + lax.broadcasted_iota(I32, (nq, 1), 0)) // MOBA_BLOCK
    for h in range(N_ATT_HEADS):
        sl = slice(h * HEAD_DIM, (h + 1) * HEAD_DIM)
        m_past = bs_ref[:, sl] * (1.0 / MOBA_BLOCK)
        m_new = jnp.sum(kn_ref[:, sl], axis=0, keepdims=True) * (1.0 / MOBA_BLOCK)
        row8 = lax.broadcasted_iota(I32, (8, HEAD_DIM), 0)
        new_rows = jnp.where(row8 == 0, jnp.broadcast_to(m_new, (8, HEAD_DIM)), 0.0)
        means = jnp.concatenate(
            [m_past, new_rows, jnp.zeros((LANES - n_past_blocks - 8, HEAD_DIM), F32)], axis=0)
        gate = lax.dot_general(q_ref[:, sl], means, _NT, precision=_HI, preferred_element_type=F32)
        gate = jnp.where(col < n_full, gate, NEG_INF)
        i0, i1, i2 = _top3(gate, col, n_full)
        o_ref[0, h] = jnp.where(lane == 0, i0, jnp.where(lane == 1, i1, jnp.where(lane == 2, i2, 0)))


def _sample_select(q, block_sums, k_new, db, nq, n_past_blocks):
    return pl.pallas_call(
        functools.partial(_sample_select_kernel, n_past_blocks=n_past_blocks,
                          past=n_past_blocks * MOBA_BLOCK),
        grid=(db,),
        in_specs=[
            pl.BlockSpec((nq, ATT_WIDTH), lambda s: (s, 0)),
            pl.BlockSpec((n_past_blocks, ATT_WIDTH), lambda s: (s, 0)),
            pl.BlockSpec((nq, ATT_WIDTH), lambda s: (s, 0)),
        ],
        out_specs=pl.BlockSpec((1, N_ATT_HEADS, nq, LANES), lambda s: (s, 0, 0, 0)),
        out_shape=jax.ShapeDtypeStruct((db, N_ATT_HEADS, nq, LANES), I32),
        compiler_params=_params("parallel"),
        name="sample_select",
    )(q, block_sums, k_new)


def _moba_sample_kernel(pt_ref, idx_ref, slopes_ref, q_ref, kn_ref, vn_ref, kc_hbm, vc_hbm, o_ref,
                        kbuf, vbuf, kown, vown, sem, *, n_steps, n_pages, nq, past):
    t = pl.program_id(0)
    nh = N_ATT_HEADS
    n_sel = nq * MOBA_TOPK
    n_slots = n_sel * PAGES_PER_BLOCK

    def copies(tt, slot, g):
        s = tt // nh
        h = tt % nh
        blk = idx_ref[tt * n_sel + g // PAGES_PER_BLOCK]
        page = pt_ref[s * n_pages + blk * PAGES_PER_BLOCK + g % PAGES_PER_BLOCK]
        return (pltpu.make_async_copy(kc_hbm.at[page, :, h, :], kbuf.at[slot, g], sem.at[0, slot]),
                pltpu.make_async_copy(vc_hbm.at[page, :, h, :], vbuf.at[slot, g], sem.at[1, slot]))

    def start(tt, slot):
        for g in range(n_slots):
            ck, cv = copies(tt, slot, g)
            ck.start()
            cv.start()

    @pl.when(t == 0)
    def _():
        start(0, 0)
        kown[...] = jnp.zeros_like(kown)
        vown[...] = jnp.zeros_like(vown)

    @pl.when(t + 1 < n_steps)
    def _():
        start(t + 1, (t + 1) % 2)

    slot = t % 2
    h = t % nh
    slope = slopes_ref[h]
    scale = HEAD_DIM ** -0.5
    qb = q_ref[...].astype(BF16)
    n_keys = n_slots * PAGE_SIZE
    keys_per_q = MOBA_TOPK * MOBA_BLOCK

    kown[0:nq, :] = kn_ref[...]
    vown[0:nq, :] = vn_ref[...]
    row = lax.broadcasted_iota(I32, (nq, LANES), 0)
    colo = lax.broadcasted_iota(I32, (nq, LANES), 1)
    s_own = _dot_nt(qb, kown[...].astype(BF16)) * scale - slope * (row - colo).astype(F32)
    s_own = jnp.where(colo <= row, s_own, NEG_INF)

    colk = lax.broadcasted_iota(I32, (1, n_keys), 1)
    grp = colk // MOBA_BLOCK
    blk_of_col = jnp.zeros((1, n_keys), I32)
    for g in range(n_sel):
        blk_of_col = jnp.where(grp == g, idx_ref[t * n_sel + g], blk_of_col)
    pos = blk_of_col * MOBA_BLOCK + colk % MOBA_BLOCK
    rowk = lax.broadcasted_iota(I32, (nq, n_keys), 0)
    mine = (lax.broadcasted_iota(I32, (nq, n_keys), 1) // keys_per_q) == rowk
    n_full = (past + rowk) // MOBA_BLOCK
    pick = (lax.broadcasted_iota(I32, (nq, n_keys), 1) // MOBA_BLOCK) % MOBA_TOPK
    valid = mine & (pick < n_full)

    for g in range(n_slots):
        ck, cv = copies(t, slot, g)
        ck.wait()
        cv.wait()

    kall = kbuf[slot].reshape(n_keys, HEAD_DIM).astype(BF16)
    s_sel = _dot_nt(qb, kall) * scale - slope * ((past + rowk) - pos).astype(F32)
    s_sel = jnp.where(valid, s_sel, NEG_INF)
    m = jnp.maximum(jnp.max(s_sel, axis=-1, keepdims=True), jnp.max(s_own, axis=-1, keepdims=True))
    p_sel = jnp.exp(s_sel - m)
    p_own = jnp.exp(s_own - m)
    l = jnp.sum(p_sel, axis=-1, keepdims=True) + jnp.sum(p_own, axis=-1, keepdims=True)
    vall = vbuf[slot].reshape(n_keys, HEAD_DIM).astype(BF16)
    o = _dot(p_sel.astype(BF16), vall) + _dot(p_own.astype(BF16), vown[...].astype(BF16))
    o_ref[...] = o / l


def _moba_sample(page_table, idx, slopes, q, k_new, v_new, cache_k, cache_v, db, nq):
    n_pages = page_table.shape[1]
    n_steps = db * N_ATT_HEADS
    n_slots = nq * MOBA_TOPK * PAGES_PER_BLOCK
    qspec = pl.BlockSpec((nq, HEAD_DIM), lambda t, pt, ix: (t // N_ATT_HEADS, t % N_ATT_HEADS))
    grid_spec = pltpu.PrefetchScalarGridSpec(
        num_scalar_prefetch=2,
        grid=(n_steps,),
        in_specs=[
            pl.BlockSpec(memory_space=pltpu.SMEM),
            qspec, qspec, qspec,
            pl.BlockSpec(memory_space=pl.ANY),
            pl.BlockSpec(memory_space=pl.ANY),
        ],
        out_specs=qspec,
        scratch_shapes=[
            pltpu.VMEM((2, n_slots, PAGE_SIZE, HEAD_DIM), F32),
            pltpu.VMEM((2, n_slots, PAGE_SIZE, HEAD_DIM), F32),
            pltpu.VMEM((LANES, HEAD_DIM), F32),
            pltpu.VMEM((LANES, HEAD_DIM), F32),
            pltpu.SemaphoreType.DMA((2, 2)),
        ],
    )
    return pl.pallas_call(
        functools.partial(_moba_sample_kernel, n_steps=n_steps, n_pages=n_pages, nq=nq,
                          past=n_pages * PAGE_SIZE),
        grid_spec=grid_spec,
        out_shape=jax.ShapeDtypeStruct((db * nq, ATT_WIDTH), F32),
        compiler_params=_params("arbitrary"),
        name="moba_sample",
    )(page_table.reshape(-1), idx.reshape(-1), slopes, q, k_new, v_new, cache_k, cache_v)


def _gla_kernel(gq_ref, gk_ref, gv_ref, gg_ref, ga_ref, wa_ref, ba_ref, nw_ref, s0_ref,
                o_ref, sfin_ref, s_scr, *, chunk, n_sub, t_valid):
    t = pl.program_id(2)

    @pl.when(t == 0)
    def _():
        s_scr[...] = s0_ref[0, 0]

    c = chunk
    rowc = lax.broadcasted_iota(I32, (c, c), 0)
    colc = lax.broadcasted_iota(I32, (c, c), 1)
    tri = rowc >= colc
    trif = tri.astype(F32)
    ones = jnp.ones((c, GLA_DK), F32)
    wa = wa_ref[...]
    ba = ba_ref[...]
    nw = nw_ref[...]

    def step(ci, carry):
        r0 = pl.multiple_of(ci * c, c)
        q = gq_ref[pl.ds(r0, c), :] * (GLA_DK ** -0.5)
        k = gk_ref[pl.ds(r0, c), :]
        v = gv_ref[pl.ds(r0, c), :].astype(BF16)
        pre = jnp.dot(ga_ref[pl.ds(r0, c), :], wa, precision=_HI, preferred_element_type=F32) + ba
        la = (jnp.minimum(pre, 0.0) - jnp.log1p(jnp.exp(-jnp.abs(pre)))) * (1.0 / GLA_TAU)
        if t_valid < c:
            la = jnp.where(lax.broadcasted_iota(I32, la.shape, 0) < t_valid, la, 0.0)
        b = jnp.dot(trif, la, precision=_HI, preferred_element_type=F32)
        b_end = b[c - 1:c, :]
        b_end_col = lax.dot_general(la, ones, _TN, precision=_HI, preferred_element_type=F32)
        q_dec = (q * jnp.exp(b)).astype(BF16)
        k_inv = (k * jnp.exp(-b)).astype(BF16)
        k_end = (k * jnp.exp(b_end - b)).astype(BF16)
        a = jnp.where(tri, _dot_nt(q_dec, k_inv), 0.0)
        s_prev = s_scr[...]
        o = _dot(a.astype(BF16), v) + _dot(q_dec, s_prev.astype(BF16))
        ds = _dot_tn(k_end, v)
        decay = jnp.exp(b_end_col)
        s_scr[...] = jnp.concatenate([decay] * (GLA_DV // GLA_DK), axis=1) * s_prev + ds
        on = o * lax.rsqrt(jnp.mean(o * o, axis=-1, keepdims=True) + RMS_EPS) * nw
        g = gg_ref[pl.ds(r0, c), :]
        o_ref[pl.ds(r0, c), :] = (on * (g * jax.nn.sigmoid(g))).astype(o_ref.dtype)
        return carry

    lax.fori_loop(0, n_sub, step, 0, unroll=2 if n_sub % 2 == 0 else 1)

    @pl.when(t == pl.num_programs(2) - 1)
    def _():
        sfin_ref[0, 0] = s_scr[...]


def _gla(gqk, gv, gg, ga, wa, ba, nw, s0, batch, seq, t_valid):
    n = gqk.shape[0]
    c = GLA_CHUNK
    n_sub = min(8, seq // c)
    rows = c * n_sub
    steps = seq // rows
    nh = N_GLA_HEADS
    return pl.pallas_call(
        functools.partial(_gla_kernel, chunk=c, n_sub=n_sub, t_valid=t_valid),
        grid=(batch, nh, steps),
        in_specs=[
            pl.BlockSpec((rows, GLA_DK), lambda b, h, t: (b * steps + t, h)),
            pl.BlockSpec((rows, GLA_DK), lambda b, h, t: (b * steps + t, nh + h)),
            pl.BlockSpec((rows, GLA_DV), lambda b, h, t: (b * steps + t, h)),
            pl.BlockSpec((rows, GLA_DV), lambda b, h, t: (b * steps + t, h)),
            pl.BlockSpec((rows, LANES), lambda b, h, t: (b * steps + t, 0)),
            pl.BlockSpec((LANES, GLA_DK), lambda b, h, t: (0, h)),
            pl.BlockSpec((1, GLA_DK), lambda b, h, t: (0, h)),
            pl.BlockSpec((1, GLA_DV), lambda b, h, t: (0, 0)),
            pl.BlockSpec((1, 1, GLA_DK, GLA_DV), lambda b, h, t: (b, h, 0, 0)),
        ],
        out_specs=[
            pl.BlockSpec((rows, GLA_DV), lambda b, h, t: (b * steps + t, h)),
            pl.BlockSpec((1, 1, GLA_DK, GLA_DV), lambda b, h, t: (b, h, 0, 0)),
        ],
        out_shape=[
            jax.ShapeDtypeStruct((n, GLA_VW), BF16),
            jax.ShapeDtypeStruct(s0.shape, F32),
        ],
        scratch_shapes=[pltpu.VMEM((GLA_DK, GLA_DV), F32)],
        compiler_params=_params("parallel", "parallel", "arbitrary"),
        name="gla",
    )(gqk, gqk, gv, gg, ga, wa, ba, nw, s0)


def _out_router_kernel(att_ref, gla_ref, x_ref, woa_ref, wog_ref, n2_ref, wrh_ref, wrl_ref, br_ref,
                       h_ref, hn_ref, re_ref, rg_ref):
    h = x_ref[...] + _dot(att_ref[...], woa_ref[...]) + _dot(gla_ref[...], wog_ref[...])
    h_ref[...] = h
    hn = h * lax.rsqrt(jnp.mean(h * h, axis=-1, keepdims=True) + RMS_EPS) * n2_ref[...]
    hn_ref[...] = hn
    hn_hi = hn.astype(BF16)
    hn_lo = (hn - hn_hi.astype(F32)).astype(BF16)
    logits = (_dot(hn_hi, wrh_ref[...]) + _dot(hn_hi, wrl_ref[...]) + _dot(hn_lo, wrh_ref[...])) + br_ref[...]
    col = lax.broadcasted_iota(I32, logits.shape, 1)
    big = jnp.int32(2 ** 30)
    in_g = col < N_GROUPS
    lg = jnp.where(in_g, logits, -jnp.inf)
    m1 = jnp.max(lg, axis=-1, keepdims=True)
    g_idx = jnp.min(jnp.where(lg == m1, col, big), axis=-1, keepdims=True)
    g_top = 1.0 / jnp.sum(jnp.exp(lg - m1), axis=-1, keepdims=True)
    ecol = col - N_GROUPS
    in_e = (ecol >= g_idx * EXPERTS_PER_GROUP) & (ecol < (g_idx + 1) * EXPERTS_PER_GROUP)
    le = jnp.where(in_e, logits, -jnp.inf)
    l1 = jnp.max(le, axis=-1, keepdims=True)
    e1 = jnp.min(jnp.where(le == l1, ecol, big), axis=-1, keepdims=True)
    le2 = jnp.where(ecol == e1, -jnp.inf, le)
    l2 = jnp.max(le2, axis=-1, keepdims=True)
    e2 = jnp.min(jnp.where(le2 == l2, ecol, big), axis=-1, keepdims=True)
    w2 = jnp.exp(l2 - l1)
    den = 1.0 + w2
    re_ref[...] = jnp.where(col == 0, e1, jnp.where(col == 1, e2, 0))
    rg_ref[...] = jnp.where(col == 0, g_top / den, jnp.where(col == 1, g_top * w2 / den, 0.0))


def _out_router(att, gla, x, wo_att, wo_gla, n2, wr_hi, wr_lo, br, tm):
    n, d = x.shape
    row = lambda i: (i, 0)
    fix = lambda i: (0, 0)
    return pl.pallas_call(
        _out_router_kernel,
        grid=(n // tm,),
        in_specs=[
            pl.BlockSpec((tm, ATT_WIDTH), row),
            pl.BlockSpec((tm, GLA_VW), row),
            pl.BlockSpec((tm, d), row),
            pl.BlockSpec((ATT_WIDTH, d), fix),
            pl.BlockSpec((GLA_VW, d), fix),
            pl.BlockSpec((1, d), fix),
            pl.BlockSpec((d, LANES), fix),
            pl.BlockSpec((d, LANES), fix),
            pl.BlockSpec((1, LANES), fix),
        ],
        out_specs=[
            pl.BlockSpec((tm, d), row),
            pl.BlockSpec((tm, d), row),
            pl.BlockSpec((tm, LANES), row),
            pl.BlockSpec((tm, LANES), row),
        ],
        out_shape=[
            jax.ShapeDtypeStruct((n, d), F32),
            jax.ShapeDtypeStruct((n, d), F32),
            jax.ShapeDtypeStruct((n, LANES), I32),
            jax.ShapeDtypeStruct((n, LANES), F32),
        ],
        compiler_params=_params("parallel"),
        name="out_router",
    )(att, gla, x, wo_att, wo_gla, n2, wr_hi, wr_lo, br)


def _expert_kernel(be_ref, tok_ref, nused_ref, hn_hbm, wg_ref, wu_ref, wd_ref, o_ref, xbuf, sem, *, rows, n_blocks):
    i = pl.program_id(0)

    n_used = nused_ref[0]

    def start(blk, slot):
        def body(g, carry):
            for u in range(DMA_UNROLL):
                r = g * DMA_UNROLL + u
                tok = tok_ref[blk * rows + r]
                pltpu.make_async_copy(hn_hbm.at[pl.ds(tok, 1), :], xbuf.at[slot, pl.ds(r, 1), :],
                                      sem.at[slot]).start(priority=u % 2)
            return carry
        lax.fori_loop(0, rows // DMA_UNROLL, body, 0)

    @pl.when(i == 0)
    def _():
        start(0, 0)

    @pl.when(i + 1 < n_used)
    def _():
        start(i + 1, (i + 1) % 2)

    slot = i % 2

    @pl.when(i < n_used)
    def _():
        pltpu.make_async_copy(hn_hbm.at[pl.ds(0, rows), :], xbuf.at[slot], sem.at[slot]).wait()
        x = xbuf[slot].astype(BF16)
        a = _dot(x, wg_ref[0])
        u = _dot(x, wu_ref[0])
        act = (a * jax.nn.sigmoid(a) * u).astype(BF16)
        o_ref[...] = _dot(act, wd_ref[0])

    @pl.when(i >= n_used)
    def _():
        o_ref[...] = jnp.zeros_like(o_ref)


def _experts(block_expert, slot_tok, n_used, hn, wg, wu, wd):
    n_blocks = block_expert.shape[0]
    rows = EXPERT_ROWS
    d = hn.shape[1]
    f = wg.shape[2]
    grid_spec = pltpu.PrefetchScalarGridSpec(
        num_scalar_prefetch=3,
        grid=(n_blocks,),
        in_specs=[
            pl.BlockSpec(memory_space=pl.ANY),
            pl.BlockSpec((1, d, f), lambda i, be, tk, nu: (be[i], 0, 0)),
            pl.BlockSpec((1, d, f), lambda i, be, tk, nu: (be[i], 0, 0)),
            pl.BlockSpec((1, f, d), lambda i, be, tk, nu: (be[i], 0, 0)),
        ],
        out_specs=pl.BlockSpec((rows, d), lambda i, be, tk, nu: (i, 0)),
        scratch_shapes=[
            pltpu.VMEM((2, rows, d), F32),
            pltpu.SemaphoreType.DMA((2,)),
        ],
    )
    return pl.pallas_call(
        functools.partial(_expert_kernel, rows=rows, n_blocks=n_blocks),
        grid_spec=grid_spec,
        out_shape=jax.ShapeDtypeStruct((n_blocks * rows, d), F32),
        compiler_params=_params("arbitrary"),
        name="experts",
    )(block_expert, slot_tok, n_used, hn, wg, wu, wd)


def _combine_kernel(pos_ref, h_ref, rg_ref, ys_hbm, y_ref, buf, sem, *, rows, n_steps, tok0):
    i = pl.program_id(0)

    def start(step, slot):
        def body(g, carry):
            for u in range(DMA_UNROLL):
                r = g * DMA_UNROLL + u
                for k in range(EXPERT_TOPK):
                    p = pos_ref[(tok0 + step * rows + r) * EXPERT_TOPK + k]
                    pltpu.make_async_copy(ys_hbm.at[pl.ds(p, 1), :], buf.at[slot, k, pl.ds(r, 1), :],
                                          sem.at[slot]).start(priority=k % 2)
            return carry
        lax.fori_loop(0, rows // DMA_UNROLL, body, 0)

    @pl.when(i == 0)
    def _():
        start(0, 0)

    @pl.when(i + 1 < n_steps)
    def _():
        start(i + 1, (i + 1) % 2)

    slot = i % 2

    for k in range(EXPERT_TOPK):
        pltpu.make_async_copy(ys_hbm.at[pl.ds(0, rows), :], buf.at[slot, k], sem.at[slot]).wait()

    rg = rg_ref[...]
    y = h_ref[...]
    for k in range(EXPERT_TOPK):
        y = y + buf[slot, k] * rg[:, k:k + 1]
    y_ref[...] = y


def _combine(pos, h, rg, ys, tok0, n_tok, rows):
    d = h.shape[1]
    n_steps = n_tok // rows
    b0 = tok0 // rows
    grid_spec = pltpu.PrefetchScalarGridSpec(
        num_scalar_prefetch=1,
        grid=(n_steps,),
        in_specs=[
            pl.BlockSpec((rows, d), lambda i, p: (b0 + i, 0)),
            pl.BlockSpec((rows, LANES), lambda i, p: (b0 + i, 0)),
            pl.BlockSpec(memory_space=pl.ANY),
        ],
        out_specs=pl.BlockSpec((rows, d), lambda i, p: (i, 0)),
        scratch_shapes=[
            pltpu.VMEM((2, EXPERT_TOPK, rows, d), F32),
            pltpu.SemaphoreType.DMA((2,)),
        ],
    )
    return pl.pallas_call(
        functools.partial(_combine_kernel, rows=rows, n_steps=n_steps, tok0=tok0),
        grid_spec=grid_spec,
        out_shape=jax.ShapeDtypeStruct((n_tok, d), F32),
        compiler_params=_params("arbitrary"),
        name="combine",
    )(pos, h, rg, ys)


def _dispatch(expert):
    a = expert.shape[0]
    rows = EXPERT_ROWS
    onehot = (expert[:, None] == jnp.arange(N_EXPERTS, dtype=I32)[None, :]).astype(I32)
    csum = jnp.cumsum(onehot, axis=0)
    counts = csum[-1]
    rank = jnp.take_along_axis(csum, expert[:, None], axis=1)[:, 0] - 1
    padded = (counts + rows - 1) // rows * rows
    ends_p = jnp.cumsum(padded)
    pstart = ends_p - padded
    pos = (pstart[expert] + rank).astype(I32)
    n_blocks = -(-a // rows) + N_EXPERTS
    slot_tok = jnp.zeros((n_blocks * rows,), I32).at[pos].set(jnp.arange(a, dtype=I32) // EXPERT_TOPK)
    block_start = jnp.arange(n_blocks, dtype=I32) * rows
    block_expert = jnp.minimum(
        jnp.sum((ends_p[None, :] <= block_start[:, None]).astype(I32), axis=1), N_EXPERTS - 1).astype(I32)
    n_used = (ends_p[-1:] // rows).astype(I32)
    return pos, slot_tok, block_expert, n_used


def kernel(x_prompt, x_sample, cache_k, cache_v, state_gla, page_table, norm1_w, w_in, q_norm_w, k_norm_w,
           w_gla_a2, b_gla_a, gla_norm_w, w_out, norm2_w, w_r1, b_r1, w_r2, b_r2, w_e_gate, w_e_up, w_e_down):
    depth = w_in.shape[0]
    assert depth == 1
    batch, seq, d = x_prompt.shape
    db, nq, _ = x_sample.shape
    n_pages = page_table.shape[1]
    past = n_pages * PAGE_SIZE
    assert past % MOBA_BLOCK == 0 and nq <= GLA_CHUNK and seq % (8 * MOBA_BLOCK) == 0
    n_past_blocks = past // MOBA_BLOCK
    assert n_past_blocks + 8 <= LANES
    n_p, n_s = batch * seq, db * nq
    l = 0

    w = w_in[l]
    o_gq = 3 * ATT_WIDTH
    o_gv = o_gq + 2 * GLA_KW
    o_ga = o_gv + GLA_VW
    o_gg = o_ga + GLA_GATE_RANK
    w_main = jnp.concatenate([w[:, :o_ga], w[:, o_gg:]], axis=1).astype(BF16)
    w_ga = jnp.pad(w[:, o_ga:o_gg], ((0, 0), (0, LANES - GLA_GATE_RANK))).astype(BF16)
    n1 = norm1_w[l][None, :]
    qn = q_norm_w[l][None, :]
    kn = k_norm_w[l][None, :]
    wa = jnp.pad(w_gla_a2[l], ((0, LANES - GLA_GATE_RANK), (0, 0)))
    ba = b_gla_a[l][None, :]
    gnw = gla_norm_w[l][None, :]
    wo = w_out[l].astype(BF16)
    wo_att, wo_gla = wo[:ATT_WIDTH], wo[ATT_WIDTH:]
    n2 = norm2_w[l][None, :]
    wr = jnp.pad(jnp.concatenate([w_r1[l], w_r2[l]], axis=1), ((0, 0), (0, LANES - N_GROUPS - N_EXPERTS)))
    br = jnp.pad(jnp.concatenate([b_r1[l], b_r2[l]]), (0, LANES - N_GROUPS - N_EXPERTS))[None, :]
    wg = w_e_gate[l].astype(BF16)
    wu = w_e_up[l].astype(BF16)
    wd = w_e_down[l].astype(BF16)
    slopes = 2.0 ** (-8.0 * jnp.arange(1, N_ATT_HEADS + 1, dtype=F32) / N_ATT_HEADS)

    xp = x_prompt.reshape(n_p, d)
    q_p, k_p, kb_p, v_p, vt_p, gqk_p, gv_p, gg_p, ga_p = _project(xp, n1, w_main, w_ga, qn, kn, 512)
    kmeans = _block_means(k_p)
    att_p = _moba_prompt(slopes, q_p, kb_p, vt_p, kmeans, batch, seq)
    s0_p = jnp.zeros((batch, N_GLA_HEADS, GLA_DK, GLA_DV), F32)
    gla_p, s_p = _gla(gqk_p, gv_p, gg_p, ga_p, wa, ba, gnw, s0_p, batch, seq, seq)

    xs = x_sample.reshape(n_s, d)
    q_s, k_s, _, v_s, _, gqk_s, gv_s, gg_s, ga_s = _project(xs, n1, w_main, w_ga, qn, kn, n_s)
    block_sums = _page_block_sums(page_table, cache_k[l])
    idx = _sample_select(q_s, block_sums.reshape(db * n_past_blocks, ATT_WIDTH), k_s, db, nq, n_past_blocks)
    att_s = _moba_sample(page_table, idx[..., :MOBA_TOPK], slopes, q_s, k_s, v_s, cache_k[l], cache_v[l], db, nq)

    def pad_seq(a):
        return jnp.pad(a.reshape(db, nq, -1), ((0, 0), (0, GLA_CHUNK - nq), (0, 0))).reshape(db * GLA_CHUNK, -1)

    gla_s, s_s = _gla(pad_seq(gqk_s), pad_seq(gv_s), pad_seq(gg_s), pad_seq(ga_s), wa, ba, gnw,
                      state_gla[l], db, GLA_CHUNK, nq)
    gla_s = gla_s.reshape(db, GLA_CHUNK, GLA_VW)[:, :nq].reshape(n_s, GLA_VW)

    x_all = jnp.concatenate([xp, xs], axis=0)
    att_all = jnp.concatenate([att_p, att_s.astype(BF16)], axis=0)
    gla_all = jnp.concatenate([gla_p, gla_s], axis=0)
    wr_hi = wr.astype(BF16)
    wr_lo = (wr - wr_hi.astype(F32)).astype(BF16)
    h, hn, re, rg = _out_router(att_all, gla_all, x_all, wo_att, wo_gla, n2, wr_hi, wr_lo, br, 256)
    expert = re[:, :EXPERT_TOPK].reshape(-1)
    pos, slot_tok, block_expert, n_used = _dispatch(expert)
    ys = _experts(block_expert, slot_tok, n_used, hn, wg, wu, wd)
    y_p = _combine(pos, h, rg, ys, 0, n_p, 128)
    y_s = _combine(pos, h, rg, ys, n_p, n_s, 128)

    hd = (N_ATT_HEADS, HEAD_DIM)
    return (y_p.reshape(batch, seq, d), y_s.reshape(db, nq, d),
            k_p.reshape(1, batch, seq, *hd), v_p.reshape(1, batch, seq, *hd), s_p[None],
            k_s.reshape(1, db, nq, *hd), v_s.reshape(1, db, nq, *hd), s_s[None])
```

```python
import functools

import jax
import jax.numpy as jnp
from jax import lax
from jax.experimental import pallas as pl
from jax.experimental.pallas import tpu as pltpu

F32 = jnp.float32
BF16 = jnp.bfloat16
I32 = jnp.int32

HEAD_DIM = 128
N_ATT_HEADS = 8
ATT_WIDTH = N_ATT_HEADS * HEAD_DIM
MOBA_BLOCK = 256
MOBA_TOPK = 3
MOBA_HALF = 2
SELECT_ROWS = 2048
SUM_ROWS = 16
LOG2E = 1.4426950408889634
PAGE_SIZE = 128
PAGES_PER_BLOCK = MOBA_BLOCK // PAGE_SIZE
N_GLA_HEADS = 4
GLA_DK = 128
GLA_DV = 256
GLA_KW = N_GLA_HEADS * GLA_DK
GLA_VW = N_GLA_HEADS * GLA_DV
GLA_GATE_RANK = 16
GLA_TAU = 16.0
GLA_CHUNK = 64
N_GROUPS = 4
EXPERTS_PER_GROUP = 8
N_EXPERTS = N_GROUPS * EXPERTS_PER_GROUP
EXPERT_TOPK = 2
RMS_EPS = 1e-6
NEG_INF = -1e30
LANES = 128
EXPERT_ROWS = 256
DMA_UNROLL = 8
VMEM_LIMIT = 56 * 1024 * 1024

_NT = (((1,), (1,)), ((), ()))
_TN = (((0,), (0,)), ((), ()))
_HI = lax.Precision.HIGHEST


def _dot(a, b):
    return jnp.dot(a, b, preferred_element_type=F32)


def _dot_nt(a, b):
    return lax.dot_general(a, b, _NT, preferred_element_type=F32)


def _dot_tn(a, b):
    return lax.dot_general(a, b, _TN, preferred_element_type=F32)


def _params(*sem):
    return pltpu.CompilerParams(dimension_semantics=sem, vmem_limit_bytes=VMEM_LIMIT)


def _top3(gate, col, n_valid, axis=-1):
    picks = []
    g = gate
    for r in range(MOBA_TOPK):
        m = jnp.max(g, axis=axis, keepdims=True)
        idx = jnp.min(jnp.where(g == m, col, jnp.int32(2 ** 30)), axis=axis, keepdims=True)
        g = jnp.where(col == idx, -jnp.inf, g)
        picks.append(jnp.where(r < n_valid, idx, -1))
    return picks


def _proj_kernel(x_ref, n1_ref, w_ref, wga_ref, qn_ref, kn_ref,
                 q_ref, k_ref, kb_ref, v_ref, vt_ref, gqk_ref, gv_ref, gg_ref, ga_ref, xn_ref):
    j = pl.program_id(1)

    @pl.when(j == 0)
    def _():
        x = x_ref[...]
        y = x * lax.rsqrt(jnp.mean(x * x, axis=-1, keepdims=True) + RMS_EPS) * n1_ref[...]
        xn_ref[...] = y.astype(BF16)
        ga_ref[...] = _dot(xn_ref[...], wga_ref[...])

    z = _dot(xn_ref[...], w_ref[...])

    def head_norm(w):
        outs = []
        for h in range(N_ATT_HEADS):
            zh = z[:, h * HEAD_DIM:(h + 1) * HEAD_DIM]
            outs.append(zh * lax.rsqrt(jnp.mean(zh * zh, axis=-1, keepdims=True) + RMS_EPS) * w)
        return outs

    @pl.when(j == 0)
    def _():
        for h, y in enumerate(head_norm(qn_ref[...])):
            q_ref[:, h * HEAD_DIM:(h + 1) * HEAD_DIM] = y

    @pl.when(j == 1)
    def _():
        for h, y in enumerate(head_norm(kn_ref[...])):
            k_ref[:, h * HEAD_DIM:(h + 1) * HEAD_DIM] = y
            kb_ref[:, h * HEAD_DIM:(h + 1) * HEAD_DIM] = y.astype(BF16)

    @pl.when(j == 2)
    def _():
        v_ref[...] = z
        zt = z.T.astype(BF16)
        for c in range(vt_ref.shape[0]):
            vt_ref[c] = zt[:, c * MOBA_BLOCK:(c + 1) * MOBA_BLOCK]

    @pl.when(j == 3)
    def _():
        gqk_ref[...] = z

    @pl.when(j == 4)
    def _():
        gv_ref[...] = z

    @pl.when(j == 5)
    def _():
        gg_ref[...] = z


def _project(x, n1, w_main, w_ga, qn, kn, tm):
    n, d = x.shape
    wide = ATT_WIDTH
    row = lambda i, j: (i, 0)
    out_shape = [
        jax.ShapeDtypeStruct((n, wide), F32),
        jax.ShapeDtypeStruct((n, wide), F32),
        jax.ShapeDtypeStruct((n, wide), BF16),
        jax.ShapeDtypeStruct((n, wide), F32),
        jax.ShapeDtypeStruct((n // MOBA_BLOCK, wide, MOBA_BLOCK), BF16),
        jax.ShapeDtypeStruct((n, wide), F32),
        jax.ShapeDtypeStruct((n, wide), F32),
        jax.ShapeDtypeStruct((n, wide), F32),
        jax.ShapeDtypeStruct((n, LANES), F32),
    ]
    out_specs = [pl.BlockSpec((tm, s.shape[1]), row) if len(s.shape) == 2
                 else pl.BlockSpec((tm // MOBA_BLOCK, wide, MOBA_BLOCK), lambda i, j: (i, 0, 0)) for s in out_shape]
    return pl.pallas_call(
        _proj_kernel,
        grid=(n // tm, 6),
        in_specs=[
            pl.BlockSpec((tm, d), row),
            pl.BlockSpec((1, d), lambda i, j: (0, 0)),
            pl.BlockSpec((d, wide), lambda i, j: (0, j)),
            pl.BlockSpec((d, LANES), lambda i, j: (0, 0)),
            pl.BlockSpec((1, HEAD_DIM), lambda i, j: (0, 0)),
            pl.BlockSpec((1, HEAD_DIM), lambda i, j: (0, 0)),
        ],
        out_specs=out_specs,
        out_shape=out_shape,
        scratch_shapes=[pltpu.VMEM((tm, d), BF16)],
        compiler_params=_params("parallel", "arbitrary"),
        name="projection",
    )(x, n1, w_main, w_ga, qn, kn)


def _kmeans_kernel(k_ref, o_ref):
    rows = k_ref.shape[0]
    k = k_ref[...].reshape(rows // MOBA_BLOCK, MOBA_BLOCK, k_ref.shape[1])
    o_ref[...] = jnp.sum(k, axis=1) * (1.0 / MOBA_BLOCK)


def _block_means(k):
    n, w = k.shape
    rows = 8 * MOBA_BLOCK
    return pl.pallas_call(
        _kmeans_kernel,
        grid=(n // rows,),
        in_specs=[pl.BlockSpec((rows, w), lambda i: (i, 0))],
        out_specs=pl.BlockSpec((8, w), lambda i: (i, 0)),
        out_shape=jax.ShapeDtypeStruct((n // MOBA_BLOCK, w), F32),
        compiler_params=_params("parallel"),
        name="block_means",
    )(k)


def _prompt_select_kernel(q_ref, km_ref, o_ref, *, tiles_per_seq):
    c = pl.program_id(0)
    rows = q_ref.shape[0]
    nb = km_ref.shape[0]
    q = q_ref[...]
    km = km_ref[...]
    q_hi = q.astype(BF16)
    q_lo = (q - q_hi.astype(F32)).astype(BF16)
    km_hi = km.astype(BF16)
    km_lo = (km - km_hi.astype(F32)).astype(BF16)
    gate = _dot_nt(km_hi, q_hi) + _dot_nt(km_lo, q_hi) + _dot_nt(km_hi, q_lo)
    blk = lax.broadcasted_iota(I32, (nb, rows), 0)
    n_full = (c % tiles_per_seq) * (rows // MOBA_BLOCK) + lax.broadcasted_iota(I32, (1, rows), 1) // MOBA_BLOCK
    gate = jnp.where(blk < n_full, gate, NEG_INF)
    i0, i1, i2 = _top3(gate, blk, n_full, axis=0)
    row = lax.broadcasted_iota(I32, (8, rows), 0)
    o_ref[...] = jnp.where(row == 0, i0, jnp.where(row == 1, i1, jnp.where(row == 2, i2, -1)))


def _prompt_select(q, kmeans, seq):
    n = q.shape[0]
    rows = SELECT_ROWS
    nb = seq // MOBA_BLOCK
    tiles_per_seq = seq // rows
    return pl.pallas_call(
        functools.partial(_prompt_select_kernel, tiles_per_seq=tiles_per_seq),
        grid=(n // rows, N_ATT_HEADS),
        in_specs=[
            pl.BlockSpec((rows, HEAD_DIM), lambda c, h: (c, h)),
            pl.BlockSpec((nb, HEAD_DIM), lambda c, h: (c // tiles_per_seq, h)),
        ],
        out_specs=pl.BlockSpec((8, rows), lambda c, h: (h, c)),
        out_shape=jax.ShapeDtypeStruct((N_ATT_HEADS * 8, n), I32),
        compiler_params=_params("parallel", "parallel"),
        name="prompt_select",
    )(q, kmeans)


def _moba_prompt_kernel(pt_ref, slopes_ref, q_ref, k_ref, vt_ref, idx_ref, kc_hbm, o_ref, ps_ref,
                        sa_scr, sb_scr, bias_scr, pbuf, psem, *, group, n_steps):
    h = pl.program_id(1)
    i = pl.program_id(2)
    bs = MOBA_BLOCK
    nb = vt_ref.shape[0]
    t = (pl.program_id(0) * pl.num_programs(1) + h) * nb + i

    def page_copy(tt, slot, g):
        return pltpu.make_async_copy(kc_hbm.at[pt_ref[tt * group + g]], pbuf.at[slot, g], psem.at[slot])

    @pl.when(t == 0)
    def _():
        for g in range(group):
            page_copy(0, 0, g).start()

    @pl.when(t + 1 < n_steps)
    def _():
        for g in range(group):
            page_copy(t + 1, (t + 1) % 2, g).start()

    pslot = t % 2
    for g in range(group):
        page_copy(t, pslot, g).wait()
    for bk in range(group // PAGES_PER_BLOCK):
        acc = jnp.sum(pbuf[pslot, PAGES_PER_BLOCK * bk], axis=0)
        for p in range(1, PAGES_PER_BLOCK):
            acc = acc + jnp.sum(pbuf[pslot, PAGES_PER_BLOCK * bk + p], axis=0)
        ps_ref[bk] = acc

    slope2 = slopes_ref[h] * LOG2E
    q2 = (q_ref[...] * (HEAD_DIM ** -0.5 * LOG2E)).astype(BF16)
    i0 = idx_ref[0:1, :]
    i1 = idx_ref[1:2, :]
    i2 = idx_ref[2:3, :]

    krow = lax.broadcasted_iota(I32, (bs, bs), 0)
    qcol = lax.broadcasted_iota(I32, (bs, bs), 1)
    bias_scr[...] = slope2 * krow.astype(F32)
    ones_rows = jnp.ones((SUM_ROWS, bs), BF16)

    def pv_dot(jc, p):
        return _dot(jnp.concatenate([vt_ref[jc], ones_rows], axis=0), p.astype(BF16))

    def selected(j):
        return (i0 == j) | (i1 == j) | (i2 == j)

    def block_offset(j):
        return slope2 * ((j - i) * bs).astype(F32)

    hb = MOBA_HALF

    def sweep1(jb, s_ref):
        r0 = pl.multiple_of(jnp.minimum(jb, nb - hb) * bs, bs)
        s = _dot_nt(k_ref[pl.ds(r0, hb * bs), :], q2)
        cmax = jnp.full((1, bs), NEG_INF, F32)
        for u in range(hb):
            su = s[u * bs:(u + 1) * bs] + bias_scr[...]
            s_ref[u] = su
            cmax = jnp.maximum(cmax, jnp.where(selected(jb + u),
                                               jnp.max(su, axis=0, keepdims=True) + block_offset(jb + u), NEG_INF))
        return cmax

    def sweep2(jb, s_ref, m_prev, m_cur, l, acc):
        pv = jnp.zeros((HEAD_DIM + SUM_ROWS, bs), F32)
        for u in range(hb):
            j = jb + u
            ref = jnp.where(selected(j), m_cur - block_offset(j), -NEG_INF)
            pv = pv + pv_dot(jnp.minimum(j, nb - 1), jnp.exp2(s_ref[u] - ref))
        alpha = jnp.exp2(m_prev - m_cur)
        return alpha * l + pv[HEAD_DIM:HEAD_DIM + 1], alpha * acc + pv[:HEAD_DIM]

    r_own = pl.multiple_of(i * bs, bs)
    s = jnp.where(krow <= qcol, _dot_nt(k_ref[pl.ds(r_own, bs), :], q2) + bias_scr[...], NEG_INF)
    m = jnp.max(s, axis=0, keepdims=True)
    pv = pv_dot(i, jnp.exp2(s - m))
    acc = pv[:HEAD_DIM]
    l = pv[HEAD_DIM:HEAD_DIM + 1]

    m_a = jnp.maximum(m, sweep1(0, sa_scr))
    m_b = jnp.maximum(m_a, sweep1(hb, sb_scr))

    def chunk(c, carry):
        m_prev, m_a, m_b, l, acc = carry
        jb = c * (2 * hb)
        l, acc = sweep2(jb, sa_scr, m_prev, m_a, l, acc)
        m_a2 = jnp.maximum(m_b, sweep1(jb + 2 * hb, sa_scr))
        l, acc = sweep2(jb + hb, sb_scr, m_a, m_b, l, acc)
        m_b2 = jnp.maximum(m_a2, sweep1(jb + 3 * hb, sb_scr))
        return m_b, m_a2, m_b2, l, acc

    n_chunks = (i + 2 * hb - 1) // (2 * hb)
    _, _, _, l, acc = lax.fori_loop(0, n_chunks, chunk, (m, m_a, m_b, l, acc))
    o_ref[...] = (acc / l).T.astype(o_ref.dtype)


def _moba_prompt(page_table, slopes, q, kb, vt, idx, cache_k, batch, seq):
    n = q.shape[0]
    nb = seq // MOBA_BLOCK
    bs = MOBA_BLOCK
    n_steps = batch * N_ATT_HEADS * nb
    n_pages_total = page_table.size
    assert n_pages_total % (n_steps * PAGES_PER_BLOCK) == 0
    group = n_pages_total // n_steps
    bpc = group // PAGES_PER_BLOCK
    grid_spec = pltpu.PrefetchScalarGridSpec(
        num_scalar_prefetch=1,
        grid=(batch, N_ATT_HEADS, nb),
        in_specs=[
            pl.BlockSpec(memory_space=pltpu.SMEM),
            pl.BlockSpec((bs, HEAD_DIM), lambda b, h, i, pt: (b * nb + i, h)),
            pl.BlockSpec((seq, HEAD_DIM), lambda b, h, i, pt: (b, h)),
            pl.BlockSpec((nb, HEAD_DIM, bs), lambda b, h, i, pt: (b, h, 0)),
            pl.BlockSpec((8, bs), lambda b, h, i, pt: (h, b * nb + i)),
            pl.BlockSpec(memory_space=pl.ANY),
        ],
        out_specs=[
            pl.BlockSpec((bs, HEAD_DIM), lambda b, h, i, pt: (b * nb + i, h)),
            pl.BlockSpec((bpc, N_ATT_HEADS, HEAD_DIM), lambda b, h, i, pt: ((b * N_ATT_HEADS + h) * nb + i, 0, 0)),
        ],
        scratch_shapes=[
            pltpu.VMEM((MOBA_HALF, bs, bs), F32),
            pltpu.VMEM((MOBA_HALF, bs, bs), F32),
            pltpu.VMEM((bs, bs), F32),
            pltpu.VMEM((2, group, PAGE_SIZE, N_ATT_HEADS, HEAD_DIM), F32),
            pltpu.SemaphoreType.DMA((2,)),
        ],
    )
    return pl.pallas_call(
        functools.partial(_moba_prompt_kernel, group=group, n_steps=n_steps),
        grid_spec=grid_spec,
        out_shape=[
            jax.ShapeDtypeStruct((n, ATT_WIDTH), BF16),
            jax.ShapeDtypeStruct((n_steps * bpc, N_ATT_HEADS, HEAD_DIM), F32),
        ],
        compiler_params=_params("arbitrary", "arbitrary", "arbitrary"),
        name="moba_prompt",
    )(page_table.reshape(-1), slopes, q, kb, vt, idx, cache_k)


def _sample_select_kernel(q_ref, bs_ref, kn_ref, o_ref, *, n_past_blocks, past):
    nq = q_ref.shape[0]
    col = lax.broadcasted_iota(I32, (nq, LANES), 1)
    lane = col
    n_full = (past + lax.broadcasted_iota(I32, (nq, 1), 0)) // MOBA_BLOCK
    for h in range(N_ATT_HEADS):
        sl = slice(h * HEAD_DIM, (h + 1) * HEAD_DIM)
        m_past = bs_ref[:, sl] * (1.0 / MOBA_BLOCK)
        m_new = jnp.sum(kn_ref[:, sl], axis=0, keepdims=True) * (1.0 / MOBA_BLOCK)
        row8 = lax.broadcasted_iota(I32, (8, HEAD_DIM), 0)
        new_rows = jnp.where(row8 == 0, jnp.broadcast_to(m_new, (8, HEAD_DIM)), 0.0)
        means = jnp.concatenate(
            [m_past, new_rows, jnp.zeros((LANES - n_past_blocks - 8, HEAD_DIM), F32)], axis=0)
        gate = lax.dot_general(q_ref[:, sl], means, _NT, precision=_HI, preferred_element_type=F32)
        gate = jnp.where(col < n_full, gate, NEG_INF)
        i0, i1, i2 = _top3(gate, col, n_full)
        o_ref[0, h] = jnp.where(lane == 0, i0, jnp.where(lane == 1, i1, jnp.where(lane == 2, i2, 0)))


def _sample_select(q, block_sums, k_new, db, nq, n_past_blocks):
    return pl.pallas_call(
        functools.partial(_sample_select_kernel, n_past_blocks=n_past_blocks,
                          past=n_past_blocks * MOBA_BLOCK),
        grid=(db,),
        in_specs=[
            pl.BlockSpec((nq, ATT_WIDTH), lambda s: (s, 0)),
            pl.BlockSpec((n_past_blocks, ATT_WIDTH), lambda s: (s, 0)),
            pl.BlockSpec((nq, ATT_WIDTH), lambda s: (s, 0)),
        ],
        out_specs=pl.BlockSpec((1, N_ATT_HEADS, nq, LANES), lambda s: (s, 0, 0, 0)),
        out_shape=jax.ShapeDtypeStruct((db, N_ATT_HEADS, nq, LANES), I32),
        compiler_params=_params("parallel"),
        name="sample_select",
    )(q, block_sums, k_new)


def _moba_sample_kernel(pt_ref, idx_ref, slopes_ref, q_ref, kn_ref, vn_ref, kc_hbm, vc_hbm, o_ref,
                        kbuf, vbuf, kown, vown, sem, *, n_steps, n_pages, nq, past):
    t = pl.program_id(0)
    nh = N_ATT_HEADS
    n_sel = nq * MOBA_TOPK
    n_slots = n_sel * PAGES_PER_BLOCK

    def copies(tt, slot, g):
        s = tt // nh
        h = tt % nh
        blk = idx_ref[tt * n_sel + g // PAGES_PER_BLOCK]
        page = pt_ref[s * n_pages + blk * PAGES_PER_BLOCK + g % PAGES_PER_BLOCK]
        return (pltpu.make_async_copy(kc_hbm.at[page, :, h, :], kbuf.at[slot, g], sem.at[0, slot]),
                pltpu.make_async_copy(vc_hbm.at[page, :, h, :], vbuf.at[slot, g], sem.at[1, slot]))

    def start(tt, slot):
        for g in range(n_slots):
            ck, cv = copies(tt, slot, g)
            ck.start()
            cv.start()

    @pl.when(t == 0)
    def _():
        start(0, 0)
        kown[...] = jnp.zeros_like(kown)
        vown[...] = jnp.zeros_like(vown)

    @pl.when(t + 1 < n_steps)
    def _():
        start(t + 1, (t + 1) % 2)

    slot = t % 2
    h = t % nh
    slope = slopes_ref[h]
    scale = HEAD_DIM ** -0.5
    qb = q_ref[...].astype(BF16)
    n_keys = n_slots * PAGE_SIZE
    keys_per_q = MOBA_TOPK * MOBA_BLOCK

    kown[0:nq, :] = kn_ref[...]
    vown[0:nq, :] = vn_ref[...]
    row = lax.broadcasted_iota(I32, (nq, LANES), 0)
    colo = lax.broadcasted_iota(I32, (nq, LANES), 1)
    s_own = _dot_nt(qb, kown[...].astype(BF16)) * scale - slope * (row - colo).astype(F32)
    s_own = jnp.where(colo <= row, s_own, NEG_INF)

    colk = lax.broadcasted_iota(I32, (1, n_keys), 1)
    grp = colk // MOBA_BLOCK
    blk_of_col = jnp.zeros((1, n_keys), I32)
    for g in range(n_sel):
        blk_of_col = jnp.where(grp == g, idx_ref[t * n_sel + g], blk_of_col)
    pos = blk_of_col * MOBA_BLOCK + colk % MOBA_BLOCK
    rowk = lax.broadcasted_iota(I32, (nq, n_keys), 0)
    mine = (lax.broadcasted_iota(I32, (nq, n_keys), 1) // keys_per_q) == rowk
    n_full = (past + rowk) // MOBA_BLOCK
    pick = (lax.broadcasted_iota(I32, (nq, n_keys), 1) // MOBA_BLOCK) % MOBA_TOPK
    valid = mine & (pick < n_full)

    for g in range(n_slots):
        ck, cv = copies(t, slot, g)
        ck.wait()
        cv.wait()

    kall = kbuf[slot].reshape(n_keys, HEAD_DIM).astype(BF16)
    s_sel = _dot_nt(qb, kall) * scale - slope * ((past + rowk) - pos).astype(F32)
    s_sel = jnp.where(valid, s_sel, NEG_INF)
    m = jnp.maximum(jnp.max(s_sel, axis=-1, keepdims=True), jnp.max(s_own, axis=-1, keepdims=True))
    p_sel = jnp.exp(s_sel - m)
    p_own = jnp.exp(s_own - m)
    l = jnp.sum(p_sel, axis=-1, keepdims=True) + jnp.sum(p_own, axis=-1, keepdims=True)
    vall = vbuf[slot].reshape(n_keys, HEAD_DIM).astype(BF16)
    o = _dot(p_sel.astype(BF16), vall) + _dot(p_own.astype(BF16), vown[...].astype(BF16))
    o_ref[...] = o / l


def _moba_sample(page_table, idx, slopes, q, k_new, v_new, cache_k, cache_v, db, nq):
    n_pages = page_table.shape[1]
    n_steps = db * N_ATT_HEADS
    n_slots = nq * MOBA_TOPK * PAGES_PER_BLOCK
    qspec = pl.BlockSpec((nq, HEAD_DIM), lambda t, pt, ix: (t // N_ATT_HEADS, t % N_ATT_HEADS))
    grid_spec = pltpu.PrefetchScalarGridSpec(
        num_scalar_prefetch=2,
        grid=(n_steps,),
        in_specs=[
            pl.BlockSpec(memory_space=pltpu.SMEM),
            qspec, qspec, qspec,
            pl.BlockSpec(memory_space=pl.ANY),
            pl.BlockSpec(memory_space=pl.ANY),
        ],
        out_specs=qspec,
        scratch_shapes=[
            pltpu.VMEM((2, n_slots, PAGE_SIZE, HEAD_DIM), F32),
            pltpu.VMEM((2, n_slots, PAGE_SIZE, HEAD_DIM), F32),
            pltpu.VMEM((LANES, HEAD_DIM), F32),
            pltpu.VMEM((LANES, HEAD_DIM), F32),
            pltpu.SemaphoreType.DMA((2, 2)),
        ],
    )
    return pl.pallas_call(
        functools.partial(_moba_sample_kernel, n_steps=n_steps, n_pages=n_pages, nq=nq,
                          past=n_pages * PAGE_SIZE),
        grid_spec=grid_spec,
        out_shape=jax.ShapeDtypeStruct((db * nq, ATT_WIDTH), F32),
        compiler_params=_params("arbitrary"),
        name="moba_sample",
    )(page_table.reshape(-1), idx.reshape(-1), slopes, q, k_new, v_new, cache_k, cache_v)


def _gla_kernel(gq_ref, gk_ref, gv_ref, gg_ref, ga_ref, wah_ref, wal_ref, ba_ref, nw_ref, s0_ref,
                o_ref, sfin_ref, s_scr, qd_scr, oi_scr, ds_scr, dec_scr, *, chunk, n_sub, t_valid):
    t = pl.program_id(2)

    @pl.when(t == 0)
    def _():
        s_scr[...] = s0_ref[0, 0]

    c = chunk
    tri = lax.broadcasted_iota(I32, (c, c), 0) >= lax.broadcasted_iota(I32, (c, c), 1)
    tri_b = jnp.where(tri, 1.0, 0.0).astype(BF16)
    ones_b = jnp.ones((c, GLA_DK), BF16)

    def split3(x):
        hi = x.astype(BF16)
        r = x - hi.astype(F32)
        mid = r.astype(BF16)
        return hi, mid, (r - mid.astype(F32)).astype(BF16)

    ga = ga_ref[...]
    ga_hi = ga.astype(BF16)
    ga_lo = (ga - ga_hi.astype(F32)).astype(BF16)
    pre = _dot(ga_hi, wah_ref[...]) + _dot(ga_hi, wal_ref[...]) + _dot(ga_lo, wah_ref[...]) + ba_ref[...]
    la = (jnp.minimum(pre, 0.0) - jnp.log1p(jnp.exp(-jnp.abs(pre)))) * (1.0 / GLA_TAU)
    if t_valid < c * n_sub:
        la = jnp.where(lax.broadcasted_iota(I32, la.shape, 0) < t_valid, la, 0.0)

    for ci in range(n_sub):
        sl = slice(ci * c, (ci + 1) * c)
        parts = split3(la[sl])
        b = _dot(tri_b, parts[0]) + _dot(tri_b, parts[1]) + _dot(tri_b, parts[2])
        b_end = b[c - 1:c, :]
        b_end_col = _dot_tn(parts[0], ones_b) + _dot_tn(parts[1], ones_b) + _dot_tn(parts[2], ones_b)
        q = gq_ref[sl, :] * (GLA_DK ** -0.5)
        k = gk_ref[sl, :]
        v = gv_ref[sl, :].astype(BF16)
        q_dec = (q * jnp.exp(b)).astype(BF16)
        k_inv = (k * jnp.exp(-b)).astype(BF16)
        k_end = (k * jnp.exp(b_end - b)).astype(BF16)
        a = jnp.where(tri, _dot_nt(q_dec, k_inv), 0.0)
        qd_scr[sl, :] = q_dec
        oi_scr[sl, :] = _dot(a.astype(BF16), v)
        ds_scr[ci] = _dot_tn(k_end, v)
        dec_scr[ci] = jnp.exp(b_end_col)

    state = s_scr[...]
    nw = nw_ref[...]
    for ci in range(n_sub):
        sl = slice(ci * c, (ci + 1) * c)
        o = oi_scr[sl, :] + _dot(qd_scr[sl, :], state.astype(BF16))
        state = jnp.concatenate([dec_scr[ci]] * (GLA_DV // GLA_DK), axis=1) * state + ds_scr[ci]
        on = o * lax.rsqrt(jnp.mean(o * o, axis=-1, keepdims=True) + RMS_EPS) * nw
        g = gg_ref[sl, :]
        o_ref[sl, :] = (on * (g * jax.nn.sigmoid(g))).astype(o_ref.dtype)
    s_scr[...] = state

    @pl.when(t == pl.num_programs(2) - 1)
    def _():
        sfin_ref[0, 0] = state


def _gla(gqk, gv, gg, ga, wa_hi, wa_lo, ba, nw, s0, batch, seq, t_valid):
    n = gqk.shape[0]
    c = GLA_CHUNK
    n_sub = min(8, seq // c)
    rows = c * n_sub
    steps = seq // rows
    nh = N_GLA_HEADS
    return pl.pallas_call(
        functools.partial(_gla_kernel, chunk=c, n_sub=n_sub, t_valid=t_valid),
        grid=(batch, nh, steps),
        in_specs=[
            pl.BlockSpec((rows, GLA_DK), lambda b, h, t: (b * steps + t, h)),
            pl.BlockSpec((rows, GLA_DK), lambda b, h, t: (b * steps + t, nh + h)),
            pl.BlockSpec((rows, GLA_DV), lambda b, h, t: (b * steps + t, h)),
            pl.BlockSpec((rows, GLA_DV), lambda b, h, t: (b * steps + t, h)),
            pl.BlockSpec((rows, LANES), lambda b, h, t: (b * steps + t, 0)),
            pl.BlockSpec((LANES, GLA_DK), lambda b, h, t: (0, h)),
            pl.BlockSpec((LANES, GLA_DK), lambda b, h, t: (0, h)),
            pl.BlockSpec((1, GLA_DK), lambda b, h, t: (0, h)),
            pl.BlockSpec((1, GLA_DV), lambda b, h, t: (0, 0)),
            pl.BlockSpec((1, 1, GLA_DK, GLA_DV), lambda b, h, t: (b, h, 0, 0)),
        ],
        out_specs=[
            pl.BlockSpec((rows, GLA_DV), lambda b, h, t: (b * steps + t, h)),
            pl.BlockSpec((1, 1, GLA_DK, GLA_DV), lambda b, h, t: (b, h, 0, 0)),
        ],
        out_shape=[
            jax.ShapeDtypeStruct((n, GLA_VW), BF16),
            jax.ShapeDtypeStruct(s0.shape, F32),
        ],
        scratch_shapes=[
            pltpu.VMEM((GLA_DK, GLA_DV), F32),
            pltpu.VMEM((rows, GLA_DK), BF16),
            pltpu.VMEM((rows, GLA_DV), F32),
            pltpu.VMEM((n_sub, GLA_DK, GLA_DV), F32),
            pltpu.VMEM((n_sub, GLA_DK, GLA_DK), F32),
        ],
        compiler_params=_params("parallel", "parallel", "arbitrary"),
        name="gla",
    )(gqk, gqk, gv, gg, ga, wa_hi, wa_lo, ba, nw, s0)


def _out_router_kernel(att_ref, gla_ref, x_ref, woa_ref, wog_ref, n2_ref, wrh_ref, wrl_ref, br_ref,
                       h_ref, hn_ref, re_ref, rg_ref):
    h = x_ref[...] + _dot(att_ref[...], woa_ref[...]) + _dot(gla_ref[...], wog_ref[...])
    h_ref[...] = h
    hn = h * lax.rsqrt(jnp.mean(h * h, axis=-1, keepdims=True) + RMS_EPS) * n2_ref[...]
    hn_ref[...] = hn
    hn_hi = hn.astype(BF16)
    hn_lo = (hn - hn_hi.astype(F32)).astype(BF16)
    logits = (_dot(hn_hi, wrh_ref[...]) + _dot(hn_hi, wrl_ref[...]) + _dot(hn_lo, wrh_ref[...])) + br_ref[...]
    col = lax.broadcasted_iota(I32, logits.shape, 1)
    big = jnp.int32(2 ** 30)
    in_g = col < N_GROUPS
    lg = jnp.where(in_g, logits, -jnp.inf)
    m1 = jnp.max(lg, axis=-1, keepdims=True)
    g_idx = jnp.min(jnp.where(lg == m1, col, big), axis=-1, keepdims=True)
    g_top = 1.0 / jnp.sum(jnp.exp(lg - m1), axis=-1, keepdims=True)
    ecol = col - N_GROUPS
    in_e = (ecol >= g_idx * EXPERTS_PER_GROUP) & (ecol < (g_idx + 1) * EXPERTS_PER_GROUP)
    le = jnp.where(in_e, logits, -jnp.inf)
    l1 = jnp.max(le, axis=-1, keepdims=True)
    e1 = jnp.min(jnp.where(le == l1, ecol, big), axis=-1, keepdims=True)
    le2 = jnp.where(ecol == e1, -jnp.inf, le)
    l2 = jnp.max(le2, axis=-1, keepdims=True)
    e2 = jnp.min(jnp.where(le2 == l2, ecol, big), axis=-1, keepdims=True)
    w2 = jnp.exp(l2 - l1)
    den = 1.0 + w2
    re_ref[...] = jnp.where(col == 0, e1, jnp.where(col == 1, e2, 0))
    rg_ref[...] = jnp.where(col == 0, g_top / den, jnp.where(col == 1, g_top * w2 / den, 0.0))


def _out_router_into_kernel(*refs):
    _out_router_kernel(*refs[:9], *refs[10:])


def _out_router_alloc_kernel(*refs, n_real):
    i = pl.program_id(0)

    @pl.when(i < n_real)
    def _():
        _out_router_kernel(*refs)

    @pl.when(i >= n_real)
    def _():
        refs[10][...] = jnp.zeros_like(refs[10])


def _out_router(att, gla, x, wo_att, wo_gla, n2, wr_hi, wr_lo, br, tm, n_all, row0, hn_all=None):
    n, d = x.shape
    n_real = n // tm
    row = lambda i: (jnp.minimum(i, n_real - 1), 0)
    fix = lambda i: (0, 0)
    b0 = row0 // tm
    if hn_all is None:
        assert row0 == 0 and n_all % tm == 0
        body, steps = functools.partial(_out_router_alloc_kernel, n_real=n_real), n_all // tm
        extra_in, extra_args, alias = [], [], {}
    else:
        body, steps = _out_router_into_kernel, n_real
        extra_in, extra_args, alias = [pl.BlockSpec(memory_space=pl.ANY)], [hn_all], {9: 1}
    return pl.pallas_call(
        body,
        grid=(steps,),
        input_output_aliases=alias,
        in_specs=[
            pl.BlockSpec((tm, ATT_WIDTH), row),
            pl.BlockSpec((tm, GLA_VW), row),
            pl.BlockSpec((tm, d), row),
            pl.BlockSpec((ATT_WIDTH, d), fix),
            pl.BlockSpec((GLA_VW, d), fix),
            pl.BlockSpec((1, d), fix),
            pl.BlockSpec((d, LANES), fix),
            pl.BlockSpec((d, LANES), fix),
            pl.BlockSpec((1, LANES), fix),
        ] + extra_in,
        out_specs=[
            pl.BlockSpec((tm, d), row),
            pl.BlockSpec((tm, d), lambda i: (b0 + i, 0)),
            pl.BlockSpec((tm, LANES), row),
            pl.BlockSpec((tm, LANES), row),
        ],
        out_shape=[
            jax.ShapeDtypeStruct((n, d), F32),
            jax.ShapeDtypeStruct((n_all, d), F32),
            jax.ShapeDtypeStruct((n, LANES), I32),
            jax.ShapeDtypeStruct((n, LANES), F32),
        ],
        compiler_params=_params("arbitrary"),
        name="out_router",
    )(att, gla, x, wo_att, wo_gla, n2, wr_hi, wr_lo, br, *extra_args)


def _expert_kernel(be_ref, tok_ref, nused_ref, hn_hbm, wg_ref, wu_ref, wd_ref, o_ref,
                   xbuf, wgb, wub, wdb, sem, *, rows, n_blocks):
    i = pl.program_id(0)

    @pl.when((i == 0) | (be_ref[i] != be_ref[jnp.maximum(i - 1, 0)]))
    def _():
        wgb[...] = wg_ref[0].astype(BF16)
        wub[...] = wu_ref[0].astype(BF16)
        wdb[...] = wd_ref[0].astype(BF16)

    n_used = nused_ref[0]

    def start(blk, slot):
        def body(g, carry):
            for u in range(DMA_UNROLL):
                r = g * DMA_UNROLL + u
                tok = tok_ref[blk * rows + r]
                pltpu.make_async_copy(hn_hbm.at[pl.ds(tok, 1), :], xbuf.at[slot, pl.ds(r, 1), :],
                                      sem.at[slot]).start(priority=u % 2)
            return carry
        lax.fori_loop(0, rows // DMA_UNROLL, body, 0)

    @pl.when(i == 0)
    def _():
        start(0, 0)

    @pl.when(i + 1 < n_used)
    def _():
        start(i + 1, (i + 1) % 2)

    slot = i % 2

    @pl.when(i < n_used)
    def _():
        pltpu.make_async_copy(hn_hbm.at[pl.ds(0, rows), :], xbuf.at[slot], sem.at[slot]).wait()
        x = xbuf[slot].astype(BF16)
        a = _dot(x, wgb[...])
        u = _dot(x, wub[...])
        act = (a * jax.nn.sigmoid(a) * u).astype(BF16)
        o_ref[...] = _dot(act, wdb[...])

    @pl.when(i >= n_used)
    def _():
        o_ref[...] = jnp.zeros_like(o_ref)


def _experts(block_expert, slot_tok, n_used, hn, wg, wu, wd):
    n_blocks = block_expert.shape[0]
    rows = EXPERT_ROWS
    d = hn.shape[1]
    f = wg.shape[2]
    grid_spec = pltpu.PrefetchScalarGridSpec(
        num_scalar_prefetch=3,
        grid=(n_blocks,),
        in_specs=[
            pl.BlockSpec(memory_space=pl.ANY),
            pl.BlockSpec((1, d, f), lambda i, be, tk, nu: (be[i], 0, 0)),
            pl.BlockSpec((1, d, f), lambda i, be, tk, nu: (be[i], 0, 0)),
            pl.BlockSpec((1, f, d), lambda i, be, tk, nu: (be[i], 0, 0)),
        ],
        out_specs=pl.BlockSpec((rows, d), lambda i, be, tk, nu: (i, 0)),
        scratch_shapes=[
            pltpu.VMEM((2, rows, d), F32),
            pltpu.VMEM((d, f), BF16),
            pltpu.VMEM((d, f), BF16),
            pltpu.VMEM((f, d), BF16),
            pltpu.SemaphoreType.DMA((2,)),
        ],
    )
    return pl.pallas_call(
        functools.partial(_expert_kernel, rows=rows, n_blocks=n_blocks),
        grid_spec=grid_spec,
        out_shape=jax.ShapeDtypeStruct((n_blocks * rows, d), F32),
        compiler_params=_params("arbitrary"),
        name="experts",
    )(block_expert, slot_tok, n_used, hn, wg, wu, wd)


def _combine_kernel(pos_ref, h_ref, rg_ref, ys_hbm, y_ref, buf, sem, *, rows, n_steps, tok0):
    i = pl.program_id(0)

    def start(step, slot):
        def body(g, carry):
            for u in range(DMA_UNROLL):
                r = g * DMA_UNROLL + u
                for k in range(EXPERT_TOPK):
                    p = pos_ref[(tok0 + step * rows + r) * EXPERT_TOPK + k]
                    pltpu.make_async_copy(ys_hbm.at[pl.ds(p, 1), :], buf.at[slot, k, pl.ds(r, 1), :],
                                          sem.at[slot]).start(priority=k % 2)
            return carry
        lax.fori_loop(0, rows // DMA_UNROLL, body, 0)

    @pl.when(i == 0)
    def _():
        start(0, 0)

    @pl.when(i + 1 < n_steps)
    def _():
        start(i + 1, (i + 1) % 2)

    slot = i % 2

    for k in range(EXPERT_TOPK):
        pltpu.make_async_copy(ys_hbm.at[pl.ds(0, rows), :], buf.at[slot, k], sem.at[slot]).wait()

    rg = rg_ref[...]
    y = h_ref[...]
    for k in range(EXPERT_TOPK):
        y = y + buf[slot, k] * rg[:, k:k + 1]
    y_ref[...] = y


def _combine(pos, h, rg, ys, tok0, rows):
    n_tok, d = h.shape
    n_steps = n_tok // rows
    grid_spec = pltpu.PrefetchScalarGridSpec(
        num_scalar_prefetch=1,
        grid=(n_steps,),
        in_specs=[
            pl.BlockSpec((rows, d), lambda i, p: (i, 0)),
            pl.BlockSpec((rows, LANES), lambda i, p: (i, 0)),
            pl.BlockSpec(memory_space=pl.ANY),
        ],
        out_specs=pl.BlockSpec((rows, d), lambda i, p: (i, 0)),
        scratch_shapes=[
            pltpu.VMEM((2, EXPERT_TOPK, rows, d), F32),
            pltpu.SemaphoreType.DMA((2,)),
        ],
    )
    return pl.pallas_call(
        functools.partial(_combine_kernel, rows=rows, n_steps=n_steps, tok0=tok0),
        grid_spec=grid_spec,
        out_shape=jax.ShapeDtypeStruct((n_tok, d), F32),
        compiler_params=_params("arbitrary"),
        name="combine",
    )(pos, h, rg, ys)


def _dispatch(expert):
    a = expert.shape[0]
    rows = EXPERT_ROWS
    onehot = (expert[:, None] == jnp.arange(N_EXPERTS, dtype=I32)[None, :]).astype(I32)
    csum = jnp.cumsum(onehot, axis=0)
    counts = csum[-1]
    rank = jnp.take_along_axis(csum, expert[:, None], axis=1)[:, 0] - 1
    padded = (counts + rows - 1) // rows * rows
    ends_p = jnp.cumsum(padded)
    pstart = ends_p - padded
    pos = (pstart[expert] + rank).astype(I32)
    n_blocks = -(-a // rows) + N_EXPERTS
    slot_tok = jnp.zeros((n_blocks * rows,), I32).at[pos].set(jnp.arange(a, dtype=I32) // EXPERT_TOPK)
    block_start = jnp.arange(n_blocks, dtype=I32) * rows
    block_expert = jnp.minimum(
        jnp.sum((ends_p[None, :] <= block_start[:, None]).astype(I32), axis=1), N_EXPERTS - 1).astype(I32)
    n_used = (ends_p[-1:] // rows).astype(I32)
    return pos, slot_tok, block_expert, n_used


def kernel(x_prompt, x_sample, cache_k, cache_v, state_gla, page_table, norm1_w, w_in, q_norm_w, k_norm_w,
           w_gla_a2, b_gla_a, gla_norm_w, w_out, norm2_w, w_r1, b_r1, w_r2, b_r2, w_e_gate, w_e_up, w_e_down):
    depth = w_in.shape[0]
    assert depth == 1
    batch, seq, d = x_prompt.shape
    db, nq, _ = x_sample.shape
    n_pages = page_table.shape[1]
    past = n_pages * PAGE_SIZE
    assert past % MOBA_BLOCK == 0 and nq <= GLA_CHUNK and seq % (8 * MOBA_BLOCK) == 0
    n_past_blocks = past // MOBA_BLOCK
    assert n_past_blocks + 8 <= LANES
    n_p, n_s = batch * seq, db * nq
    l = 0

    w = w_in[l]
    o_gq = 3 * ATT_WIDTH
    o_gv = o_gq + 2 * GLA_KW
    o_ga = o_gv + GLA_VW
    o_gg = o_ga + GLA_GATE_RANK
    w_main = jnp.concatenate([w[:, :o_ga], w[:, o_gg:]], axis=1).astype(BF16)
    w_ga = jnp.pad(w[:, o_ga:o_gg], ((0, 0), (0, LANES - GLA_GATE_RANK))).astype(BF16)
    n1 = norm1_w[l][None, :]
    qn = q_norm_w[l][None, :]
    kn = k_norm_w[l][None, :]
    wa = jnp.pad(w_gla_a2[l], ((0, LANES - GLA_GATE_RANK), (0, 0)))
    wa_hi = wa.astype(BF16)
    wa_lo = (wa - wa_hi.astype(F32)).astype(BF16)
    ba = b_gla_a[l][None, :]
    gnw = gla_norm_w[l][None, :]
    wo = w_out[l].astype(BF16)
    wo_att, wo_gla = wo[:ATT_WIDTH], wo[ATT_WIDTH:]
    n2 = norm2_w[l][None, :]
    wr = jnp.pad(jnp.concatenate([w_r1[l], w_r2[l]], axis=1), ((0, 0), (0, LANES - N_GROUPS - N_EXPERTS)))
    br = jnp.pad(jnp.concatenate([b_r1[l], b_r2[l]]), (0, LANES - N_GROUPS - N_EXPERTS))[None, :]
    slopes = 2.0 ** (-8.0 * jnp.arange(1, N_ATT_HEADS + 1, dtype=F32) / N_ATT_HEADS)

    xp = x_prompt.reshape(n_p, d)
    q_p, k_p, kb_p, v_p, vt_p, gqk_p, gv_p, gg_p, ga_p = _project(xp, n1, w_main, w_ga, qn, kn, 512)
    kmeans = _block_means(k_p)
    idx_p = _prompt_select(q_p, kmeans, seq)
    att_p, block_sums = _moba_prompt(page_table, slopes, q_p, kb_p, vt_p, idx_p, cache_k[l], batch, seq)
    s0_p = jnp.zeros((batch, N_GLA_HEADS, GLA_DK, GLA_DV), F32)
    gla_p, s_p = _gla(gqk_p, gv_p, gg_p, ga_p, wa_hi, wa_lo, ba, gnw, s0_p, batch, seq, seq)

    xs = x_sample.reshape(n_s, d)
    q_s, k_s, _, v_s, _, gqk_s, gv_s, gg_s, ga_s = _project(xs, n1, w_main, w_ga, qn, kn, n_s)
    idx = _sample_select(q_s, block_sums.reshape(db * n_past_blocks, ATT_WIDTH), k_s, db, nq, n_past_blocks)
    att_s = _moba_sample(page_table, idx[..., :MOBA_TOPK], slopes, q_s, k_s, v_s, cache_k[l], cache_v[l], db, nq)

    def pad_seq(a):
        return jnp.pad(a.reshape(db, nq, -1), ((0, 0), (0, GLA_CHUNK - nq), (0, 0))).reshape(db * GLA_CHUNK, -1)

    gla_s, s_s = _gla(pad_seq(gqk_s), pad_seq(gv_s), pad_seq(gg_s), pad_seq(ga_s), wa_hi, wa_lo, ba, gnw,
                      state_gla[l], db, GLA_CHUNK, nq)
    gla_s = gla_s.reshape(db, GLA_CHUNK, GLA_VW)[:, :nq].reshape(n_s, GLA_VW)

    wr_hi = wr.astype(BF16)
    wr_lo = (wr - wr_hi.astype(F32)).astype(BF16)
    n_all = n_p + n_s
    h_p, hn, re_p, rg_p = _out_router(att_p, gla_p, xp, wo_att, wo_gla, n2, wr_hi, wr_lo, br, 256, n_all, 0)
    h_s, hn, re_s, rg_s = _out_router(att_s.astype(BF16), gla_s, xs, wo_att, wo_gla, n2, wr_hi, wr_lo, br,
                                      n_s, n_all, n_p, hn)
    expert = jnp.concatenate([re_p[:, :EXPERT_TOPK], re_s[:, :EXPERT_TOPK]], axis=0).reshape(-1)
    pos, slot_tok, block_expert, n_used = _dispatch(expert)
    ys = _experts(block_expert, slot_tok, n_used, hn, w_e_gate[l], w_e_up[l], w_e_down[l])
    y_p = _combine(pos, h_p, rg_p, ys, 0, 128)
    y_s = _combine(pos, h_s, rg_s, ys, n_p, 128)

    hd = (N_ATT_HEADS, HEAD_DIM)
    return (y_p.reshape(batch, seq, d), y_s.reshape(db, nq, d),
            k_p.reshape(1, batch, seq, *hd), v_p.reshape(1, batch, seq, *hd), s_p[None],
            k_s.reshape(1, db, nq, *hd), v_s.reshape(1, db, nq, *hd), s_s[None])
```

```python
import functools

import jax
import jax.numpy as jnp
from jax import lax
from jax.experimental import pallas as pl
from jax.experimental.pallas import tpu as pltpu

F32 = jnp.float32
BF16 = jnp.bfloat16
I32 = jnp.int32

HEAD_DIM = 128
N_ATT_HEADS = 8
ATT_WIDTH = N_ATT_HEADS * HEAD_DIM
MOBA_BLOCK = 256
MOBA_TOPK = 3
MOBA_HALF = 2
HEADS_PER_STEP = 2
SELECT_ROWS = 2048
SUM_ROWS = 16
LOG2E = 1.4426950408889634
PAGE_SIZE = 128
PAGES_PER_BLOCK = MOBA_BLOCK // PAGE_SIZE
N_GLA_HEADS = 4
GLA_DK = 128
GLA_DV = 256
GLA_KW = N_GLA_HEADS * GLA_DK
GLA_VW = N_GLA_HEADS * GLA_DV
GLA_GATE_RANK = 16
GLA_TAU = 16.0
GLA_CHUNK = 64
N_GROUPS = 4
EXPERTS_PER_GROUP = 8
N_EXPERTS = N_GROUPS * EXPERTS_PER_GROUP
EXPERT_TOPK = 2
RMS_EPS = 1e-6
NEG_INF = -1e30
LANES = 128
EXPERT_ROWS = 256
DMA_UNROLL = 8
VMEM_LIMIT = 56 * 1024 * 1024

_NT = (((1,), (1,)), ((), ()))
_TN = (((0,), (0,)), ((), ()))
_HI = lax.Precision.HIGHEST


def _dot(a, b):
    return jnp.dot(a, b, preferred_element_type=F32)


def _dot_nt(a, b):
    return lax.dot_general(a, b, _NT, preferred_element_type=F32)


def _dot_tn(a, b):
    return lax.dot_general(a, b, _TN, preferred_element_type=F32)


def _params(*sem):
    return pltpu.CompilerParams(dimension_semantics=sem, vmem_limit_bytes=VMEM_LIMIT)


def _top3(gate, col, n_valid, axis=-1):
    picks = []
    g = gate
    for r in range(MOBA_TOPK):
        m = jnp.max(g, axis=axis, keepdims=True)
        idx = jnp.min(jnp.where(g == m, col, jnp.int32(2 ** 30)), axis=axis, keepdims=True)
        g = jnp.where(col == idx, -jnp.inf, g)
        picks.append(jnp.where(r < n_valid, idx, -1))
    return picks


def _proj_kernel(x_ref, n1_ref, w_ref, wga_ref, qn_ref, kn_ref,
                 q_ref, k_ref, kb_ref, v_ref, vt_ref, gqk_ref, gv_ref, gg_ref, ga_ref, xn_ref):
    j = pl.program_id(1)

    @pl.when(j == 0)
    def _():
        x = x_ref[...]
        y = x * lax.rsqrt(jnp.mean(x * x, axis=-1, keepdims=True) + RMS_EPS) * n1_ref[...]
        xn_ref[...] = y.astype(BF16)
        ga_ref[...] = _dot(xn_ref[...], wga_ref[...])

    z = _dot(xn_ref[...], w_ref[...])

    def head_norm(w):
        outs = []
        for h in range(N_ATT_HEADS):
            zh = z[:, h * HEAD_DIM:(h + 1) * HEAD_DIM]
            outs.append(zh * lax.rsqrt(jnp.mean(zh * zh, axis=-1, keepdims=True) + RMS_EPS) * w)
        return outs

    @pl.when(j == 0)
    def _():
        for h, y in enumerate(head_norm(qn_ref[...])):
            q_ref[:, h * HEAD_DIM:(h + 1) * HEAD_DIM] = y

    @pl.when(j == 1)
    def _():
        for h, y in enumerate(head_norm(kn_ref[...])):
            k_ref[:, h * HEAD_DIM:(h + 1) * HEAD_DIM] = y
            kb_ref[:, h * HEAD_DIM:(h + 1) * HEAD_DIM] = y.astype(BF16)

    @pl.when(j == 2)
    def _():
        v_ref[...] = z
        zt = z.T.astype(BF16)
        for c in range(vt_ref.shape[0]):
            vt_ref[c] = zt[:, c * MOBA_BLOCK:(c + 1) * MOBA_BLOCK]

    @pl.when(j == 3)
    def _():
        gqk_ref[...] = z

    @pl.when(j == 4)
    def _():
        gv_ref[...] = z

    @pl.when(j == 5)
    def _():
        gg_ref[...] = z


def _project(x, n1, w_main, w_ga, qn, kn, tm):
    n, d = x.shape
    wide = ATT_WIDTH
    row = lambda i, j: (i, 0)
    out_shape = [
        jax.ShapeDtypeStruct((n, wide), F32),
        jax.ShapeDtypeStruct((n, wide), F32),
        jax.ShapeDtypeStruct((n, wide), BF16),
        jax.ShapeDtypeStruct((n, wide), F32),
        jax.ShapeDtypeStruct((n // MOBA_BLOCK, wide, MOBA_BLOCK), BF16),
        jax.ShapeDtypeStruct((n, wide), F32),
        jax.ShapeDtypeStruct((n, wide), F32),
        jax.ShapeDtypeStruct((n, wide), F32),
        jax.ShapeDtypeStruct((n, LANES), F32),
    ]
    out_specs = [pl.BlockSpec((tm, s.shape[1]), row) if len(s.shape) == 2
                 else pl.BlockSpec((tm // MOBA_BLOCK, wide, MOBA_BLOCK), lambda i, j: (i, 0, 0)) for s in out_shape]
    return pl.pallas_call(
        _proj_kernel,
        grid=(n // tm, 6),
        in_specs=[
            pl.BlockSpec((tm, d), row),
            pl.BlockSpec((1, d), lambda i, j: (0, 0)),
            pl.BlockSpec((d, wide), lambda i, j: (0, j)),
            pl.BlockSpec((d, LANES), lambda i, j: (0, 0)),
            pl.BlockSpec((1, HEAD_DIM), lambda i, j: (0, 0)),
            pl.BlockSpec((1, HEAD_DIM), lambda i, j: (0, 0)),
        ],
        out_specs=out_specs,
        out_shape=out_shape,
        scratch_shapes=[pltpu.VMEM((tm, d), BF16)],
        compiler_params=_params("parallel", "arbitrary"),
        name="projection",
    )(x, n1, w_main, w_ga, qn, kn)


def _kmeans_kernel(k_ref, o_ref):
    rows = k_ref.shape[0]
    k = k_ref[...].reshape(rows // MOBA_BLOCK, MOBA_BLOCK, k_ref.shape[1])
    o_ref[...] = jnp.sum(k, axis=1) * (1.0 / MOBA_BLOCK)


def _block_means(k):
    n, w = k.shape
    rows = 8 * MOBA_BLOCK
    return pl.pallas_call(
        _kmeans_kernel,
        grid=(n // rows,),
        in_specs=[pl.BlockSpec((rows, w), lambda i: (i, 0))],
        out_specs=pl.BlockSpec((8, w), lambda i: (i, 0)),
        out_shape=jax.ShapeDtypeStruct((n // MOBA_BLOCK, w), F32),
        compiler_params=_params("parallel"),
        name="block_means",
    )(k)


def _prompt_select_kernel(q_ref, km_ref, o_ref, *, tiles_per_seq):
    c = pl.program_id(0)
    rows = q_ref.shape[0]
    nb = km_ref.shape[0]
    q = q_ref[...]
    km = km_ref[...]
    q_hi = q.astype(BF16)
    q_lo = (q - q_hi.astype(F32)).astype(BF16)
    km_hi = km.astype(BF16)
    km_lo = (km - km_hi.astype(F32)).astype(BF16)
    gate = _dot_nt(km_hi, q_hi) + _dot_nt(km_lo, q_hi) + _dot_nt(km_hi, q_lo)
    blk = lax.broadcasted_iota(I32, (nb, rows), 0)
    n_full = (c % tiles_per_seq) * (rows // MOBA_BLOCK) + lax.broadcasted_iota(I32, (1, rows), 1) // MOBA_BLOCK
    gate = jnp.where(blk < n_full, gate, NEG_INF)
    i0, i1, i2 = _top3(gate, blk, n_full, axis=0)
    row = lax.broadcasted_iota(I32, (8, rows), 0)
    o_ref[...] = jnp.where(row == 0, i0, jnp.where(row == 1, i1, jnp.where(row == 2, i2, -1)))


def _prompt_select(q, kmeans, seq):
    n = q.shape[0]
    rows = SELECT_ROWS
    nb = seq // MOBA_BLOCK
    tiles_per_seq = seq // rows
    return pl.pallas_call(
        functools.partial(_prompt_select_kernel, tiles_per_seq=tiles_per_seq),
        grid=(n // rows, N_ATT_HEADS),
        in_specs=[
            pl.BlockSpec((rows, HEAD_DIM), lambda c, h: (c, h)),
            pl.BlockSpec((nb, HEAD_DIM), lambda c, h: (c // tiles_per_seq, h)),
        ],
        out_specs=pl.BlockSpec((8, rows), lambda c, h: (h, c)),
        out_shape=jax.ShapeDtypeStruct((N_ATT_HEADS * 8, n), I32),
        compiler_params=_params("parallel", "parallel"),
        name="prompt_select",
    )(q, kmeans)


def _moba_prompt_kernel(pt_ref, slopes_ref, q_ref, k_ref, vt_ref, idx_ref, kc_hbm, o_ref, ps_ref,
                        sa_scr, sb_scr, bias_scr, pbuf, psem, *, group, n_steps):
    hg = pl.program_id(1)
    i = pl.program_id(2)
    bs = MOBA_BLOCK
    nb = vt_ref.shape[0]
    t = (pl.program_id(0) * pl.num_programs(1) + hg) * nb + i

    def page_copy(tt, slot, g):
        return pltpu.make_async_copy(kc_hbm.at[pt_ref[tt * group + g]], pbuf.at[slot, g], psem.at[slot])

    @pl.when(t == 0)
    def _():
        for g in range(group):
            page_copy(0, 0, g).start()

    @pl.when(t + 1 < n_steps)
    def _():
        for g in range(group):
            page_copy(t + 1, (t + 1) % 2, g).start()

    pslot = t % 2
    for g in range(group):
        page_copy(t, pslot, g).wait()
    for bk in range(group // PAGES_PER_BLOCK):
        acc = jnp.sum(pbuf[pslot, PAGES_PER_BLOCK * bk], axis=0)
        for p in range(1, PAGES_PER_BLOCK):
            acc = acc + jnp.sum(pbuf[pslot, PAGES_PER_BLOCK * bk + p], axis=0)
        ps_ref[bk] = acc

    krow = lax.broadcasted_iota(I32, (bs, bs), 0)
    qcol = lax.broadcasted_iota(I32, (bs, bs), 1)
    ones_rows = jnp.ones((SUM_ROWS, bs), BF16)
    hb = MOBA_HALF

    def head_job(hh):
        lanes = slice(hh * HEAD_DIM, (hh + 1) * HEAD_DIM)
        slope2 = slopes_ref[hg * HEADS_PER_STEP + hh] * LOG2E
        q2 = (q_ref[:, lanes] * (HEAD_DIM ** -0.5 * LOG2E)).astype(BF16)
        i0 = idx_ref[8 * hh:8 * hh + 1, :]
        i1 = idx_ref[8 * hh + 1:8 * hh + 2, :]
        i2 = idx_ref[8 * hh + 2:8 * hh + 3, :]
        bias_ref, sa_ref, sb_ref = bias_scr.at[hh], sa_scr.at[hh], sb_scr.at[hh]
        bias_ref[...] = slope2 * krow.astype(F32)

        def pv_dot(jc, p):
            return _dot(jnp.concatenate([vt_ref[jc, lanes, :], ones_rows], axis=0), p.astype(BF16))

        def selected(j):
            return (i0 == j) | (i1 == j) | (i2 == j)

        def block_offset(j):
            return slope2 * ((j - i) * bs).astype(F32)

        def sweep1(jb, s_ref):
            r0 = pl.multiple_of(jnp.minimum(jb, nb - hb) * bs, bs)
            s = _dot_nt(k_ref[pl.ds(r0, hb * bs), lanes], q2)
            cmax = jnp.full((1, bs), NEG_INF, F32)
            for u in range(hb):
                su = s[u * bs:(u + 1) * bs] + bias_ref[...]
                s_ref[u] = su
                cmax = jnp.maximum(cmax, jnp.where(selected(jb + u),
                                                   jnp.max(su, axis=0, keepdims=True) + block_offset(jb + u),
                                                   NEG_INF))
            return cmax

        def sweep2(jb, s_ref, m_prev, m_cur, l, acc):
            pv = jnp.zeros((HEAD_DIM + SUM_ROWS, bs), F32)
            for u in range(hb):
                j = jb + u
                ref = jnp.where(selected(j), m_cur - block_offset(j), -NEG_INF)
                pv = pv + pv_dot(jnp.minimum(j, nb - 1), jnp.exp2(s_ref[u] - ref))
            alpha = jnp.exp2(m_prev - m_cur)
            return alpha * l + pv[HEAD_DIM:HEAD_DIM + 1], alpha * acc + pv[:HEAD_DIM]

        def prologue():
            r_own = pl.multiple_of(i * bs, bs)
            s = jnp.where(krow <= qcol, _dot_nt(k_ref[pl.ds(r_own, bs), lanes], q2) + bias_ref[...], NEG_INF)
            m = jnp.max(s, axis=0, keepdims=True)
            pv = pv_dot(i, jnp.exp2(s - m))
            m_a = jnp.maximum(m, sweep1(0, sa_ref))
            m_b = jnp.maximum(m_a, sweep1(hb, sb_ref))
            return m, m_a, m_b, pv[HEAD_DIM:HEAD_DIM + 1], pv[:HEAD_DIM]

        def chunk(c, carry):
            m_prev, m_a, m_b, l, acc = carry
            jb = c * (2 * hb)
            l, acc = sweep2(jb, sa_ref, m_prev, m_a, l, acc)
            m_a2 = jnp.maximum(m_b, sweep1(jb + 2 * hb, sa_ref))
            l, acc = sweep2(jb + hb, sb_ref, m_a, m_b, l, acc)
            m_b2 = jnp.maximum(m_a2, sweep1(jb + 3 * hb, sb_ref))
            return m_b, m_a2, m_b2, l, acc

        def finish(carry):
            _, _, _, l, acc = carry
            o_ref[:, lanes] = (acc / l).T.astype(o_ref.dtype)

        return prologue, chunk, finish

    jobs = [head_job(hh) for hh in range(HEADS_PER_STEP)]
    carries = tuple(job[0]() for job in jobs)
    n_chunks = (i + 2 * hb - 1) // (2 * hb)
    carries = lax.fori_loop(0, n_chunks, lambda c, cs: tuple(job[1](c, cr) for job, cr in zip(jobs, cs)), carries)
    for job, cr in zip(jobs, carries):
        job[2](cr)


def _moba_prompt(page_table, slopes, q, kb, vt, idx, cache_k, batch, seq):
    n = q.shape[0]
    nb = seq // MOBA_BLOCK
    bs = MOBA_BLOCK
    hps = HEADS_PER_STEP
    n_hg = N_ATT_HEADS // hps
    n_steps = batch * n_hg * nb
    n_pages_total = page_table.size
    assert n_pages_total % (n_steps * PAGES_PER_BLOCK) == 0
    group = n_pages_total // n_steps
    bpc = group // PAGES_PER_BLOCK
    grid_spec = pltpu.PrefetchScalarGridSpec(
        num_scalar_prefetch=1,
        grid=(batch, n_hg, nb),
        in_specs=[
            pl.BlockSpec(memory_space=pltpu.SMEM),
            pl.BlockSpec((bs, hps * HEAD_DIM), lambda b, h, i, pt: (b * nb + i, h)),
            pl.BlockSpec((seq, hps * HEAD_DIM), lambda b, h, i, pt: (b, h)),
            pl.BlockSpec((nb, hps * HEAD_DIM, bs), lambda b, h, i, pt: (b, h, 0)),
            pl.BlockSpec((8 * hps, bs), lambda b, h, i, pt: (h, b * nb + i)),
            pl.BlockSpec(memory_space=pl.ANY),
        ],
        out_specs=[
            pl.BlockSpec((bs, hps * HEAD_DIM), lambda b, h, i, pt: (b * nb + i, h)),
            pl.BlockSpec((bpc, N_ATT_HEADS, HEAD_DIM), lambda b, h, i, pt: ((b * n_hg + h) * nb + i, 0, 0)),
        ],
        scratch_shapes=[
            pltpu.VMEM((hps, MOBA_HALF, bs, bs), F32),
            pltpu.VMEM((hps, MOBA_HALF, bs, bs), F32),
            pltpu.VMEM((hps, bs, bs), F32),
            pltpu.VMEM((2, group, PAGE_SIZE, N_ATT_HEADS, HEAD_DIM), F32),
            pltpu.SemaphoreType.DMA((2,)),
        ],
    )
    return pl.pallas_call(
        functools.partial(_moba_prompt_kernel, group=group, n_steps=n_steps),
        grid_spec=grid_spec,
        out_shape=[
            jax.ShapeDtypeStruct((n, ATT_WIDTH), BF16),
            jax.ShapeDtypeStruct((n_steps * bpc, N_ATT_HEADS, HEAD_DIM), F32),
        ],
        compiler_params=_params("arbitrary", "arbitrary", "arbitrary"),
        name="moba_prompt",
    )(page_table.reshape(-1), slopes, q, kb, vt, idx, cache_k)


def _sample_select_kernel(q_ref, bs_ref, kn_ref, o_ref, *, n_past_blocks, past):
    nq = q_ref.shape[0]
    col = lax.broadcasted_iota(I32, (nq, LANES), 1)
    lane = col
    n_full = (past + lax.broadcasted_iota(I32, (nq, 1), 0)) // MOBA_BLOCK
    for h in range(N_ATT_HEADS):
        sl = slice(h * HEAD_DIM, (h + 1) * HEAD_DIM)
        m_past = bs_ref[:, sl] * (1.0 / MOBA_BLOCK)
        m_new = jnp.sum(kn_ref[:, sl], axis=0, keepdims=True) * (1.0 / MOBA_BLOCK)
        row8 = lax.broadcasted_iota(I32, (8, HEAD_DIM), 0)
        new_rows = jnp.where(row8 == 0, jnp.broadcast_to(m_new, (8, HEAD_DIM)), 0.0)
        means = jnp.concatenate(
            [m_past, new_rows, jnp.zeros((LANES - n_past_blocks - 8, HEAD_DIM), F32)], axis=0)
        gate = lax.dot_general(q_ref[:, sl], means, _NT, precision=_HI, preferred_element_type=F32)
        gate = jnp.where(col < n_full, gate, NEG_INF)
        i0, i1, i2 = _top3(gate, col, n_full)
        o_ref[0, h] = jnp.where(lane == 0, i0, jnp.where(lane == 1, i1, jnp.where(lane == 2, i2, 0)))


def _sample_select(q, block_sums, k_new, db, nq, n_past_blocks):
    return pl.pallas_call(
        functools.partial(_sample_select_kernel, n_past_blocks=n_past_blocks,
                          past=n_past_blocks * MOBA_BLOCK),
        grid=(db,),
        in_specs=[
            pl.BlockSpec((nq, ATT_WIDTH), lambda s: (s, 0)),
            pl.BlockSpec((n_past_blocks, ATT_WIDTH), lambda s: (s, 0)),
            pl.BlockSpec((nq, ATT_WIDTH), lambda s: (s, 0)),
        ],
        out_specs=pl.BlockSpec((1, N_ATT_HEADS, nq, LANES), lambda s: (s, 0, 0, 0)),
        out_shape=jax.ShapeDtypeStruct((db, N_ATT_HEADS, nq, LANES), I32),
        compiler_params=_params("parallel"),
        name="sample_select",
    )(q, block_sums, k_new)


def _moba_sample_kernel(pt_ref, idx_ref, slopes_ref, q_ref, kn_ref, vn_ref, kc_hbm, vc_hbm, o_ref,
                        kbuf, vbuf, kown, vown, sem, *, n_steps, n_pages, nq, past):
    t = pl.program_id(0)
    nh = N_ATT_HEADS
    n_sel = nq * MOBA_TOPK
    n_slots = n_sel * PAGES_PER_BLOCK

    def copies(tt, slot, g):
        s = tt // nh
        h = tt % nh
        blk = idx_ref[tt * n_sel + g // PAGES_PER_BLOCK]
        page = pt_ref[s * n_pages + blk * PAGES_PER_BLOCK + g % PAGES_PER_BLOCK]
        return (pltpu.make_async_copy(kc_hbm.at[page, :, h, :], kbuf.at[slot, g], sem.at[0, slot]),
                pltpu.make_async_copy(vc_hbm.at[page, :, h, :], vbuf.at[slot, g], sem.at[1, slot]))

    def start(tt, slot):
        for g in range(n_slots):
            ck, cv = copies(tt, slot, g)
            ck.start()
            cv.start()

    @pl.when(t == 0)
    def _():
        start(0, 0)
        kown[...] = jnp.zeros_like(kown)
        vown[...] = jnp.zeros_like(vown)

    @pl.when(t + 1 < n_steps)
    def _():
        start(t + 1, (t + 1) % 2)

    slot = t % 2
    h = t % nh
    slope = slopes_ref[h]
    scale = HEAD_DIM ** -0.5
    qb = q_ref[...].astype(BF16)
    n_keys = n_slots * PAGE_SIZE
    keys_per_q = MOBA_TOPK * MOBA_BLOCK

    kown[0:nq, :] = kn_ref[...]
    vown[0:nq, :] = vn_ref[...]
    row = lax.broadcasted_iota(I32, (nq, LANES), 0)
    colo = lax.broadcasted_iota(I32, (nq, LANES), 1)
    s_own = _dot_nt(qb, kown[...].astype(BF16)) * scale - slope * (row - colo).astype(F32)
    s_own = jnp.where(colo <= row, s_own, NEG_INF)

    colk = lax.broadcasted_iota(I32, (1, n_keys), 1)
    grp = colk // MOBA_BLOCK
    blk_of_col = jnp.zeros((1, n_keys), I32)
    for g in range(n_sel):
        blk_of_col = jnp.where(grp == g, idx_ref[t * n_sel + g], blk_of_col)
    pos = blk_of_col * MOBA_BLOCK + colk % MOBA_BLOCK
    rowk = lax.broadcasted_iota(I32, (nq, n_keys), 0)
    mine = (lax.broadcasted_iota(I32, (nq, n_keys), 1) // keys_per_q) == rowk
    n_full = (past + rowk) // MOBA_BLOCK
    pick = (lax.broadcasted_iota(I32, (nq, n_keys), 1) // MOBA_BLOCK) % MOBA_TOPK
    valid = mine & (pick < n_full)

    for g in range(n_slots):
        ck, cv = copies(t, slot, g)
        ck.wait()
        cv.wait()

    kall = kbuf[slot].reshape(n_keys, HEAD_DIM).astype(BF16)
    s_sel = _dot_nt(qb, kall) * scale - slope * ((past + rowk) - pos).astype(F32)
    s_sel = jnp.where(valid, s_sel, NEG_INF)
    m = jnp.maximum(jnp.max(s_sel, axis=-1, keepdims=True), jnp.max(s_own, axis=-1, keepdims=True))
    p_sel = jnp.exp(s_sel - m)
    p_own = jnp.exp(s_own - m)
    l = jnp.sum(p_sel, axis=-1, keepdims=True) + jnp.sum(p_own, axis=-1, keepdims=True)
    vall = vbuf[slot].reshape(n_keys, HEAD_DIM).astype(BF16)
    o = _dot(p_sel.astype(BF16), vall) + _dot(p_own.astype(BF16), vown[...].astype(BF16))
    o_ref[...] = o / l


def _moba_sample(page_table, idx, slopes, q, k_new, v_new, cache_k, cache_v, db, nq):
    n_pages = page_table.shape[1]
    n_steps = db * N_ATT_HEADS
    n_slots = nq * MOBA_TOPK * PAGES_PER_BLOCK
    qspec = pl.BlockSpec((nq, HEAD_DIM), lambda t, pt, ix: (t // N_ATT_HEADS, t % N_ATT_HEADS))
    grid_spec = pltpu.PrefetchScalarGridSpec(
        num_scalar_prefetch=2,
        grid=(n_steps,),
        in_specs=[
            pl.BlockSpec(memory_space=pltpu.SMEM),
            qspec, qspec, qspec,
            pl.BlockSpec(memory_space=pl.ANY),
            pl.BlockSpec(memory_space=pl.ANY),
        ],
        out_specs=qspec,
        scratch_shapes=[
            pltpu.VMEM((2, n_slots, PAGE_SIZE, HEAD_DIM), F32),
            pltpu.VMEM((2, n_slots, PAGE_SIZE, HEAD_DIM), F32),
            pltpu.VMEM((LANES, HEAD_DIM), F32),
            pltpu.VMEM((LANES, HEAD_DIM), F32),
            pltpu.SemaphoreType.DMA((2, 2)),
        ],
    )
    return pl.pallas_call(
        functools.partial(_moba_sample_kernel, n_steps=n_steps, n_pages=n_pages, nq=nq,
                          past=n_pages * PAGE_SIZE),
        grid_spec=grid_spec,
        out_shape=jax.ShapeDtypeStruct((db * nq, ATT_WIDTH), F32),
        compiler_params=_params("arbitrary"),
        name="moba_sample",
    )(page_table.reshape(-1), idx.reshape(-1), slopes, q, k_new, v_new, cache_k, cache_v)


def _gla_kernel(gq_ref, gk_ref, gv_ref, gg_ref, ga_ref, wah_ref, wal_ref, ba_ref, nw_ref, s0_ref,
                o_ref, sfin_ref, s_scr, qd_scr, oi_scr, ds_scr, dec_scr, *, chunk, n_sub, t_valid):
    t = pl.program_id(2)

    @pl.when(t == 0)
    def _():
        s_scr[...] = s0_ref[0, 0]

    c = chunk
    tri = lax.broadcasted_iota(I32, (c, c), 0) >= lax.broadcasted_iota(I32, (c, c), 1)
    tri_b = jnp.where(tri, 1.0, 0.0).astype(BF16)
    ones_b = jnp.ones((c, GLA_DK), BF16)

    def split3(x):
        hi = x.astype(BF16)
        r = x - hi.astype(F32)
        mid = r.astype(BF16)
        return hi, mid, (r - mid.astype(F32)).astype(BF16)

    ga = ga_ref[...]
    ga_hi = ga.astype(BF16)
    ga_lo = (ga - ga_hi.astype(F32)).astype(BF16)
    pre = _dot(ga_hi, wah_ref[...]) + _dot(ga_hi, wal_ref[...]) + _dot(ga_lo, wah_ref[...]) + ba_ref[...]
    la = (jnp.minimum(pre, 0.0) - jnp.log1p(jnp.exp(-jnp.abs(pre)))) * (1.0 / GLA_TAU)
    if t_valid < c * n_sub:
        la = jnp.where(lax.broadcasted_iota(I32, la.shape, 0) < t_valid, la, 0.0)

    for ci in range(n_sub):
        sl = slice(ci * c, (ci + 1) * c)
        parts = split3(la[sl])
        b = _dot(tri_b, parts[0]) + _dot(tri_b, parts[1]) + _dot(tri_b, parts[2])
        b_end = b[c - 1:c, :]
        b_end_col = _dot_tn(parts[0], ones_b) + _dot_tn(parts[1], ones_b) + _dot_tn(parts[2], ones_b)
        q = gq_ref[sl, :] * (GLA_DK ** -0.5)
        k = gk_ref[sl, :]
        v = gv_ref[sl, :].astype(BF16)
        q_dec = (q * jnp.exp(b)).astype(BF16)
        k_inv = (k * jnp.exp(-b)).astype(BF16)
        k_end = (k * jnp.exp(b_end - b)).astype(BF16)
        a = jnp.where(tri, _dot_nt(q_dec, k_inv), 0.0)
        qd_scr[sl, :] = q_dec
        oi_scr[sl, :] = _dot(a.astype(BF16), v)
        ds_scr[ci] = _dot_tn(k_end, v)
        dec_scr[ci] = jnp.exp(b_end_col)

    state = s_scr[...]
    nw = nw_ref[...]
    for ci in range(n_sub):
        sl = slice(ci * c, (ci + 1) * c)
        o = oi_scr[sl, :] + _dot(qd_scr[sl, :], state.astype(BF16))
        state = jnp.concatenate([dec_scr[ci]] * (GLA_DV // GLA_DK), axis=1) * state + ds_scr[ci]
        on = o * lax.rsqrt(jnp.mean(o * o, axis=-1, keepdims=True) + RMS_EPS) * nw
        g = gg_ref[sl, :]
        o_ref[sl, :] = (on * (g * jax.nn.sigmoid(g))).astype(o_ref.dtype)
    s_scr[...] = state

    @pl.when(t == pl.num_programs(2) - 1)
    def _():
        sfin_ref[0, 0] = state


def _gla(gqk, gv, gg, ga, wa_hi, wa_lo, ba, nw, s0, batch, seq, t_valid):
    n = gqk.shape[0]
    c = GLA_CHUNK
    n_sub = min(8, seq // c)
    rows = c * n_sub
    steps = seq // rows
    nh = N_GLA_HEADS
    return pl.pallas_call(
        functools.partial(_gla_kernel, chunk=c, n_sub=n_sub, t_valid=t_valid),
        grid=(batch, nh, steps),
        in_specs=[
            pl.BlockSpec((rows, GLA_DK), lambda b, h, t: (b * steps + t, h)),
            pl.BlockSpec((rows, GLA_DK), lambda b, h, t: (b * steps + t, nh + h)),
            pl.BlockSpec((rows, GLA_DV), lambda b, h, t: (b * steps + t, h)),
            pl.BlockSpec((rows, GLA_DV), lambda b, h, t: (b * steps + t, h)),
            pl.BlockSpec((rows, LANES), lambda b, h, t: (b * steps + t, 0)),
            pl.BlockSpec((LANES, GLA_DK), lambda b, h, t: (0, h)),
            pl.BlockSpec((LANES, GLA_DK), lambda b, h, t: (0, h)),
            pl.BlockSpec((1, GLA_DK), lambda b, h, t: (0, h)),
            pl.BlockSpec((1, GLA_DV), lambda b, h, t: (0, 0)),
            pl.BlockSpec((1, 1, GLA_DK, GLA_DV), lambda b, h, t: (b, h, 0, 0)),
        ],
        out_specs=[
            pl.BlockSpec((rows, GLA_DV), lambda b, h, t: (b * steps + t, h)),
            pl.BlockSpec((1, 1, GLA_DK, GLA_DV), lambda b, h, t: (b, h, 0, 0)),
        ],
        out_shape=[
            jax.ShapeDtypeStruct((n, GLA_VW), BF16),
            jax.ShapeDtypeStruct(s0.shape, F32),
        ],
        scratch_shapes=[
            pltpu.VMEM((GLA_DK, GLA_DV), F32),
            pltpu.VMEM((rows, GLA_DK), BF16),
            pltpu.VMEM((rows, GLA_DV), F32),
            pltpu.VMEM((n_sub, GLA_DK, GLA_DV), F32),
            pltpu.VMEM((n_sub, GLA_DK, GLA_DK), F32),
        ],
        compiler_params=_params("parallel", "parallel", "arbitrary"),
        name="gla",
    )(gqk, gqk, gv, gg, ga, wa_hi, wa_lo, ba, nw, s0)


def _out_router_kernel(att_ref, gla_ref, x_ref, woa_ref, wog_ref, n2_ref, wrh_ref, wrl_ref, br_ref,
                       h_ref, hn_ref, re_ref, rg_ref):
    h = x_ref[...] + _dot(att_ref[...], woa_ref[...]) + _dot(gla_ref[...], wog_ref[...])
    h_ref[...] = h
    hn = h * lax.rsqrt(jnp.mean(h * h, axis=-1, keepdims=True) + RMS_EPS) * n2_ref[...]
    hn_ref[...] = hn
    hn_hi = hn.astype(BF16)
    hn_lo = (hn - hn_hi.astype(F32)).astype(BF16)
    logits = (_dot(hn_hi, wrh_ref[...]) + _dot(hn_hi, wrl_ref[...]) + _dot(hn_lo, wrh_ref[...])) + br_ref[...]
    col = lax.broadcasted_iota(I32, logits.shape, 1)
    big = jnp.int32(2 ** 30)
    in_g = col < N_GROUPS
    lg = jnp.where(in_g, logits, -jnp.inf)
    m1 = jnp.max(lg, axis=-1, keepdims=True)
    g_idx = jnp.min(jnp.where(lg == m1, col, big), axis=-1, keepdims=True)
    g_top = 1.0 / jnp.sum(jnp.exp(lg - m1), axis=-1, keepdims=True)
    ecol = col - N_GROUPS
    in_e = (ecol >= g_idx * EXPERTS_PER_GROUP) & (ecol < (g_idx + 1) * EXPERTS_PER_GROUP)
    le = jnp.where(in_e, logits, -jnp.inf)
    l1 = jnp.max(le, axis=-1, keepdims=True)
    e1 = jnp.min(jnp.where(le == l1, ecol, big), axis=-1, keepdims=True)
    le2 = jnp.where(ecol == e1, -jnp.inf, le)
    l2 = jnp.max(le2, axis=-1, keepdims=True)
    e2 = jnp.min(jnp.where(le2 == l2, ecol, big), axis=-1, keepdims=True)
    w2 = jnp.exp(l2 - l1)
    den = 1.0 + w2
    re_ref[...] = jnp.where(col == 0, e1, jnp.where(col == 1, e2, 0))
    rg_ref[...] = jnp.where(col == 0, g_top / den, jnp.where(col == 1, g_top * w2 / den, 0.0))


def _out_router_into_kernel(*refs):
    _out_router_kernel(*refs[:9], *refs[10:])


def _out_router_alloc_kernel(*refs, n_real):
    i = pl.program_id(0)

    @pl.when(i < n_real)
    def _():
        _out_router_kernel(*refs)

    @pl.when(i >= n_real)
    def _():
        refs[10][...] = jnp.zeros_like(refs[10])


def _out_router(att, gla, x, wo_att, wo_gla, n2, wr_hi, wr_lo, br, tm, n_all, row0, hn_all=None):
    n, d = x.shape
    n_real = n // tm
    row = lambda i: (jnp.minimum(i, n_real - 1), 0)
    fix = lambda i: (0, 0)
    b0 = row0 // tm
    if hn_all is None:
        assert row0 == 0 and n_all % tm == 0
        body, steps = functools.partial(_out_router_alloc_kernel, n_real=n_real), n_all // tm
        extra_in, extra_args, alias = [], [], {}
    else:
        body, steps = _out_router_into_kernel, n_real
        extra_in, extra_args, alias = [pl.BlockSpec(memory_space=pl.ANY)], [hn_all], {9: 1}
    return pl.pallas_call(
        body,
        grid=(steps,),
        input_output_aliases=alias,
        in_specs=[
            pl.BlockSpec((tm, ATT_WIDTH), row),
            pl.BlockSpec((tm, GLA_VW), row),
            pl.BlockSpec((tm, d), row),
            pl.BlockSpec((ATT_WIDTH, d), fix),
            pl.BlockSpec((GLA_VW, d), fix),
            pl.BlockSpec((1, d), fix),
            pl.BlockSpec((d, LANES), fix),
            pl.BlockSpec((d, LANES), fix),
            pl.BlockSpec((1, LANES), fix),
        ] + extra_in,
        out_specs=[
            pl.BlockSpec((tm, d), row),
            pl.BlockSpec((tm, d), lambda i: (b0 + i, 0)),
            pl.BlockSpec((tm, LANES), row),
            pl.BlockSpec((tm, LANES), row),
        ],
        out_shape=[
            jax.ShapeDtypeStruct((n, d), F32),
            jax.ShapeDtypeStruct((n_all, d), F32),
            jax.ShapeDtypeStruct((n, LANES), I32),
            jax.ShapeDtypeStruct((n, LANES), F32),
        ],
        compiler_params=_params("arbitrary"),
        name="out_router",
    )(att, gla, x, wo_att, wo_gla, n2, wr_hi, wr_lo, br, *extra_args)


def _expert_kernel(be_ref, tok_ref, nused_ref, hn_hbm, wg_ref, wu_ref, wd_ref, o_ref,
                   xbuf, wgb, wub, wdb, sem, *, rows, n_blocks):
    i = pl.program_id(0)

    @pl.when((i == 0) | (be_ref[i] != be_ref[jnp.maximum(i - 1, 0)]))
    def _():
        wgb[...] = wg_ref[0].astype(BF16)
        wub[...] = wu_ref[0].astype(BF16)
        wdb[...] = wd_ref[0].astype(BF16)

    n_used = nused_ref[0]

    def issue(blk, slot, r, priority):
        tok = tok_ref[blk * rows + r]
        pltpu.make_async_copy(hn_hbm.at[pl.ds(tok, 1), :], xbuf.at[slot, pl.ds(r, 1), :],
                              sem.at[slot]).start(priority=priority)

    def wait_rows(slot):
        pltpu.make_async_copy(hn_hbm.at[pl.ds(0, rows), :], xbuf.at[slot], sem.at[slot]).wait()

    @pl.when(i == 0)
    def _():
        def body(g, carry):
            for u in range(DMA_UNROLL):
                issue(0, 0, g * DMA_UNROLL + u, u % 2)
            return carry
        lax.fori_loop(0, rows // DMA_UNROLL, body, 0)

    slot = i % 2

    @pl.when(i < n_used)
    def _():
        wait_rows(slot)
        x = xbuf[slot].astype(BF16)
        for r in range(rows):
            issue(i + 1, 1 - slot, r, r % 2)
        a = _dot(x, wgb[...])
        u = _dot(x, wub[...])
        act = (a * jax.nn.sigmoid(a) * u).astype(BF16)
        o_ref[...] = _dot(act, wdb[...])

    @pl.when(i == n_used)
    def _():
        wait_rows(slot)

    @pl.when(i >= n_used)
    def _():
        o_ref[...] = jnp.zeros_like(o_ref)


def _experts(block_expert, slot_tok, n_used, hn, wg, wu, wd):
    n_blocks = block_expert.shape[0]
    rows = EXPERT_ROWS
    d = hn.shape[1]
    f = wg.shape[2]
    grid_spec = pltpu.PrefetchScalarGridSpec(
        num_scalar_prefetch=3,
        grid=(n_blocks,),
        in_specs=[
            pl.BlockSpec(memory_space=pl.ANY),
            pl.BlockSpec((1, d, f), lambda i, be, tk, nu: (be[i], 0, 0)),
            pl.BlockSpec((1, d, f), lambda i, be, tk, nu: (be[i], 0, 0)),
            pl.BlockSpec((1, f, d), lambda i, be, tk, nu: (be[i], 0, 0)),
        ],
        out_specs=pl.BlockSpec((rows, d), lambda i, be, tk, nu: (i, 0)),
        scratch_shapes=[
            pltpu.VMEM((2, rows, d), F32),
            pltpu.VMEM((d, f), BF16),
            pltpu.VMEM((d, f), BF16),
            pltpu.VMEM((f, d), BF16),
            pltpu.SemaphoreType.DMA((2,)),
        ],
    )
    return pl.pallas_call(
        functools.partial(_expert_kernel, rows=rows, n_blocks=n_blocks),
        grid_spec=grid_spec,
        out_shape=jax.ShapeDtypeStruct((n_blocks * rows, d), F32),
        compiler_params=_params("arbitrary"),
        name="experts",
    )(block_expert, slot_tok, n_used, hn, wg, wu, wd)


def _combine_kernel(pos_ref, h_ref, rg_ref, ys_hbm, y_ref, buf, sem, *, rows, n_steps, tok0):
    i = pl.program_id(0)

    def start(step, slot):
        def body(g, carry):
            for u in range(DMA_UNROLL):
                r = g * DMA_UNROLL + u
                for k in range(EXPERT_TOPK):
                    p = pos_ref[(tok0 + step * rows + r) * EXPERT_TOPK + k]
                    pltpu.make_async_copy(ys_hbm.at[pl.ds(p, 1), :], buf.at[slot, k, pl.ds(r, 1), :],
                                          sem.at[slot]).start(priority=k % 2)
            return carry
        lax.fori_loop(0, rows // DMA_UNROLL, body, 0)

    @pl.when(i == 0)
    def _():
        start(0, 0)

    @pl.when(i + 1 < n_steps)
    def _():
        start(i + 1, (i + 1) % 2)

    slot = i % 2

    for k in range(EXPERT_TOPK):
        pltpu.make_async_copy(ys_hbm.at[pl.ds(0, rows), :], buf.at[slot, k], sem.at[slot]).wait()

    rg = rg_ref[...]
    y = h_ref[...]
    for k in range(EXPERT_TOPK):
        y = y + buf[slot, k] * rg[:, k:k + 1]
    y_ref[...] = y


def _combine(pos, h, rg, ys, tok0, rows):
    n_tok, d = h.shape
    n_steps = n_tok // rows
    grid_spec = pltpu.PrefetchScalarGridSpec(
        num_scalar_prefetch=1,
        grid=(n_steps,),
        in_specs=[
            pl.BlockSpec((rows, d), lambda i, p: (i, 0)),
            pl.BlockSpec((rows, LANES), lambda i, p: (i, 0)),
            pl.BlockSpec(memory_space=pl.ANY),
        ],
        out_specs=pl.BlockSpec((rows, d), lambda i, p: (i, 0)),
        scratch_shapes=[
            pltpu.VMEM((2, EXPERT_TOPK, rows, d), F32),
            pltpu.SemaphoreType.DMA((2,)),
        ],
    )
    return pl.pallas_call(
        functools.partial(_combine_kernel, rows=rows, n_steps=n_steps, tok0=tok0),
        grid_spec=grid_spec,
        out_shape=jax.ShapeDtypeStruct((n_tok, d), F32),
        compiler_params=_params("arbitrary"),
        name="combine",
    )(pos, h, rg, ys)


def _dispatch(expert):
    a = expert.shape[0]
    rows = EXPERT_ROWS
    onehot = (expert[:, None] == jnp.arange(N_EXPERTS, dtype=I32)[None, :]).astype(I32)
    csum = jnp.cumsum(onehot, axis=0)
    counts = csum[-1]
    rank = jnp.take_along_axis(csum, expert[:, None], axis=1)[:, 0] - 1
    padded = (counts + rows - 1) // rows * rows
    ends_p = jnp.cumsum(padded)
    pstart = ends_p - padded
    pos = (pstart[expert] + rank).astype(I32)
    n_blocks = -(-a // rows) + N_EXPERTS + 1
    slot_tok = jnp.zeros((n_blocks * rows,), I32).at[pos].set(jnp.arange(a, dtype=I32) // EXPERT_TOPK)
    block_start = jnp.arange(n_blocks, dtype=I32) * rows
    block_expert = jnp.minimum(
        jnp.sum((ends_p[None, :] <= block_start[:, None]).astype(I32), axis=1), N_EXPERTS - 1).astype(I32)
    n_used = (ends_p[-1:] // rows).astype(I32)
    return pos, slot_tok, block_expert, n_used


def kernel(x_prompt, x_sample, cache_k, cache_v, state_gla, page_table, norm1_w, w_in, q_norm_w, k_norm_w,
           w_gla_a2, b_gla_a, gla_norm_w, w_out, norm2_w, w_r1, b_r1, w_r2, b_r2, w_e_gate, w_e_up, w_e_down):
    depth = w_in.shape[0]
    assert depth == 1
    batch, seq, d = x_prompt.shape
    db, nq, _ = x_sample.shape
    n_pages = page_table.shape[1]
    past = n_pages * PAGE_SIZE
    assert past % MOBA_BLOCK == 0 and nq <= GLA_CHUNK and seq % (8 * MOBA_BLOCK) == 0
    n_past_blocks = past // MOBA_BLOCK
    assert n_past_blocks + 8 <= LANES
    n_p, n_s = batch * seq, db * nq
    l = 0

    w = w_in[l]
    o_gq = 3 * ATT_WIDTH
    o_gv = o_gq + 2 * GLA_KW
    o_ga = o_gv + GLA_VW
    o_gg = o_ga + GLA_GATE_RANK
    w_main = jnp.concatenate([w[:, :o_ga], w[:, o_gg:]], axis=1).astype(BF16)
    w_ga = jnp.pad(w[:, o_ga:o_gg], ((0, 0), (0, LANES - GLA_GATE_RANK))).astype(BF16)
    n1 = norm1_w[l][None, :]
    qn = q_norm_w[l][None, :]
    kn = k_norm_w[l][None, :]
    wa = jnp.pad(w_gla_a2[l], ((0, LANES - GLA_GATE_RANK), (0, 0)))
    wa_hi = wa.astype(BF16)
    wa_lo = (wa - wa_hi.astype(F32)).astype(BF16)
    ba = b_gla_a[l][None, :]
    gnw = gla_norm_w[l][None, :]
    wo = w_out[l].astype(BF16)
    wo_att, wo_gla = wo[:ATT_WIDTH], wo[ATT_WIDTH:]
    n2 = norm2_w[l][None, :]
    wr = jnp.pad(jnp.concatenate([w_r1[l], w_r2[l]], axis=1), ((0, 0), (0, LANES - N_GROUPS - N_EXPERTS)))
    br = jnp.pad(jnp.concatenate([b_r1[l], b_r2[l]]), (0, LANES - N_GROUPS - N_EXPERTS))[None, :]
    slopes = 2.0 ** (-8.0 * jnp.arange(1, N_ATT_HEADS + 1, dtype=F32) / N_ATT_HEADS)

    xp = x_prompt.reshape(n_p, d)
    q_p, k_p, kb_p, v_p, vt_p, gqk_p, gv_p, gg_p, ga_p = _project(xp, n1, w_main, w_ga, qn, kn, 512)
    kmeans = _block_means(k_p)
    idx_p = _prompt_select(q_p, kmeans, seq)
    att_p, block_sums = _moba_prompt(page_table, slopes, q_p, kb_p, vt_p, idx_p, cache_k[l], batch, seq)
    s0_p = jnp.zeros((batch, N_GLA_HEADS, GLA_DK, GLA_DV), F32)
    gla_p, s_p = _gla(gqk_p, gv_p, gg_p, ga_p, wa_hi, wa_lo, ba, gnw, s0_p, batch, seq, seq)

    xs = x_sample.reshape(n_s, d)
    q_s, k_s, _, v_s, _, gqk_s, gv_s, gg_s, ga_s = _project(xs, n1, w_main, w_ga, qn, kn, n_s)
    idx = _sample_select(q_s, block_sums.reshape(db * n_past_blocks, ATT_WIDTH), k_s, db, nq, n_past_blocks)
    att_s = _moba_sample(page_table, idx[..., :MOBA_TOPK], slopes, q_s, k_s, v_s, cache_k[l], cache_v[l], db, nq)

    def pad_seq(a):
        return jnp.pad(a.reshape(db, nq, -1), ((0, 0), (0, GLA_CHUNK - nq), (0, 0))).reshape(db * GLA_CHUNK, -1)

    gla_s, s_s = _gla(pad_seq(gqk_s), pad_seq(gv_s), pad_seq(gg_s), pad_seq(ga_s), wa_hi, wa_lo, ba, gnw,
                      state_gla[l], db, GLA_CHUNK, nq)
    gla_s = gla_s.reshape(db, GLA_CHUNK, GLA_VW)[:, :nq].reshape(n_s, GLA_VW)

    wr_hi = wr.astype(BF16)
    wr_lo = (wr - wr_hi.astype(F32)).astype(BF16)
    n_all = n_p + n_s
    h_p, hn, re_p, rg_p = _out_router(att_p, gla_p, xp, wo_att, wo_gla, n2, wr_hi, wr_lo, br, 256, n_all, 0)
    h_s, hn, re_s, rg_s = _out_router(att_s.astype(BF16), gla_s, xs, wo_att, wo_gla, n2, wr_hi, wr_lo, br,
                                      n_s, n_all, n_p, hn)
    expert = jnp.concatenate([re_p[:, :EXPERT_TOPK], re_s[:, :EXPERT_TOPK]], axis=0).reshape(-1)
    pos, slot_tok, block_expert, n_used = _dispatch(expert)
    ys = _experts(block_expert, slot_tok, n_used, hn, w_e_gate[l], w_e_up[l], w_e_down[l])
    y_p = _combine(pos, h_p, rg_p, ys, 0, 128)
    y_s = _combine(pos, h_s, rg_s, ys, n_p, 128)

    hd = (N_ATT_HEADS, HEAD_DIM)
    return (y_p.reshape(batch, seq, d), y_s.reshape(db, nq, d),
            k_p.reshape(1, batch, seq, *hd), v_p.reshape(1, batch, seq, *hd), s_p[None],
            k_s.reshape(1, db, nq, *hd), v_s.reshape(1, db, nq, *hd), s_s[None])
```

```python
import functools

import jax
import jax.numpy as jnp
from jax import lax
from jax.experimental import pallas as pl
from jax.experimental.pallas import tpu as pltpu

F32 = jnp.float32
BF16 = jnp.bfloat16
I32 = jnp.int32

HEAD_DIM = 128
N_ATT_HEADS = 8
ATT_WIDTH = N_ATT_HEADS * HEAD_DIM
MOBA_BLOCK = 256
MOBA_TOPK = 3
MOBA_HALF = 2
HEADS_PER_STEP = 2
SELECT_ROWS = 2048
SUM_ROWS = 16
LOG2E = 1.4426950408889634
PAGE_SIZE = 128
PAGES_PER_BLOCK = MOBA_BLOCK // PAGE_SIZE
N_GLA_HEADS = 4
GLA_DK = 128
GLA_DV = 256
GLA_KW = N_GLA_HEADS * GLA_DK
GLA_VW = N_GLA_HEADS * GLA_DV
GLA_GATE_RANK = 16
GLA_TAU = 16.0
GLA_CHUNK = 64
N_GROUPS = 4
EXPERTS_PER_GROUP = 8
N_EXPERTS = N_GROUPS * EXPERTS_PER_GROUP
EXPERT_TOPK = 2
RMS_EPS = 1e-6
NEG_INF = -1e30
LANES = 128
EXPERT_ROWS = 512
OUT_ROWS = 512
DMA_UNROLL = 8
VMEM_LIMIT = 56 * 1024 * 1024

_NT = (((1,), (1,)), ((), ()))
_TN = (((0,), (0,)), ((), ()))
_HI = lax.Precision.HIGHEST


def _dot(a, b):
    return jnp.dot(a, b, preferred_element_type=F32)


def _dot_nt(a, b):
    return lax.dot_general(a, b, _NT, preferred_element_type=F32)


def _dot_tn(a, b):
    return lax.dot_general(a, b, _TN, preferred_element_type=F32)


def _params(*sem):
    return pltpu.CompilerParams(dimension_semantics=sem, vmem_limit_bytes=VMEM_LIMIT)


def _top3(gate, col, n_valid, axis=-1):
    picks = []
    g = gate
    for r in range(MOBA_TOPK):
        m = jnp.max(g, axis=axis, keepdims=True)
        idx = jnp.min(jnp.where(g == m, col, jnp.int32(2 ** 30)), axis=axis, keepdims=True)
        g = jnp.where(col == idx, -jnp.inf, g)
        picks.append(jnp.where(r < n_valid, idx, -1))
    return picks


def _proj_kernel(x_ref, n1_ref, w_ref, wga_ref, qn_ref, kn_ref,
                 q_ref, k_ref, kb_ref, v_ref, vt_ref, gqk_ref, gv_ref, gg_ref, ga_ref, xn_ref):
    j = pl.program_id(1)

    @pl.when(j == 0)
    def _():
        x = x_ref[...]
        y = x * lax.rsqrt(jnp.mean(x * x, axis=-1, keepdims=True) + RMS_EPS) * n1_ref[...]
        xn_ref[...] = y.astype(BF16)
        ga_ref[...] = _dot(xn_ref[...], wga_ref[...])

    z = _dot(xn_ref[...], w_ref[...])

    def head_norm(w):
        outs = []
        for h in range(N_ATT_HEADS):
            zh = z[:, h * HEAD_DIM:(h + 1) * HEAD_DIM]
            outs.append(zh * lax.rsqrt(jnp.mean(zh * zh, axis=-1, keepdims=True) + RMS_EPS) * w)
        return outs

    @pl.when(j == 0)
    def _():
        for h, y in enumerate(head_norm(qn_ref[...])):
            q_ref[:, h * HEAD_DIM:(h + 1) * HEAD_DIM] = y

    @pl.when(j == 1)
    def _():
        for h, y in enumerate(head_norm(kn_ref[...])):
            k_ref[:, h * HEAD_DIM:(h + 1) * HEAD_DIM] = y
            kb_ref[:, h * HEAD_DIM:(h + 1) * HEAD_DIM] = y.astype(BF16)

    @pl.when(j == 2)
    def _():
        v_ref[...] = z
        zt = z.T.astype(BF16)
        for c in range(vt_ref.shape[0]):
            vt_ref[c] = zt[:, c * MOBA_BLOCK:(c + 1) * MOBA_BLOCK]

    @pl.when(j == 3)
    def _():
        gqk_ref[...] = z

    @pl.when(j == 4)
    def _():
        gv_ref[...] = z

    @pl.when(j == 5)
    def _():
        gg_ref[...] = z


def _project(x, n1, w_main, w_ga, qn, kn, tm):
    n, d = x.shape
    wide = ATT_WIDTH
    row = lambda i, j: (i, 0)
    out_shape = [
        jax.ShapeDtypeStruct((n, wide), F32),
        jax.ShapeDtypeStruct((n, wide), F32),
        jax.ShapeDtypeStruct((n, wide), BF16),
        jax.ShapeDtypeStruct((n, wide), F32),
        jax.ShapeDtypeStruct((n // MOBA_BLOCK, wide, MOBA_BLOCK), BF16),
        jax.ShapeDtypeStruct((n, wide), F32),
        jax.ShapeDtypeStruct((n, wide), F32),
        jax.ShapeDtypeStruct((n, wide), F32),
        jax.ShapeDtypeStruct((n, LANES), F32),
    ]
    out_specs = [pl.BlockSpec((tm, s.shape[1]), row) if len(s.shape) == 2
                 else pl.BlockSpec((tm // MOBA_BLOCK, wide, MOBA_BLOCK), lambda i, j: (i, 0, 0)) for s in out_shape]
    return pl.pallas_call(
        _proj_kernel,
        grid=(n // tm, 6),
        in_specs=[
            pl.BlockSpec((tm, d), row),
            pl.BlockSpec((1, d), lambda i, j: (0, 0)),
            pl.BlockSpec((d, wide), lambda i, j: (0, j)),
            pl.BlockSpec((d, LANES), lambda i, j: (0, 0)),
            pl.BlockSpec((1, HEAD_DIM), lambda i, j: (0, 0)),
            pl.BlockSpec((1, HEAD_DIM), lambda i, j: (0, 0)),
        ],
        out_specs=out_specs,
        out_shape=out_shape,
        scratch_shapes=[pltpu.VMEM((tm, d), BF16)],
        compiler_params=_params("parallel", "arbitrary"),
        name="projection",
    )(x, n1, w_main, w_ga, qn, kn)


def _kmeans_kernel(k_ref, o_ref):
    rows = k_ref.shape[0]
    k = k_ref[...].reshape(rows // MOBA_BLOCK, MOBA_BLOCK, k_ref.shape[1])
    o_ref[...] = jnp.sum(k, axis=1) * (1.0 / MOBA_BLOCK)


def _block_means(k):
    n, w = k.shape
    rows = 8 * MOBA_BLOCK
    return pl.pallas_call(
        _kmeans_kernel,
        grid=(n // rows,),
        in_specs=[pl.BlockSpec((rows, w), lambda i: (i, 0))],
        out_specs=pl.BlockSpec((8, w), lambda i: (i, 0)),
        out_shape=jax.ShapeDtypeStruct((n // MOBA_BLOCK, w), F32),
        compiler_params=_params("parallel"),
        name="block_means",
    )(k)


def _prompt_select_kernel(q_ref, km_ref, o_ref, *, tiles_per_seq):
    c = pl.program_id(0)
    rows = q_ref.shape[0]
    nb = km_ref.shape[0]
    q = q_ref[...]
    km = km_ref[...]
    q_hi = q.astype(BF16)
    q_lo = (q - q_hi.astype(F32)).astype(BF16)
    km_hi = km.astype(BF16)
    km_lo = (km - km_hi.astype(F32)).astype(BF16)
    gate = _dot_nt(km_hi, q_hi) + _dot_nt(km_lo, q_hi) + _dot_nt(km_hi, q_lo)
    blk = lax.broadcasted_iota(I32, (nb, rows), 0)
    n_full = (c % tiles_per_seq) * (rows // MOBA_BLOCK) + lax.broadcasted_iota(I32, (1, rows), 1) // MOBA_BLOCK
    gate = jnp.where(blk < n_full, gate, NEG_INF)
    i0, i1, i2 = _top3(gate, blk, n_full, axis=0)
    row = lax.broadcasted_iota(I32, (8, rows), 0)
    o_ref[...] = jnp.where(row == 0, i0, jnp.where(row == 1, i1, jnp.where(row == 2, i2, -1)))


def _prompt_select(q, kmeans, seq):
    n = q.shape[0]
    rows = SELECT_ROWS
    nb = seq // MOBA_BLOCK
    tiles_per_seq = seq // rows
    return pl.pallas_call(
        functools.partial(_prompt_select_kernel, tiles_per_seq=tiles_per_seq),
        grid=(n // rows, N_ATT_HEADS),
        in_specs=[
            pl.BlockSpec((rows, HEAD_DIM), lambda c, h: (c, h)),
            pl.BlockSpec((nb, HEAD_DIM), lambda c, h: (c // tiles_per_seq, h)),
        ],
        out_specs=pl.BlockSpec((8, rows), lambda c, h: (h, c)),
        out_shape=jax.ShapeDtypeStruct((N_ATT_HEADS * 8, n), I32),
        compiler_params=_params("parallel", "parallel"),
        name="prompt_select",
    )(q, kmeans)


def _moba_prompt_kernel(pt_ref, slopes_ref, q_ref, k_ref, vt_ref, idx_ref, kc_hbm, o_ref, ps_ref,
                        sa_scr, sb_scr, bias_scr, pbuf, psem, *, group, n_steps):
    hg = pl.program_id(1)
    i = pl.program_id(2)
    bs = MOBA_BLOCK
    nb = vt_ref.shape[0]
    t = (pl.program_id(0) * pl.num_programs(1) + hg) * nb + i

    def page_copy(tt, slot, g):
        return pltpu.make_async_copy(kc_hbm.at[pt_ref[tt * group + g]], pbuf.at[slot, g], psem.at[slot])

    @pl.when(t == 0)
    def _():
        for g in range(group):
            page_copy(0, 0, g).start()

    @pl.when(t + 1 < n_steps)
    def _():
        for g in range(group):
            page_copy(t + 1, (t + 1) % 2, g).start()

    pslot = t % 2
    for g in range(group):
        page_copy(t, pslot, g).wait()
    for bk in range(group // PAGES_PER_BLOCK):
        acc = jnp.sum(pbuf[pslot, PAGES_PER_BLOCK * bk], axis=0)
        for p in range(1, PAGES_PER_BLOCK):
            acc = acc + jnp.sum(pbuf[pslot, PAGES_PER_BLOCK * bk + p], axis=0)
        ps_ref[bk] = acc

    krow = lax.broadcasted_iota(I32, (bs, bs), 0)
    qcol = lax.broadcasted_iota(I32, (bs, bs), 1)
    ones_rows = jnp.ones((SUM_ROWS, bs), BF16)
    hb = MOBA_HALF

    def head_job(hh):
        lanes = slice(hh * HEAD_DIM, (hh + 1) * HEAD_DIM)
        slope2 = slopes_ref[hg * HEADS_PER_STEP + hh] * LOG2E
        q2 = (q_ref[:, lanes] * (HEAD_DIM ** -0.5 * LOG2E)).astype(BF16)
        i0 = idx_ref[8 * hh:8 * hh + 1, :]
        i1 = idx_ref[8 * hh + 1:8 * hh + 2, :]
        i2 = idx_ref[8 * hh + 2:8 * hh + 3, :]
        bias_ref, sa_ref, sb_ref = bias_scr.at[hh], sa_scr.at[hh], sb_scr.at[hh]
        bias_ref[...] = slope2 * krow.astype(F32)

        def pv_dot(jc, p):
            return _dot(jnp.concatenate([vt_ref[jc, lanes, :], ones_rows], axis=0), p.astype(BF16))

        def selected(j):
            return (i0 == j) | (i1 == j) | (i2 == j)

        def block_offset(j):
            return slope2 * ((j - i) * bs).astype(F32)

        def sweep1(jb, s_ref):
            r0 = pl.multiple_of(jnp.minimum(jb, nb - hb) * bs, bs)
            s = _dot_nt(k_ref[pl.ds(r0, hb * bs), lanes], q2)
            cmax = jnp.full((1, bs), NEG_INF, F32)
            for u in range(hb):
                su = s[u * bs:(u + 1) * bs] + bias_ref[...]
                s_ref[u] = su
                cmax = jnp.maximum(cmax, jnp.where(selected(jb + u),
                                                   jnp.max(su, axis=0, keepdims=True) + block_offset(jb + u),
                                                   NEG_INF))
            return cmax

        def sweep2(jb, s_ref, m_prev, m_cur, l, acc):
            pv = jnp.zeros((HEAD_DIM + SUM_ROWS, bs), F32)
            for u in range(hb):
                j = jb + u
                ref = jnp.where(selected(j), m_cur - block_offset(j), -NEG_INF)
                pv = pv + pv_dot(jnp.minimum(j, nb - 1), jnp.exp2(s_ref[u] - ref))
            alpha = jnp.exp2(m_prev - m_cur)
            return alpha * l + pv[HEAD_DIM:HEAD_DIM + 1], alpha * acc + pv[:HEAD_DIM]

        def prologue():
            r_own = pl.multiple_of(i * bs, bs)
            s = jnp.where(krow <= qcol, _dot_nt(k_ref[pl.ds(r_own, bs), lanes], q2) + bias_ref[...], NEG_INF)
            m = jnp.max(s, axis=0, keepdims=True)
            pv = pv_dot(i, jnp.exp2(s - m))
            m_a = jnp.maximum(m, sweep1(0, sa_ref))
            m_b = jnp.maximum(m_a, sweep1(hb, sb_ref))
            return m, m_a, m_b, pv[HEAD_DIM:HEAD_DIM + 1], pv[:HEAD_DIM]

        def chunk(c, carry):
            m_prev, m_a, m_b, l, acc = carry
            jb = c * (2 * hb)
            l, acc = sweep2(jb, sa_ref, m_prev, m_a, l, acc)
            m_a2 = jnp.maximum(m_b, sweep1(jb + 2 * hb, sa_ref))
            l, acc = sweep2(jb + hb, sb_ref, m_a, m_b, l, acc)
            m_b2 = jnp.maximum(m_a2, sweep1(jb + 3 * hb, sb_ref))
            return m_b, m_a2, m_b2, l, acc

        def finish(carry):
            _, _, _, l, acc = carry
            o_ref[:, lanes] = (acc / l).T.astype(o_ref.dtype)

        return prologue, chunk, finish

    jobs = [head_job(hh) for hh in range(HEADS_PER_STEP)]
    carries = tuple(job[0]() for job in jobs)
    n_chunks = (i + 2 * hb - 1) // (2 * hb)
    carries = lax.fori_loop(0, n_chunks, lambda c, cs: tuple(job[1](c, cr) for job, cr in zip(jobs, cs)), carries)
    for job, cr in zip(jobs, carries):
        job[2](cr)


def _moba_prompt(page_table, slopes, q, kb, vt, idx, cache_k, batch, seq):
    n = q.shape[0]
    nb = seq // MOBA_BLOCK
    bs = MOBA_BLOCK
    hps = HEADS_PER_STEP
    n_hg = N_ATT_HEADS // hps
    n_steps = batch * n_hg * nb
    n_pages_total = page_table.size
    assert n_pages_total % (n_steps * PAGES_PER_BLOCK) == 0
    group = n_pages_total // n_steps
    bpc = group // PAGES_PER_BLOCK
    grid_spec = pltpu.PrefetchScalarGridSpec(
        num_scalar_prefetch=1,
        grid=(batch, n_hg, nb),
        in_specs=[
            pl.BlockSpec(memory_space=pltpu.SMEM),
            pl.BlockSpec((bs, hps * HEAD_DIM), lambda b, h, i, pt: (b * nb + i, h)),
            pl.BlockSpec((seq, hps * HEAD_DIM), lambda b, h, i, pt: (b, h)),
            pl.BlockSpec((nb, hps * HEAD_DIM, bs), lambda b, h, i, pt: (b, h, 0)),
            pl.BlockSpec((8 * hps, bs), lambda b, h, i, pt: (h, b * nb + i)),
            pl.BlockSpec(memory_space=pl.ANY),
        ],
        out_specs=[
            pl.BlockSpec((bs, hps * HEAD_DIM), lambda b, h, i, pt: (b * nb + i, h)),
            pl.BlockSpec((bpc, N_ATT_HEADS, HEAD_DIM), lambda b, h, i, pt: ((b * n_hg + h) * nb + i, 0, 0)),
        ],
        scratch_shapes=[
            pltpu.VMEM((hps, MOBA_HALF, bs, bs), F32),
            pltpu.VMEM((hps, MOBA_HALF, bs, bs), F32),
            pltpu.VMEM((hps, bs, bs), F32),
            pltpu.VMEM((2, group, PAGE_SIZE, N_ATT_HEADS, HEAD_DIM), F32),
            pltpu.SemaphoreType.DMA((2,)),
        ],
    )
    return pl.pallas_call(
        functools.partial(_moba_prompt_kernel, group=group, n_steps=n_steps),
        grid_spec=grid_spec,
        out_shape=[
            jax.ShapeDtypeStruct((n, ATT_WIDTH), BF16),
            jax.ShapeDtypeStruct((n_steps * bpc, N_ATT_HEADS, HEAD_DIM), F32),
        ],
        compiler_params=_params("arbitrary", "arbitrary", "arbitrary"),
        name="moba_prompt",
    )(page_table.reshape(-1), slopes, q, kb, vt, idx, cache_k)


def _sample_select_kernel(q_ref, bs_ref, kn_ref, o_ref, *, n_past_blocks, past):
    nq = q_ref.shape[0]
    col = lax.broadcasted_iota(I32, (nq, LANES), 1)
    lane = col
    n_full = (past + lax.broadcasted_iota(I32, (nq, 1), 0)) // MOBA_BLOCK
    for h in range(N_ATT_HEADS):
        sl = slice(h * HEAD_DIM, (h + 1) * HEAD_DIM)
        m_past = bs_ref[:, sl] * (1.0 / MOBA_BLOCK)
        m_new = jnp.sum(kn_ref[:, sl], axis=0, keepdims=True) * (1.0 / MOBA_BLOCK)
        row8 = lax.broadcasted_iota(I32, (8, HEAD_DIM), 0)
        new_rows = jnp.where(row8 == 0, jnp.broadcast_to(m_new, (8, HEAD_DIM)), 0.0)
        means = jnp.concatenate(
            [m_past, new_rows, jnp.zeros((LANES - n_past_blocks - 8, HEAD_DIM), F32)], axis=0)
        gate = lax.dot_general(q_ref[:, sl], means, _NT, precision=_HI, preferred_element_type=F32)
        gate = jnp.where(col < n_full, gate, NEG_INF)
        i0, i1, i2 = _top3(gate, col, n_full)
        o_ref[0, h] = jnp.where(lane == 0, i0, jnp.where(lane == 1, i1, jnp.where(lane == 2, i2, 0)))


def _sample_select(q, block_sums, k_new, db, nq, n_past_blocks):
    return pl.pallas_call(
        functools.partial(_sample_select_kernel, n_past_blocks=n_past_blocks,
                          past=n_past_blocks * MOBA_BLOCK),
        grid=(db,),
        in_specs=[
            pl.BlockSpec((nq, ATT_WIDTH), lambda s: (s, 0)),
            pl.BlockSpec((n_past_blocks, ATT_WIDTH), lambda s: (s, 0)),
            pl.BlockSpec((nq, ATT_WIDTH), lambda s: (s, 0)),
        ],
        out_specs=pl.BlockSpec((1, N_ATT_HEADS, nq, LANES), lambda s: (s, 0, 0, 0)),
        out_shape=jax.ShapeDtypeStruct((db, N_ATT_HEADS, nq, LANES), I32),
        compiler_params=_params("parallel"),
        name="sample_select",
    )(q, block_sums, k_new)


def _moba_sample_kernel(pt_ref, idx_ref, slopes_ref, q_ref, kn_ref, vn_ref, kc_hbm, vc_hbm, o_ref,
                        kbuf, vbuf, kown, vown, sem, *, n_steps, n_pages, nq, past):
    t = pl.program_id(0)
    nh = N_ATT_HEADS
    n_sel = nq * MOBA_TOPK
    n_slots = n_sel * PAGES_PER_BLOCK

    def copies(tt, slot, g):
        s = tt // nh
        h = tt % nh
        blk = idx_ref[tt * n_sel + g // PAGES_PER_BLOCK]
        page = pt_ref[s * n_pages + blk * PAGES_PER_BLOCK + g % PAGES_PER_BLOCK]
        return (pltpu.make_async_copy(kc_hbm.at[page, :, h, :], kbuf.at[slot, g], sem.at[0, slot]),
                pltpu.make_async_copy(vc_hbm.at[page, :, h, :], vbuf.at[slot, g], sem.at[1, slot]))

    def start(tt, slot):
        for g in range(n_slots):
            ck, cv = copies(tt, slot, g)
            ck.start()
            cv.start()

    @pl.when(t == 0)
    def _():
        start(0, 0)
        kown[...] = jnp.zeros_like(kown)
        vown[...] = jnp.zeros_like(vown)

    @pl.when(t + 1 < n_steps)
    def _():
        start(t + 1, (t + 1) % 2)

    slot = t % 2
    h = t % nh
    slope = slopes_ref[h]
    scale = HEAD_DIM ** -0.5
    qb = q_ref[...].astype(BF16)
    n_keys = n_slots * PAGE_SIZE
    keys_per_q = MOBA_TOPK * MOBA_BLOCK

    kown[0:nq, :] = kn_ref[...]
    vown[0:nq, :] = vn_ref[...]
    row = lax.broadcasted_iota(I32, (nq, LANES), 0)
    colo = lax.broadcasted_iota(I32, (nq, LANES), 1)
    s_own = _dot_nt(qb, kown[...].astype(BF16)) * scale - slope * (row - colo).astype(F32)
    s_own = jnp.where(colo <= row, s_own, NEG_INF)

    colk = lax.broadcasted_iota(I32, (1, n_keys), 1)
    grp = colk // MOBA_BLOCK
    blk_of_col = jnp.zeros((1, n_keys), I32)
    for g in range(n_sel):
        blk_of_col = jnp.where(grp == g, idx_ref[t * n_sel + g], blk_of_col)
    pos = blk_of_col * MOBA_BLOCK + colk % MOBA_BLOCK
    rowk = lax.broadcasted_iota(I32, (nq, n_keys), 0)
    mine = (lax.broadcasted_iota(I32, (nq, n_keys), 1) // keys_per_q) == rowk
    n_full = (past + rowk) // MOBA_BLOCK
    pick = (lax.broadcasted_iota(I32, (nq, n_keys), 1) // MOBA_BLOCK) % MOBA_TOPK
    valid = mine & (pick < n_full)

    for g in range(n_slots):
        ck, cv = copies(t, slot, g)
        ck.wait()
        cv.wait()

    kall = kbuf[slot].reshape(n_keys, HEAD_DIM).astype(BF16)
    s_sel = _dot_nt(qb, kall) * scale - slope * ((past + rowk) - pos).astype(F32)
    s_sel = jnp.where(valid, s_sel, NEG_INF)
    m = jnp.maximum(jnp.max(s_sel, axis=-1, keepdims=True), jnp.max(s_own, axis=-1, keepdims=True))
    p_sel = jnp.exp(s_sel - m)
    p_own = jnp.exp(s_own - m)
    l = jnp.sum(p_sel, axis=-1, keepdims=True) + jnp.sum(p_own, axis=-1, keepdims=True)
    vall = vbuf[slot].reshape(n_keys, HEAD_DIM).astype(BF16)
    o = _dot(p_sel.astype(BF16), vall) + _dot(p_own.astype(BF16), vown[...].astype(BF16))
    o_ref[...] = o / l


def _moba_sample(page_table, idx, slopes, q, k_new, v_new, cache_k, cache_v, db, nq):
    n_pages = page_table.shape[1]
    n_steps = db * N_ATT_HEADS
    n_slots = nq * MOBA_TOPK * PAGES_PER_BLOCK
    qspec = pl.BlockSpec((nq, HEAD_DIM), lambda t, pt, ix: (t // N_ATT_HEADS, t % N_ATT_HEADS))
    grid_spec = pltpu.PrefetchScalarGridSpec(
        num_scalar_prefetch=2,
        grid=(n_steps,),
        in_specs=[
            pl.BlockSpec(memory_space=pltpu.SMEM),
            qspec, qspec, qspec,
            pl.BlockSpec(memory_space=pl.ANY),
            pl.BlockSpec(memory_space=pl.ANY),
        ],
        out_specs=qspec,
        scratch_shapes=[
            pltpu.VMEM((2, n_slots, PAGE_SIZE, HEAD_DIM), F32),
            pltpu.VMEM((2, n_slots, PAGE_SIZE, HEAD_DIM), F32),
            pltpu.VMEM((LANES, HEAD_DIM), F32),
            pltpu.VMEM((LANES, HEAD_DIM), F32),
            pltpu.SemaphoreType.DMA((2, 2)),
        ],
    )
    return pl.pallas_call(
        functools.partial(_moba_sample_kernel, n_steps=n_steps, n_pages=n_pages, nq=nq,
                          past=n_pages * PAGE_SIZE),
        grid_spec=grid_spec,
        out_shape=jax.ShapeDtypeStruct((db * nq, ATT_WIDTH), F32),
        compiler_params=_params("arbitrary"),
        name="moba_sample",
    )(page_table.reshape(-1), idx.reshape(-1), slopes, q, k_new, v_new, cache_k, cache_v)


def _gla_kernel(gq_ref, gk_ref, gv_ref, gg_ref, ga_ref, wah_ref, wal_ref, ba_ref, nw_ref, s0_ref,
                o_ref, sfin_ref, s_scr, qd_scr, oi_scr, ds_scr, dec_scr, *, chunk, n_sub, t_valid):
    t = pl.program_id(2)

    @pl.when(t == 0)
    def _():
        s_scr[...] = s0_ref[0, 0]

    c = chunk
    tri = lax.broadcasted_iota(I32, (c, c), 0) >= lax.broadcasted_iota(I32, (c, c), 1)
    tri_b = jnp.where(tri, 1.0, 0.0).astype(BF16)
    ones_b = jnp.ones((c, GLA_DK), BF16)

    def split3(x):
        hi = x.astype(BF16)
        r = x - hi.astype(F32)
        mid = r.astype(BF16)
        return hi, mid, (r - mid.astype(F32)).astype(BF16)

    ga = ga_ref[...]
    ga_hi = ga.astype(BF16)
    ga_lo = (ga - ga_hi.astype(F32)).astype(BF16)
    pre = _dot(ga_hi, wah_ref[...]) + _dot(ga_hi, wal_ref[...]) + _dot(ga_lo, wah_ref[...]) + ba_ref[...]
    la = (jnp.minimum(pre, 0.0) - jnp.log1p(jnp.exp(-jnp.abs(pre)))) * (1.0 / GLA_TAU)
    if t_valid < c * n_sub:
        la = jnp.where(lax.broadcasted_iota(I32, la.shape, 0) < t_valid, la, 0.0)

    for ci in range(n_sub):
        sl = slice(ci * c, (ci + 1) * c)
        parts = split3(la[sl])
        b = _dot(tri_b, parts[0]) + _dot(tri_b, parts[1]) + _dot(tri_b, parts[2])
        b_end = b[c - 1:c, :]
        b_end_col = _dot_tn(parts[0], ones_b) + _dot_tn(parts[1], ones_b) + _dot_tn(parts[2], ones_b)
        q = gq_ref[sl, :] * (GLA_DK ** -0.5)
        k = gk_ref[sl, :]
        v = gv_ref[sl, :].astype(BF16)
        q_dec = (q * jnp.exp(b)).astype(BF16)
        k_inv = (k * jnp.exp(-b)).astype(BF16)
        k_end = (k * jnp.exp(b_end - b)).astype(BF16)
        a = jnp.where(tri, _dot_nt(q_dec, k_inv), 0.0)
        qd_scr[sl, :] = q_dec
        oi_scr[sl, :] = _dot(a.astype(BF16), v)
        ds_scr[ci] = _dot_tn(k_end, v)
        dec_scr[ci] = jnp.exp(b_end_col)

    state = s_scr[...]
    nw = nw_ref[...]
    for ci in range(n_sub):
        sl = slice(ci * c, (ci + 1) * c)
        o = oi_scr[sl, :] + _dot(qd_scr[sl, :], state.astype(BF16))
        state = jnp.concatenate([dec_scr[ci]] * (GLA_DV // GLA_DK), axis=1) * state + ds_scr[ci]
        on = o * lax.rsqrt(jnp.mean(o * o, axis=-1, keepdims=True) + RMS_EPS) * nw
        g = gg_ref[sl, :]
        o_ref[sl, :] = (on * (g * jax.nn.sigmoid(g))).astype(o_ref.dtype)
    s_scr[...] = state

    @pl.when(t == pl.num_programs(2) - 1)
    def _():
        sfin_ref[0, 0] = state


def _gla(gqk, gv, gg, ga, wa_hi, wa_lo, ba, nw, s0, batch, seq, t_valid):
    n = gqk.shape[0]
    c = GLA_CHUNK
    n_sub = min(8, seq // c)
    rows = c * n_sub
    steps = seq // rows
    nh = N_GLA_HEADS
    return pl.pallas_call(
        functools.partial(_gla_kernel, chunk=c, n_sub=n_sub, t_valid=t_valid),
        grid=(batch, nh, steps),
        in_specs=[
            pl.BlockSpec((rows, GLA_DK), lambda b, h, t: (b * steps + t, h)),
            pl.BlockSpec((rows, GLA_DK), lambda b, h, t: (b * steps + t, nh + h)),
            pl.BlockSpec((rows, GLA_DV), lambda b, h, t: (b * steps + t, h)),
            pl.BlockSpec((rows, GLA_DV), lambda b, h, t: (b * steps + t, h)),
            pl.BlockSpec((rows, LANES), lambda b, h, t: (b * steps + t, 0)),
            pl.BlockSpec((LANES, GLA_DK), lambda b, h, t: (0, h)),
            pl.BlockSpec((LANES, GLA_DK), lambda b, h, t: (0, h)),
            pl.BlockSpec((1, GLA_DK), lambda b, h, t: (0, h)),
            pl.BlockSpec((1, GLA_DV), lambda b, h, t: (0, 0)),
            pl.BlockSpec((1, 1, GLA_DK, GLA_DV), lambda b, h, t: (b, h, 0, 0)),
        ],
        out_specs=[
            pl.BlockSpec((rows, GLA_DV), lambda b, h, t: (b * steps + t, h)),
            pl.BlockSpec((1, 1, GLA_DK, GLA_DV), lambda b, h, t: (b, h, 0, 0)),
        ],
        out_shape=[
            jax.ShapeDtypeStruct((n, GLA_VW), BF16),
            jax.ShapeDtypeStruct(s0.shape, F32),
        ],
        scratch_shapes=[
            pltpu.VMEM((GLA_DK, GLA_DV), F32),
            pltpu.VMEM((rows, GLA_DK), BF16),
            pltpu.VMEM((rows, GLA_DV), F32),
            pltpu.VMEM((n_sub, GLA_DK, GLA_DV), F32),
            pltpu.VMEM((n_sub, GLA_DK, GLA_DK), F32),
        ],
        compiler_params=_params("parallel", "parallel", "arbitrary"),
        name="gla",
    )(gqk, gqk, gv, gg, ga, wa_hi, wa_lo, ba, nw, s0)


def _out_router_kernel(att_ref, gla_ref, x_ref, woa_ref, wog_ref, n2_ref, wrh_ref, wrl_ref, br_ref,
                       h_ref, hn_ref, re_ref, rg_ref):
    h = x_ref[...] + _dot(att_ref[...], woa_ref[...]) + _dot(gla_ref[...], wog_ref[...])
    h_ref[...] = h
    hn = h * lax.rsqrt(jnp.mean(h * h, axis=-1, keepdims=True) + RMS_EPS) * n2_ref[...]
    hn_ref[...] = hn
    hn_hi = hn.astype(BF16)
    hn_lo = (hn - hn_hi.astype(F32)).astype(BF16)
    logits = (_dot(hn_hi, wrh_ref[...]) + _dot(hn_hi, wrl_ref[...]) + _dot(hn_lo, wrh_ref[...])) + br_ref[...]
    col = lax.broadcasted_iota(I32, logits.shape, 1)
    big = jnp.int32(2 ** 30)
    in_g = col < N_GROUPS
    lg = jnp.where(in_g, logits, -jnp.inf)
    m1 = jnp.max(lg, axis=-1, keepdims=True)
    g_idx = jnp.min(jnp.where(lg == m1, col, big), axis=-1, keepdims=True)
    g_top = 1.0 / jnp.sum(jnp.exp(lg - m1), axis=-1, keepdims=True)
    ecol = col - N_GROUPS
    in_e = (ecol >= g_idx * EXPERTS_PER_GROUP) & (ecol < (g_idx + 1) * EXPERTS_PER_GROUP)
    le = jnp.where(in_e, logits, -jnp.inf)
    l1 = jnp.max(le, axis=-1, keepdims=True)
    e1 = jnp.min(jnp.where(le == l1, ecol, big), axis=-1, keepdims=True)
    le2 = jnp.where(ecol == e1, -jnp.inf, le)
    l2 = jnp.max(le2, axis=-1, keepdims=True)
    e2 = jnp.min(jnp.where(le2 == l2, ecol, big), axis=-1, keepdims=True)
    w2 = jnp.exp(l2 - l1)
    den = 1.0 + w2
    re_ref[...] = jnp.where(col == 0, e1, jnp.where(col == 1, e2, 0))
    rg_ref[...] = jnp.where(col == 0, g_top / den, jnp.where(col == 1, g_top * w2 / den, 0.0))


def _out_router_into_kernel(*refs):
    _out_router_kernel(*refs[:9], *refs[10:])


def _out_router_alloc_kernel(*refs, n_real):
    i = pl.program_id(0)

    @pl.when(i < n_real)
    def _():
        _out_router_kernel(*refs)

    @pl.when(i >= n_real)
    def _():
        refs[10][...] = jnp.zeros_like(refs[10])


def _out_router(att, gla, x, wo_att, wo_gla, n2, wr_hi, wr_lo, br, tm, n_all, row0, hn_all=None):
    n, d = x.shape
    n_real = n // tm
    row = lambda i: (jnp.minimum(i, n_real - 1), 0)
    fix = lambda i: (0, 0)
    b0 = row0 // tm
    if hn_all is None:
        assert row0 == 0 and n_all % tm == 0
        body, steps = functools.partial(_out_router_alloc_kernel, n_real=n_real), n_all // tm
        extra_in, extra_args, alias = [], [], {}
    else:
        body, steps = _out_router_into_kernel, n_real
        extra_in, extra_args, alias = [pl.BlockSpec(memory_space=pl.ANY)], [hn_all], {9: 1}
    return pl.pallas_call(
        body,
        grid=(steps,),
        input_output_aliases=alias,
        in_specs=[
            pl.BlockSpec((tm, ATT_WIDTH), row),
            pl.BlockSpec((tm, GLA_VW), row),
            pl.BlockSpec((tm, d), row),
            pl.BlockSpec((ATT_WIDTH, d), fix),
            pl.BlockSpec((GLA_VW, d), fix),
            pl.BlockSpec((1, d), fix),
            pl.BlockSpec((d, LANES), fix),
            pl.BlockSpec((d, LANES), fix),
            pl.BlockSpec((1, LANES), fix),
        ] + extra_in,
        out_specs=[
            pl.BlockSpec((tm, d), row),
            pl.BlockSpec((tm, d), lambda i: (b0 + i, 0)),
            pl.BlockSpec((tm, LANES), row),
            pl.BlockSpec((tm, LANES), row),
        ],
        out_shape=[
            jax.ShapeDtypeStruct((n, d), F32),
            jax.ShapeDtypeStruct((n_all, d), F32),
            jax.ShapeDtypeStruct((n, LANES), I32),
            jax.ShapeDtypeStruct((n, LANES), F32),
        ],
        compiler_params=_params("arbitrary"),
        name="out_router",
    )(att, gla, x, wo_att, wo_gla, n2, wr_hi, wr_lo, br, *extra_args)


def _expert_kernel(be_ref, tok_ref, nused_ref, hn_hbm, wg_ref, wu_ref, wd_ref, o_ref,
                   xbuf, wgb, wub, wdb, sem, *, rows, n_blocks):
    i = pl.program_id(0)

    @pl.when((i == 0) | (be_ref[i] != be_ref[jnp.maximum(i - 1, 0)]))
    def _():
        wgb[...] = wg_ref[0].astype(BF16)
        wub[...] = wu_ref[0].astype(BF16)
        wdb[...] = wd_ref[0].astype(BF16)

    n_used = nused_ref[0]

    def issue(blk, slot, r, priority):
        tok = tok_ref[blk * rows + r]
        pltpu.make_async_copy(hn_hbm.at[pl.ds(tok, 1), :], xbuf.at[slot, pl.ds(r, 1), :],
                              sem.at[slot]).start(priority=priority)

    def wait_rows(slot):
        pltpu.make_async_copy(hn_hbm.at[pl.ds(0, rows), :], xbuf.at[slot], sem.at[slot]).wait()

    @pl.when(i == 0)
    def _():
        def body(g, carry):
            for u in range(DMA_UNROLL):
                issue(0, 0, g * DMA_UNROLL + u, u % 2)
            return carry
        lax.fori_loop(0, rows // DMA_UNROLL, body, 0)

    slot = i % 2

    @pl.when(i < n_used)
    def _():
        wait_rows(slot)
        x = xbuf[slot].astype(BF16)
        for r in range(rows):
            issue(i + 1, 1 - slot, r, r % 2)
        a = _dot(x, wgb[...])
        u = _dot(x, wub[...])
        act = (a * jax.nn.sigmoid(a) * u).astype(BF16)
        o_ref[...] = _dot(act, wdb[...])

    @pl.when(i == n_used)
    def _():
        wait_rows(slot)

    @pl.when(i >= n_used)
    def _():
        o_ref[...] = jnp.zeros_like(o_ref)


def _experts(block_expert, slot_tok, n_used, hn, wg, wu, wd):
    n_blocks = block_expert.shape[0]
    rows = EXPERT_ROWS
    d = hn.shape[1]
    f = wg.shape[2]
    grid_spec = pltpu.PrefetchScalarGridSpec(
        num_scalar_prefetch=3,
        grid=(n_blocks,),
        in_specs=[
            pl.BlockSpec(memory_space=pl.ANY),
            pl.BlockSpec((1, d, f), lambda i, be, tk, nu: (be[i], 0, 0)),
            pl.BlockSpec((1, d, f), lambda i, be, tk, nu: (be[i], 0, 0)),
            pl.BlockSpec((1, f, d), lambda i, be, tk, nu: (be[i], 0, 0)),
        ],
        out_specs=pl.BlockSpec((rows, d), lambda i, be, tk, nu: (i, 0)),
        scratch_shapes=[
            pltpu.VMEM((2, rows, d), F32),
            pltpu.VMEM((d, f), BF16),
            pltpu.VMEM((d, f), BF16),
            pltpu.VMEM((f, d), BF16),
            pltpu.SemaphoreType.DMA((2,)),
        ],
    )
    return pl.pallas_call(
        functools.partial(_expert_kernel, rows=rows, n_blocks=n_blocks),
        grid_spec=grid_spec,
        out_shape=jax.ShapeDtypeStruct((n_blocks * rows, d), F32),
        compiler_params=_params("arbitrary"),
        name="experts",
    )(block_expert, slot_tok, n_used, hn, wg, wu, wd)


def _combine_kernel(pos_ref, h_ref, rg_ref, ys_hbm, y_ref, buf, sem, *, rows, n_steps, tok0):
    i = pl.program_id(0)

    def start(step, slot):
        def body(g, carry):
            for u in range(DMA_UNROLL):
                r = g * DMA_UNROLL + u
                for k in range(EXPERT_TOPK):
                    p = pos_ref[(tok0 + step * rows + r) * EXPERT_TOPK + k]
                    pltpu.make_async_copy(ys_hbm.at[pl.ds(p, 1), :], buf.at[slot, k, pl.ds(r, 1), :],
                                          sem.at[slot]).start(priority=k % 2)
            return carry
        lax.fori_loop(0, rows // DMA_UNROLL, body, 0)

    @pl.when(i == 0)
    def _():
        start(0, 0)

    @pl.when(i + 1 < n_steps)
    def _():
        start(i + 1, (i + 1) % 2)

    slot = i % 2

    for k in range(EXPERT_TOPK):
        pltpu.make_async_copy(ys_hbm.at[pl.ds(0, rows), :], buf.at[slot, k], sem.at[slot]).wait()

    rg = rg_ref[...]
    y = h_ref[...]
    for k in range(EXPERT_TOPK):
        y = y + buf[slot, k] * rg[:, k:k + 1]
    y_ref[...] = y


def _combine(pos, h, rg, ys, tok0, rows):
    n_tok, d = h.shape
    n_steps = n_tok // rows
    grid_spec = pltpu.PrefetchScalarGridSpec(
        num_scalar_prefetch=1,
        grid=(n_steps,),
        in_specs=[
            pl.BlockSpec((rows, d), lambda i, p: (i, 0)),
            pl.BlockSpec((rows, LANES), lambda i, p: (i, 0)),
            pl.BlockSpec(memory_space=pl.ANY),
        ],
        out_specs=pl.BlockSpec((rows, d), lambda i, p: (i, 0)),
        scratch_shapes=[
            pltpu.VMEM((2, EXPERT_TOPK, rows, d), F32),
            pltpu.SemaphoreType.DMA((2,)),
        ],
    )
    return pl.pallas_call(
        functools.partial(_combine_kernel, rows=rows, n_steps=n_steps, tok0=tok0),
        grid_spec=grid_spec,
        out_shape=jax.ShapeDtypeStruct((n_tok, d), F32),
        compiler_params=_params("arbitrary"),
        name="combine",
    )(pos, h, rg, ys)


def _dispatch(expert):
    a = expert.shape[0]
    rows = EXPERT_ROWS
    onehot = (expert[:, None] == jnp.arange(N_EXPERTS, dtype=I32)[None, :]).astype(I32)
    csum = jnp.cumsum(onehot, axis=0)
    counts = csum[-1]
    rank = jnp.take_along_axis(csum, expert[:, None], axis=1)[:, 0] - 1
    padded = (counts + rows - 1) // rows * rows
    ends_p = jnp.cumsum(padded)
    pstart = ends_p - padded
    pos = (pstart[expert] + rank).astype(I32)
    n_blocks = -(-a // rows) + N_EXPERTS + 1
    slot_tok = jnp.zeros((n_blocks * rows,), I32).at[pos].set(jnp.arange(a, dtype=I32) // EXPERT_TOPK)
    block_start = jnp.arange(n_blocks, dtype=I32) * rows
    block_expert = jnp.minimum(
        jnp.sum((ends_p[None, :] <= block_start[:, None]).astype(I32), axis=1), N_EXPERTS - 1).astype(I32)
    n_used = (ends_p[-1:] // rows).astype(I32)
    return pos, slot_tok, block_expert, n_used


def kernel(x_prompt, x_sample, cache_k, cache_v, state_gla, page_table, norm1_w, w_in, q_norm_w, k_norm_w,
           w_gla_a2, b_gla_a, gla_norm_w, w_out, norm2_w, w_r1, b_r1, w_r2, b_r2, w_e_gate, w_e_up, w_e_down):
    depth = w_in.shape[0]
    assert depth == 1
    batch, seq, d = x_prompt.shape
    db, nq, _ = x_sample.shape
    n_pages = page_table.shape[1]
    past = n_pages * PAGE_SIZE
    assert past % MOBA_BLOCK == 0 and nq <= GLA_CHUNK and seq % (8 * MOBA_BLOCK) == 0
    n_past_blocks = past // MOBA_BLOCK
    assert n_past_blocks + 8 <= LANES
    n_p, n_s = batch * seq, db * nq
    l = 0

    w = w_in[l]
    o_gq = 3 * ATT_WIDTH
    o_gv = o_gq + 2 * GLA_KW
    o_ga = o_gv + GLA_VW
    o_gg = o_ga + GLA_GATE_RANK
    w_main = jnp.concatenate([w[:, :o_ga], w[:, o_gg:]], axis=1).astype(BF16)
    w_ga = jnp.pad(w[:, o_ga:o_gg], ((0, 0), (0, LANES - GLA_GATE_RANK))).astype(BF16)
    n1 = norm1_w[l][None, :]
    qn = q_norm_w[l][None, :]
    kn = k_norm_w[l][None, :]
    wa = jnp.pad(w_gla_a2[l], ((0, LANES - GLA_GATE_RANK), (0, 0)))
    wa_hi = wa.astype(BF16)
    wa_lo = (wa - wa_hi.astype(F32)).astype(BF16)
    ba = b_gla_a[l][None, :]
    gnw = gla_norm_w[l][None, :]
    wo = w_out[l].astype(BF16)
    wo_att, wo_gla = wo[:ATT_WIDTH], wo[ATT_WIDTH:]
    n2 = norm2_w[l][None, :]
    wr = jnp.pad(jnp.concatenate([w_r1[l], w_r2[l]], axis=1), ((0, 0), (0, LANES - N_GROUPS - N_EXPERTS)))
    br = jnp.pad(jnp.concatenate([b_r1[l], b_r2[l]]), (0, LANES - N_GROUPS - N_EXPERTS))[None, :]
    slopes = 2.0 ** (-8.0 * jnp.arange(1, N_ATT_HEADS + 1, dtype=F32) / N_ATT_HEADS)

    xp = x_prompt.reshape(n_p, d)
    q_p, k_p, kb_p, v_p, vt_p, gqk_p, gv_p, gg_p, ga_p = _project(xp, n1, w_main, w_ga, qn, kn, 512)
    kmeans = _block_means(k_p)
    idx_p = _prompt_select(q_p, kmeans, seq)
    att_p, block_sums = _moba_prompt(page_table, slopes, q_p, kb_p, vt_p, idx_p, cache_k[l], batch, seq)
    s0_p = jnp.zeros((batch, N_GLA_HEADS, GLA_DK, GLA_DV), F32)
    gla_p, s_p = _gla(gqk_p, gv_p, gg_p, ga_p, wa_hi, wa_lo, ba, gnw, s0_p, batch, seq, seq)

    xs = x_sample.reshape(n_s, d)
    q_s, k_s, _, v_s, _, gqk_s, gv_s, gg_s, ga_s = _project(xs, n1, w_main, w_ga, qn, kn, n_s)
    idx = _sample_select(q_s, block_sums.reshape(db * n_past_blocks, ATT_WIDTH), k_s, db, nq, n_past_blocks)
    att_s = _moba_sample(page_table, idx[..., :MOBA_TOPK], slopes, q_s, k_s, v_s, cache_k[l], cache_v[l], db, nq)

    def pad_seq(a):
        return jnp.pad(a.reshape(db, nq, -1), ((0, 0), (0, GLA_CHUNK - nq), (0, 0))).reshape(db * GLA_CHUNK, -1)

    gla_s, s_s = _gla(pad_seq(gqk_s), pad_seq(gv_s), pad_seq(gg_s), pad_seq(ga_s), wa_hi, wa_lo, ba, gnw,
                      state_gla[l], db, GLA_CHUNK, nq)
    gla_s = gla_s.reshape(db, GLA_CHUNK, GLA_VW)[:, :nq].reshape(n_s, GLA_VW)

    wr_hi = wr.astype(BF16)
    wr_lo = (wr - wr_hi.astype(F32)).astype(BF16)
    n_all = -(-(n_p + n_s) // OUT_ROWS) * OUT_ROWS
    h_p, hn, re_p, rg_p = _out_router(att_p, gla_p, xp, wo_att, wo_gla, n2, wr_hi, wr_lo, br, OUT_ROWS, n_all, 0)
    h_s, hn, re_s, rg_s = _out_router(att_s.astype(BF16), gla_s, xs, wo_att, wo_gla, n2, wr_hi, wr_lo, br,
                                      n_s, n_all, n_p, hn)
    expert = jnp.concatenate([re_p[:, :EXPERT_TOPK], re_s[:, :EXPERT_TOPK]], axis=0).reshape(-1)
    pos, slot_tok, block_expert, n_used = _dispatch(expert)
    ys = _experts(block_expert, slot_tok, n_used, hn, w_e_gate[l], w_e_up[l], w_e_down[l])
    y_p = _combine(pos, h_p, rg_p, ys, 0, 128)
    y_s = _combine(pos, h_s, rg_s, ys, n_p, 128)

    hd = (N_ATT_HEADS, HEAD_DIM)
    return (y_p.reshape(batch, seq, d), y_s.reshape(db, nq, d),
            k_p.reshape(1, batch, seq, *hd), v_p.reshape(1, batch, seq, *hd), s_p[None],
            k_s.reshape(1, db, nq, *hd), v_s.reshape(1, db, nq, *hd), s_s[None])
```

```python
import functools

import jax
import jax.numpy as jnp
from jax import lax
from jax.experimental import pallas as pl
from jax.experimental.pallas import tpu as pltpu

F32 = jnp.float32
BF16 = jnp.bfloat16
I32 = jnp.int32

HEAD_DIM = 128
N_ATT_HEADS = 8
ATT_WIDTH = N_ATT_HEADS * HEAD_DIM
MOBA_BLOCK = 256
MOBA_TOPK = 3
MOBA_HALF = 2
HEADS_PER_STEP = 2
SELECT_ROWS = 2048
SUM_ROWS = 16
LOG2E = 1.4426950408889634
PAGE_SIZE = 128
PAGES_PER_BLOCK = MOBA_BLOCK // PAGE_SIZE
N_GLA_HEADS = 4
GLA_DK = 128
GLA_DV = 256
GLA_KW = N_GLA_HEADS * GLA_DK
GLA_VW = N_GLA_HEADS * GLA_DV
GLA_GATE_RANK = 16
GLA_TAU = 16.0
GLA_CHUNK = 64
N_GROUPS = 4
EXPERTS_PER_GROUP = 8
N_EXPERTS = N_GROUPS * EXPERTS_PER_GROUP
EXPERT_TOPK = 2
RMS_EPS = 1e-6
NEG_INF = -1e30
LANES = 128
EXPERT_ROWS = 256
OUT_ROWS = 512
DMA_UNROLL = 8
VMEM_LIMIT = 56 * 1024 * 1024

_NT = (((1,), (1,)), ((), ()))
_TN = (((0,), (0,)), ((), ()))
_HI = lax.Precision.HIGHEST


def _dot(a, b):
    return jnp.dot(a, b, preferred_element_type=F32)


def _dot_nt(a, b):
    return lax.dot_general(a, b, _NT, preferred_element_type=F32)


def _dot_tn(a, b):
    return lax.dot_general(a, b, _TN, preferred_element_type=F32)


def _params(*sem):
    return pltpu.CompilerParams(dimension_semantics=sem, vmem_limit_bytes=VMEM_LIMIT)


def _top3(gate, col, n_valid, axis=-1):
    picks = []
    g = gate
    for r in range(MOBA_TOPK):
        m = jnp.max(g, axis=axis, keepdims=True)
        idx = jnp.min(jnp.where(g == m, col, jnp.int32(2 ** 30)), axis=axis, keepdims=True)
        g = jnp.where(col == idx, -jnp.inf, g)
        picks.append(jnp.where(r < n_valid, idx, -1))
    return picks


def _proj_kernel(x_ref, n1_ref, w_ref, wga_ref, qn_ref, kn_ref,
                 q_ref, k_ref, kb_ref, v_ref, vt_ref, gqk_ref, gv_ref, gg_ref, ga_ref, xn_ref):
    j = pl.program_id(1)

    @pl.when(j == 0)
    def _():
        x = x_ref[...]
        y = x * lax.rsqrt(jnp.mean(x * x, axis=-1, keepdims=True) + RMS_EPS) * n1_ref[...]
        xn_ref[...] = y.astype(BF16)
        ga_ref[...] = _dot(xn_ref[...], wga_ref[...])

    z = _dot(xn_ref[...], w_ref[...])

    def head_norm(w):
        outs = []
        for h in range(N_ATT_HEADS):
            zh = z[:, h * HEAD_DIM:(h + 1) * HEAD_DIM]
            outs.append(zh * lax.rsqrt(jnp.mean(zh * zh, axis=-1, keepdims=True) + RMS_EPS) * w)
        return outs

    @pl.when(j == 0)
    def _():
        for h, y in enumerate(head_norm(qn_ref[...])):
            q_ref[:, h * HEAD_DIM:(h + 1) * HEAD_DIM] = y

    @pl.when(j == 1)
    def _():
        for h, y in enumerate(head_norm(kn_ref[...])):
            k_ref[:, h * HEAD_DIM:(h + 1) * HEAD_DIM] = y
            kb_ref[:, h * HEAD_DIM:(h + 1) * HEAD_DIM] = y.astype(BF16)

    @pl.when(j == 2)
    def _():
        v_ref[...] = z
        zt = z.T.astype(BF16)
        for c in range(vt_ref.shape[0]):
            vt_ref[c] = zt[:, c * MOBA_BLOCK:(c + 1) * MOBA_BLOCK]

    @pl.when(j == 3)
    def _():
        gqk_ref[...] = z

    @pl.when(j == 4)
    def _():
        gv_ref[...] = z

    @pl.when(j == 5)
    def _():
        gg_ref[...] = z


def _project(x, n1, w_main, w_ga, qn, kn, tm):
    n, d = x.shape
    wide = ATT_WIDTH
    row = lambda i, j: (i, 0)
    out_shape = [
        jax.ShapeDtypeStruct((n, wide), F32),
        jax.ShapeDtypeStruct((n, wide), F32),
        jax.ShapeDtypeStruct((n, wide), BF16),
        jax.ShapeDtypeStruct((n, wide), F32),
        jax.ShapeDtypeStruct((n // MOBA_BLOCK, wide, MOBA_BLOCK), BF16),
        jax.ShapeDtypeStruct((n, wide), F32),
        jax.ShapeDtypeStruct((n, wide), F32),
        jax.ShapeDtypeStruct((n, wide), F32),
        jax.ShapeDtypeStruct((n, LANES), F32),
    ]
    out_specs = [pl.BlockSpec((tm, s.shape[1]), row) if len(s.shape) == 2
                 else pl.BlockSpec((tm // MOBA_BLOCK, wide, MOBA_BLOCK), lambda i, j: (i, 0, 0)) for s in out_shape]
    return pl.pallas_call(
        _proj_kernel,
        grid=(n // tm, 6),
        in_specs=[
            pl.BlockSpec((tm, d), row),
            pl.BlockSpec((1, d), lambda i, j: (0, 0)),
            pl.BlockSpec((d, wide), lambda i, j: (0, j)),
            pl.BlockSpec((d, LANES), lambda i, j: (0, 0)),
            pl.BlockSpec((1, HEAD_DIM), lambda i, j: (0, 0)),
            pl.BlockSpec((1, HEAD_DIM), lambda i, j: (0, 0)),
        ],
        out_specs=out_specs,
        out_shape=out_shape,
        scratch_shapes=[pltpu.VMEM((tm, d), BF16)],
        compiler_params=_params("parallel", "arbitrary"),
        name="projection",
    )(x, n1, w_main, w_ga, qn, kn)


def _kmeans_kernel(k_ref, o_ref):
    rows = k_ref.shape[0]
    k = k_ref[...].reshape(rows // MOBA_BLOCK, MOBA_BLOCK, k_ref.shape[1])
    o_ref[...] = jnp.sum(k, axis=1) * (1.0 / MOBA_BLOCK)


def _block_means(k):
    n, w = k.shape
    rows = 8 * MOBA_BLOCK
    return pl.pallas_call(
        _kmeans_kernel,
        grid=(n // rows,),
        in_specs=[pl.BlockSpec((rows, w), lambda i: (i, 0))],
        out_specs=pl.BlockSpec((8, w), lambda i: (i, 0)),
        out_shape=jax.ShapeDtypeStruct((n // MOBA_BLOCK, w), F32),
        compiler_params=_params("parallel"),
        name="block_means",
    )(k)


def _prompt_select_kernel(q_ref, km_ref, o_ref, *, tiles_per_seq):
    c = pl.program_id(0)
    rows = q_ref.shape[0]
    nb = km_ref.shape[0]
    q = q_ref[...]
    km = km_ref[...]
    q_hi = q.astype(BF16)
    q_lo = (q - q_hi.astype(F32)).astype(BF16)
    km_hi = km.astype(BF16)
    km_lo = (km - km_hi.astype(F32)).astype(BF16)
    gate = _dot_nt(km_hi, q_hi) + _dot_nt(km_lo, q_hi) + _dot_nt(km_hi, q_lo)
    blk = lax.broadcasted_iota(I32, (nb, rows), 0)
    n_full = (c % tiles_per_seq) * (rows // MOBA_BLOCK) + lax.broadcasted_iota(I32, (1, rows), 1) // MOBA_BLOCK
    gate = jnp.where(blk < n_full, gate, NEG_INF)
    i0, i1, i2 = _top3(gate, blk, n_full, axis=0)
    row = lax.broadcasted_iota(I32, (8, rows), 0)
    o_ref[...] = jnp.where(row == 0, i0, jnp.where(row == 1, i1, jnp.where(row == 2, i2, -1)))


def _prompt_select(q, kmeans, seq):
    n = q.shape[0]
    rows = SELECT_ROWS
    nb = seq // MOBA_BLOCK
    tiles_per_seq = seq // rows
    return pl.pallas_call(
        functools.partial(_prompt_select_kernel, tiles_per_seq=tiles_per_seq),
        grid=(n // rows, N_ATT_HEADS),
        in_specs=[
            pl.BlockSpec((rows, HEAD_DIM), lambda c, h: (c, h)),
            pl.BlockSpec((nb, HEAD_DIM), lambda c, h: (c // tiles_per_seq, h)),
        ],
        out_specs=pl.BlockSpec((8, rows), lambda c, h: (h, c)),
        out_shape=jax.ShapeDtypeStruct((N_ATT_HEADS * 8, n), I32),
        compiler_params=_params("parallel", "parallel"),
        name="prompt_select",
    )(q, kmeans)


def _moba_prompt_kernel(pt_ref, slopes_ref, q_ref, k_ref, vt_ref, idx_ref, kc_hbm, o_ref, ps_ref,
                        sa_scr, sb_scr, bias_scr, pbuf, psem, *, group, n_steps):
    hg = pl.program_id(1)
    i = pl.program_id(2)
    bs = MOBA_BLOCK
    nb = vt_ref.shape[0]
    t = (pl.program_id(0) * pl.num_programs(1) + hg) * nb + i

    def page_copy(tt, slot, g):
        return pltpu.make_async_copy(kc_hbm.at[pt_ref[tt * group + g]], pbuf.at[slot, g], psem.at[slot])

    @pl.when(t == 0)
    def _():
        for g in range(group):
            page_copy(0, 0, g).start()

    @pl.when(t + 1 < n_steps)
    def _():
        for g in range(group):
            page_copy(t + 1, (t + 1) % 2, g).start()

    pslot = t % 2
    for g in range(group):
        page_copy(t, pslot, g).wait()
    for bk in range(group // PAGES_PER_BLOCK):
        acc = jnp.sum(pbuf[pslot, PAGES_PER_BLOCK * bk], axis=0)
        for p in range(1, PAGES_PER_BLOCK):
            acc = acc + jnp.sum(pbuf[pslot, PAGES_PER_BLOCK * bk + p], axis=0)
        ps_ref[bk] = acc

    krow = lax.broadcasted_iota(I32, (bs, bs), 0)
    qcol = lax.broadcasted_iota(I32, (bs, bs), 1)
    ones_rows = jnp.ones((SUM_ROWS, bs), BF16)
    hb = MOBA_HALF

    def head_job(hh):
        lanes = slice(hh * HEAD_DIM, (hh + 1) * HEAD_DIM)
        slope2 = slopes_ref[hg * HEADS_PER_STEP + hh] * LOG2E
        q2 = (q_ref[:, lanes] * (HEAD_DIM ** -0.5 * LOG2E)).astype(BF16)
        i0 = idx_ref[8 * hh:8 * hh + 1, :]
        i1 = idx_ref[8 * hh + 1:8 * hh + 2, :]
        i2 = idx_ref[8 * hh + 2:8 * hh + 3, :]
        bias_ref, sa_ref, sb_ref = bias_scr.at[hh], sa_scr.at[hh], sb_scr.at[hh]
        bias_ref[...] = slope2 * krow.astype(F32)

        def pv_dot(jc, p):
            return _dot(jnp.concatenate([vt_ref[jc, lanes, :], ones_rows], axis=0), p.astype(BF16))

        def selected(j):
            return (i0 == j) | (i1 == j) | (i2 == j)

        def block_offset(j):
            return slope2 * ((j - i) * bs).astype(F32)

        def sweep1(jb, s_ref):
            r0 = pl.multiple_of(jnp.minimum(jb, nb - hb) * bs, bs)
            s = _dot_nt(k_ref[pl.ds(r0, hb * bs), lanes], q2)
            cmax = jnp.full((1, bs), NEG_INF, F32)
            for u in range(hb):
                su = s[u * bs:(u + 1) * bs] + bias_ref[...]
                s_ref[u] = su
                cmax = jnp.maximum(cmax, jnp.where(selected(jb + u),
                                                   jnp.max(su, axis=0, keepdims=True) + block_offset(jb + u),
                                                   NEG_INF))
            return cmax

        def sweep2(jb, s_ref, m_prev, m_cur, l, acc):
            pv = jnp.zeros((HEAD_DIM + SUM_ROWS, bs), F32)
            for u in range(hb):
                j = jb + u
                ref = jnp.where(selected(j), m_cur - block_offset(j), -NEG_INF)
                pv = pv + pv_dot(jnp.minimum(j, nb - 1), jnp.exp2(s_ref[u] - ref))
            alpha = jnp.exp2(m_prev - m_cur)
            return alpha * l + pv[HEAD_DIM:HEAD_DIM + 1], alpha * acc + pv[:HEAD_DIM]

        def prologue():
            r_own = pl.multiple_of(i * bs, bs)
            s = jnp.where(krow <= qcol, _dot_nt(k_ref[pl.ds(r_own, bs), lanes], q2) + bias_ref[...], NEG_INF)
            m = jnp.max(s, axis=0, keepdims=True)
            pv = pv_dot(i, jnp.exp2(s - m))
            m_a = jnp.maximum(m, sweep1(0, sa_ref))
            m_b = jnp.maximum(m_a, sweep1(hb, sb_ref))
            return m, m_a, m_b, pv[HEAD_DIM:HEAD_DIM + 1], pv[:HEAD_DIM]

        def chunk(c, carry):
            m_prev, m_a, m_b, l, acc = carry
            jb = c * (2 * hb)
            l, acc = sweep2(jb, sa_ref, m_prev, m_a, l, acc)
            m_a2 = jnp.maximum(m_b, sweep1(jb + 2 * hb, sa_ref))
            l, acc = sweep2(jb + hb, sb_ref, m_a, m_b, l, acc)
            m_b2 = jnp.maximum(m_a2, sweep1(jb + 3 * hb, sb_ref))
            return m_b, m_a2, m_b2, l, acc

        def finish(carry):
            _, _, _, l, acc = carry
            o_ref[:, lanes] = (acc / l).T.astype(o_ref.dtype)

        return prologue, chunk, finish

    jobs = [head_job(hh) for hh in range(HEADS_PER_STEP)]
    carries = tuple(job[0]() for job in jobs)
    n_chunks = (i + 2 * hb - 1) // (2 * hb)
    carries = lax.fori_loop(0, n_chunks, lambda c, cs: tuple(job[1](c, cr) for job, cr in zip(jobs, cs)), carries)
    for job, cr in zip(jobs, carries):
        job[2](cr)


def _moba_prompt(page_table, slopes, q, kb, vt, idx, cache_k, batch, seq):
    n = q.shape[0]
    nb = seq // MOBA_BLOCK
    bs = MOBA_BLOCK
    hps = HEADS_PER_STEP
    n_hg = N_ATT_HEADS // hps
    n_steps = batch * n_hg * nb
    n_pages_total = page_table.size
    assert n_pages_total % (n_steps * PAGES_PER_BLOCK) == 0
    group = n_pages_total // n_steps
    bpc = group // PAGES_PER_BLOCK
    grid_spec = pltpu.PrefetchScalarGridSpec(
        num_scalar_prefetch=1,
        grid=(batch, n_hg, nb),
        in_specs=[
            pl.BlockSpec(memory_space=pltpu.SMEM),
            pl.BlockSpec((bs, hps * HEAD_DIM), lambda b, h, i, pt: (b * nb + i, h)),
            pl.BlockSpec((seq, hps * HEAD_DIM), lambda b, h, i, pt: (b, h)),
            pl.BlockSpec((nb, hps * HEAD_DIM, bs), lambda b, h, i, pt: (b, h, 0)),
            pl.BlockSpec((8 * hps, bs), lambda b, h, i, pt: (h, b * nb + i)),
            pl.BlockSpec(memory_space=pl.ANY),
        ],
        out_specs=[
            pl.BlockSpec((bs, hps * HEAD_DIM), lambda b, h, i, pt: (b * nb + i, h)),
            pl.BlockSpec((bpc, N_ATT_HEADS, HEAD_DIM), lambda b, h, i, pt: ((b * n_hg + h) * nb + i, 0, 0)),
        ],
        scratch_shapes=[
            pltpu.VMEM((hps, MOBA_HALF, bs, bs), F32),
            pltpu.VMEM((hps, MOBA_HALF, bs, bs), F32),
            pltpu.VMEM((hps, bs, bs), F32),
            pltpu.VMEM((2, group, PAGE_SIZE, N_ATT_HEADS, HEAD_DIM), F32),
            pltpu.SemaphoreType.DMA((2,)),
        ],
    )
    return pl.pallas_call(
        functools.partial(_moba_prompt_kernel, group=group, n_steps=n_steps),
        grid_spec=grid_spec,
        out_shape=[
            jax.ShapeDtypeStruct((n, ATT_WIDTH), BF16),
            jax.ShapeDtypeStruct((n_steps * bpc, N_ATT_HEADS, HEAD_DIM), F32),
        ],
        compiler_params=_params("arbitrary", "arbitrary", "arbitrary"),
        name="moba_prompt",
    )(page_table.reshape(-1), slopes, q, kb, vt, idx, cache_k)


def _sample_select_kernel(q_ref, bs_ref, kn_ref, o_ref, *, n_past_blocks, past):
    nq = q_ref.shape[0]
    col = lax.broadcasted_iota(I32, (nq, LANES), 1)
    lane = col
    n_full = (past + lax.broadcasted_iota(I32, (nq, 1), 0)) // MOBA_BLOCK
    for h in range(N_ATT_HEADS):
        sl = slice(h * HEAD_DIM, (h + 1) * HEAD_DIM)
        m_past = bs_ref[:, sl] * (1.0 / MOBA_BLOCK)
        m_new = jnp.sum(kn_ref[:, sl], axis=0, keepdims=True) * (1.0 / MOBA_BLOCK)
        row8 = lax.broadcasted_iota(I32, (8, HEAD_DIM), 0)
        new_rows = jnp.where(row8 == 0, jnp.broadcast_to(m_new, (8, HEAD_DIM)), 0.0)
        means = jnp.concatenate(
            [m_past, new_rows, jnp.zeros((LANES - n_past_blocks - 8, HEAD_DIM), F32)], axis=0)
        gate = lax.dot_general(q_ref[:, sl], means, _NT, precision=_HI, preferred_element_type=F32)
        gate = jnp.where(col < n_full, gate, NEG_INF)
        i0, i1, i2 = _top3(gate, col, n_full)
        o_ref[0, h] = jnp.where(lane == 0, i0, jnp.where(lane == 1, i1, jnp.where(lane == 2, i2, 0)))


def _sample_select(q, block_sums, k_new, db, nq, n_past_blocks):
    return pl.pallas_call(
        functools.partial(_sample_select_kernel, n_past_blocks=n_past_blocks,
                          past=n_past_blocks * MOBA_BLOCK),
        grid=(db,),
        in_specs=[
            pl.BlockSpec((nq, ATT_WIDTH), lambda s: (s, 0)),
            pl.BlockSpec((n_past_blocks, ATT_WIDTH), lambda s: (s, 0)),
            pl.BlockSpec((nq, ATT_WIDTH), lambda s: (s, 0)),
        ],
        out_specs=pl.BlockSpec((1, N_ATT_HEADS, nq, LANES), lambda s: (s, 0, 0, 0)),
        out_shape=jax.ShapeDtypeStruct((db, N_ATT_HEADS, nq, LANES), I32),
        compiler_params=_params("parallel"),
        name="sample_select",
    )(q, block_sums, k_new)


def _moba_sample_kernel(pt_ref, idx_ref, slopes_ref, q_ref, kn_ref, vn_ref, kc_hbm, vc_hbm, o_ref,
                        kbuf, vbuf, kown, vown, sem, *, n_steps, n_pages, nq, past):
    t = pl.program_id(0)
    nh = N_ATT_HEADS
    n_sel = nq * MOBA_TOPK
    n_slots = n_sel * PAGES_PER_BLOCK

    def copies(tt, slot, g):
        s = tt // nh
        h = tt % nh
        blk = idx_ref[tt * n_sel + g // PAGES_PER_BLOCK]
        page = pt_ref[s * n_pages + blk * PAGES_PER_BLOCK + g % PAGES_PER_BLOCK]
        return (pltpu.make_async_copy(kc_hbm.at[page, :, h, :], kbuf.at[slot, g], sem.at[0, slot]),
                pltpu.make_async_copy(vc_hbm.at[page, :, h, :], vbuf.at[slot, g], sem.at[1, slot]))

    def start(tt, slot):
        for g in range(n_slots):
            ck, cv = copies(tt, slot, g)
            ck.start()
            cv.start()

    @pl.when(t == 0)
    def _():
        start(0, 0)
        kown[...] = jnp.zeros_like(kown)
        vown[...] = jnp.zeros_like(vown)

    @pl.when(t + 1 < n_steps)
    def _():
        start(t + 1, (t + 1) % 2)

    slot = t % 2
    h = t % nh
    slope = slopes_ref[h]
    scale = HEAD_DIM ** -0.5
    qb = q_ref[...].astype(BF16)
    n_keys = n_slots * PAGE_SIZE
    keys_per_q = MOBA_TOPK * MOBA_BLOCK

    kown[0:nq, :] = kn_ref[...]
    vown[0:nq, :] = vn_ref[...]
    row = lax.broadcasted_iota(I32, (nq, LANES), 0)
    colo = lax.broadcasted_iota(I32, (nq, LANES), 1)
    s_own = _dot_nt(qb, kown[...].astype(BF16)) * scale - slope * (row - colo).astype(F32)
    s_own = jnp.where(colo <= row, s_own, NEG_INF)

    colk = lax.broadcasted_iota(I32, (1, n_keys), 1)
    grp = colk // MOBA_BLOCK
    blk_of_col = jnp.zeros((1, n_keys), I32)
    for g in range(n_sel):
        blk_of_col = jnp.where(grp == g, idx_ref[t * n_sel + g], blk_of_col)
    pos = blk_of_col * MOBA_BLOCK + colk % MOBA_BLOCK
    rowk = lax.broadcasted_iota(I32, (nq, n_keys), 0)
    mine = (lax.broadcasted_iota(I32, (nq, n_keys), 1) // keys_per_q) == rowk
    n_full = (past + rowk) // MOBA_BLOCK
    pick = (lax.broadcasted_iota(I32, (nq, n_keys), 1) // MOBA_BLOCK) % MOBA_TOPK
    valid = mine & (pick < n_full)

    for g in range(n_slots):
        ck, cv = copies(t, slot, g)
        ck.wait()
        cv.wait()

    kall = kbuf[slot].reshape(n_keys, HEAD_DIM).astype(BF16)
    s_sel = _dot_nt(qb, kall) * scale - slope * ((past + rowk) - pos).astype(F32)
    s_sel = jnp.where(valid, s_sel, NEG_INF)
    m = jnp.maximum(jnp.max(s_sel, axis=-1, keepdims=True), jnp.max(s_own, axis=-1, keepdims=True))
    p_sel = jnp.exp(s_sel - m)
    p_own = jnp.exp(s_own - m)
    l = jnp.sum(p_sel, axis=-1, keepdims=True) + jnp.sum(p_own, axis=-1, keepdims=True)
    vall = vbuf[slot].reshape(n_keys, HEAD_DIM).astype(BF16)
    o = _dot(p_sel.astype(BF16), vall) + _dot(p_own.astype(BF16), vown[...].astype(BF16))
    o_ref[...] = o / l


def _moba_sample(page_table, idx, slopes, q, k_new, v_new, cache_k, cache_v, db, nq):
    n_pages = page_table.shape[1]
    n_steps = db * N_ATT_HEADS
    n_slots = nq * MOBA_TOPK * PAGES_PER_BLOCK
    qspec = pl.BlockSpec((nq, HEAD_DIM), lambda t, pt, ix: (t // N_ATT_HEADS, t % N_ATT_HEADS))
    grid_spec = pltpu.PrefetchScalarGridSpec(
        num_scalar_prefetch=2,
        grid=(n_steps,),
        in_specs=[
            pl.BlockSpec(memory_space=pltpu.SMEM),
            qspec, qspec, qspec,
            pl.BlockSpec(memory_space=pl.ANY),
            pl.BlockSpec(memory_space=pl.ANY),
        ],
        out_specs=qspec,
        scratch_shapes=[
            pltpu.VMEM((2, n_slots, PAGE_SIZE, HEAD_DIM), F32),
            pltpu.VMEM((2, n_slots, PAGE_SIZE, HEAD_DIM), F32),
            pltpu.VMEM((LANES, HEAD_DIM), F32),
            pltpu.VMEM((LANES, HEAD_DIM), F32),
            pltpu.SemaphoreType.DMA((2, 2)),
        ],
    )
    return pl.pallas_call(
        functools.partial(_moba_sample_kernel, n_steps=n_steps, n_pages=n_pages, nq=nq,
                          past=n_pages * PAGE_SIZE),
        grid_spec=grid_spec,
        out_shape=jax.ShapeDtypeStruct((db * nq, ATT_WIDTH), F32),
        compiler_params=_params("arbitrary"),
        name="moba_sample",
    )(page_table.reshape(-1), idx.reshape(-1), slopes, q, k_new, v_new, cache_k, cache_v)


def _gla_kernel(gq_ref, gk_ref, gv_ref, gg_ref, ga_ref, wah_ref, wal_ref, ba_ref, nw_ref, s0_ref,
                o_ref, sfin_ref, s_scr, qd_scr, oi_scr, ds_scr, dec_scr, *, chunk, n_sub, t_valid):
    t = pl.program_id(2)

    @pl.when(t == 0)
    def _():
        s_scr[...] = s0_ref[0, 0]

    c = chunk
    tri = lax.broadcasted_iota(I32, (c, c), 0) >= lax.broadcasted_iota(I32, (c, c), 1)
    tri_b = jnp.where(tri, 1.0, 0.0).astype(BF16)
    ones_b = jnp.ones((c, GLA_DK), BF16)

    def split3(x):
        hi = x.astype(BF16)
        r = x - hi.astype(F32)
        mid = r.astype(BF16)
        return hi, mid, (r - mid.astype(F32)).astype(BF16)

    ga = ga_ref[...]
    ga_hi = ga.astype(BF16)
    ga_lo = (ga - ga_hi.astype(F32)).astype(BF16)
    pre = _dot(ga_hi, wah_ref[...]) + _dot(ga_hi, wal_ref[...]) + _dot(ga_lo, wah_ref[...]) + ba_ref[...]
    la = (jnp.minimum(pre, 0.0) - jnp.log1p(jnp.exp(-jnp.abs(pre)))) * (1.0 / GLA_TAU)
    if t_valid < c * n_sub:
        la = jnp.where(lax.broadcasted_iota(I32, la.shape, 0) < t_valid, la, 0.0)

    for ci in range(n_sub):
        sl = slice(ci * c, (ci + 1) * c)
        parts = split3(la[sl])
        b = _dot(tri_b, parts[0]) + _dot(tri_b, parts[1]) + _dot(tri_b, parts[2])
        b_end = b[c - 1:c, :]
        b_end_col = _dot_tn(parts[0], ones_b) + _dot_tn(parts[1], ones_b) + _dot_tn(parts[2], ones_b)
        q = gq_ref[sl, :] * (GLA_DK ** -0.5)
        k = gk_ref[sl, :]
        v = gv_ref[sl, :].astype(BF16)
        q_dec = (q * jnp.exp(b)).astype(BF16)
        k_inv = (k * jnp.exp(-b)).astype(BF16)
        k_end = (k * jnp.exp(b_end - b)).astype(BF16)
        a = jnp.where(tri, _dot_nt(q_dec, k_inv), 0.0)
        qd_scr[sl, :] = q_dec
        oi_scr[sl, :] = _dot(a.astype(BF16), v)
        ds_scr[ci] = _dot_tn(k_end, v)
        dec_scr[ci] = jnp.exp(b_end_col)

    state = s_scr[...]
    nw = nw_ref[...]
    for ci in range(n_sub):
        sl = slice(ci * c, (ci + 1) * c)
        o = oi_scr[sl, :] + _dot(qd_scr[sl, :], state.astype(BF16))
        state = jnp.concatenate([dec_scr[ci]] * (GLA_DV // GLA_DK), axis=1) * state + ds_scr[ci]
        on = o * lax.rsqrt(jnp.mean(o * o, axis=-1, keepdims=True) + RMS_EPS) * nw
        g = gg_ref[sl, :]
        o_ref[sl, :] = (on * (g * jax.nn.sigmoid(g))).astype(o_ref.dtype)
    s_scr[...] = state

    @pl.when(t == pl.num_programs(2) - 1)
    def _():
        sfin_ref[0, 0] = state


def _gla(gqk, gv, gg, ga, wa_hi, wa_lo, ba, nw, s0, batch, seq, t_valid):
    n = gqk.shape[0]
    c = GLA_CHUNK
    n_sub = min(8, seq // c)
    rows = c * n_sub
    steps = seq // rows
    nh = N_GLA_HEADS
    return pl.pallas_call(
        functools.partial(_gla_kernel, chunk=c, n_sub=n_sub, t_valid=t_valid),
        grid=(batch, nh, steps),
        in_specs=[
            pl.BlockSpec((rows, GLA_DK), lambda b, h, t: (b * steps + t, h)),
            pl.BlockSpec((rows, GLA_DK), lambda b, h, t: (b * steps + t, nh + h)),
            pl.BlockSpec((rows, GLA_DV), lambda b, h, t: (b * steps + t, h)),
            pl.BlockSpec((rows, GLA_DV), lambda b, h, t: (b * steps + t, h)),
            pl.BlockSpec((rows, LANES), lambda b, h, t: (b * steps + t, 0)),
            pl.BlockSpec((LANES, GLA_DK), lambda b, h, t: (0, h)),
            pl.BlockSpec((LANES, GLA_DK), lambda b, h, t: (0, h)),
            pl.BlockSpec((1, GLA_DK), lambda b, h, t: (0, h)),
            pl.BlockSpec((1, GLA_DV), lambda b, h, t: (0, 0)),
            pl.BlockSpec((1, 1, GLA_DK, GLA_DV), lambda b, h, t: (b, h, 0, 0)),
        ],
        out_specs=[
            pl.BlockSpec((rows, GLA_DV), lambda b, h, t: (b * steps + t, h)),
            pl.BlockSpec((1, 1, GLA_DK, GLA_DV), lambda b, h, t: (b, h, 0, 0)),
        ],
        out_shape=[
            jax.ShapeDtypeStruct((n, GLA_VW), BF16),
            jax.ShapeDtypeStruct(s0.shape, F32),
        ],
        scratch_shapes=[
            pltpu.VMEM((GLA_DK, GLA_DV), F32),
            pltpu.VMEM((rows, GLA_DK), BF16),
            pltpu.VMEM((rows, GLA_DV), F32),
            pltpu.VMEM((n_sub, GLA_DK, GLA_DV), F32),
            pltpu.VMEM((n_sub, GLA_DK, GLA_DK), F32),
        ],
        compiler_params=_params("parallel", "parallel", "arbitrary"),
        name="gla",
    )(gqk, gqk, gv, gg, ga, wa_hi, wa_lo, ba, nw, s0)


def _out_router_kernel(att_ref, gla_ref, x_ref, woa_ref, wog_ref, n2_ref, wrh_ref, wrl_ref, br_ref,
                       h_ref, hn_ref, re_ref, rg_ref):
    h = x_ref[...] + _dot(att_ref[...], woa_ref[...]) + _dot(gla_ref[...], wog_ref[...])
    h_ref[...] = h
    hn = h * lax.rsqrt(jnp.mean(h * h, axis=-1, keepdims=True) + RMS_EPS) * n2_ref[...]
    pieces = hn.shape[1] // LANES
    for c in range(pieces):
        hn_ref[pl.ds(c, hn.shape[0], stride=pieces), :] = hn[:, c * LANES:(c + 1) * LANES]
    hn_hi = hn.astype(BF16)
    hn_lo = (hn - hn_hi.astype(F32)).astype(BF16)
    logits = (_dot(hn_hi, wrh_ref[...]) + _dot(hn_hi, wrl_ref[...]) + _dot(hn_lo, wrh_ref[...])) + br_ref[...]
    col = lax.broadcasted_iota(I32, logits.shape, 1)
    big = jnp.int32(2 ** 30)
    in_g = col < N_GROUPS
    lg = jnp.where(in_g, logits, -jnp.inf)
    m1 = jnp.max(lg, axis=-1, keepdims=True)
    g_idx = jnp.min(jnp.where(lg == m1, col, big), axis=-1, keepdims=True)
    g_top = 1.0 / jnp.sum(jnp.exp(lg - m1), axis=-1, keepdims=True)
    ecol = col - N_GROUPS
    in_e = (ecol >= g_idx * EXPERTS_PER_GROUP) & (ecol < (g_idx + 1) * EXPERTS_PER_GROUP)
    le = jnp.where(in_e, logits, -jnp.inf)
    l1 = jnp.max(le, axis=-1, keepdims=True)
    e1 = jnp.min(jnp.where(le == l1, ecol, big), axis=-1, keepdims=True)
    le2 = jnp.where(ecol == e1, -jnp.inf, le)
    l2 = jnp.max(le2, axis=-1, keepdims=True)
    e2 = jnp.min(jnp.where(le2 == l2, ecol, big), axis=-1, keepdims=True)
    w2 = jnp.exp(l2 - l1)
    den = 1.0 + w2
    re_ref[...] = jnp.where(col == 0, e1, jnp.where(col == 1, e2, 0))
    rg_ref[...] = jnp.where(col == 0, g_top / den, jnp.where(col == 1, g_top * w2 / den, 0.0))


def _out_router_into_kernel(*refs):
    _out_router_kernel(*refs[:9], *refs[10:])


def _out_router_alloc_kernel(*refs, n_real):
    i = pl.program_id(0)

    @pl.when(i < n_real)
    def _():
        _out_router_kernel(*refs)

    @pl.when(i >= n_real)
    def _():
        refs[10][...] = jnp.zeros_like(refs[10])


def _out_router(att, gla, x, wo_att, wo_gla, n2, wr_hi, wr_lo, br, tm, n_all, row0, hn_all=None):
    n, d = x.shape
    n_real = n // tm
    row = lambda i: (jnp.minimum(i, n_real - 1), 0)
    fix = lambda i: (0, 0)
    b0 = row0 // tm
    if hn_all is None:
        assert row0 == 0 and n_all % tm == 0
        body, steps = functools.partial(_out_router_alloc_kernel, n_real=n_real), n_all // tm
        extra_in, extra_args, alias = [], [], {}
    else:
        body, steps = _out_router_into_kernel, n_real
        extra_in, extra_args, alias = [pl.BlockSpec(memory_space=pl.ANY)], [hn_all], {9: 1}
    return pl.pallas_call(
        body,
        grid=(steps,),
        input_output_aliases=alias,
        in_specs=[
            pl.BlockSpec((tm, ATT_WIDTH), row),
            pl.BlockSpec((tm, GLA_VW), row),
            pl.BlockSpec((tm, d), row),
            pl.BlockSpec((ATT_WIDTH, d), fix),
            pl.BlockSpec((GLA_VW, d), fix),
            pl.BlockSpec((1, d), fix),
            pl.BlockSpec((d, LANES), fix),
            pl.BlockSpec((d, LANES), fix),
            pl.BlockSpec((1, LANES), fix),
        ] + extra_in,
        out_specs=[
            pl.BlockSpec((tm, d), row),
            pl.BlockSpec((tm * (d // LANES), LANES), lambda i: (b0 + i, 0)),
            pl.BlockSpec((tm, LANES), row),
            pl.BlockSpec((tm, LANES), row),
        ],
        out_shape=[
            jax.ShapeDtypeStruct((n, d), F32),
            jax.ShapeDtypeStruct((n_all * (d // LANES), LANES), F32),
            jax.ShapeDtypeStruct((n, LANES), I32),
            jax.ShapeDtypeStruct((n, LANES), F32),
        ],
        compiler_params=_params("arbitrary"),
        name="out_router",
    )(att, gla, x, wo_att, wo_gla, n2, wr_hi, wr_lo, br, *extra_args)


def _expert_kernel(be_ref, tok_ref, nused_ref, hn_hbm, wg_ref, wu_ref, wd_ref, o_ref,
                   xbuf, wgb, wub, wdb, sem, *, rows, n_blocks):
    i = pl.program_id(0)

    @pl.when((i == 0) | (be_ref[i] != be_ref[jnp.maximum(i - 1, 0)]))
    def _():
        wgb[...] = wg_ref[0].astype(BF16)
        wub[...] = wu_ref[0].astype(BF16)
        wdb[...] = wd_ref[0].astype(BF16)

    n_used = nused_ref[0]

    pieces = wgb.shape[0] // LANES

    def issue(blk, slot, r, priority):
        src = pl.multiple_of(tok_ref[blk * rows + r] * pieces, pieces)
        pltpu.make_async_copy(hn_hbm.at[pl.ds(src, pieces), :], xbuf.at[slot, pl.ds(r * pieces, pieces), :],
                              sem.at[slot]).start(priority=priority)

    def wait_rows(slot):
        pltpu.make_async_copy(hn_hbm.at[pl.ds(0, rows * pieces), :], xbuf.at[slot], sem.at[slot]).wait()

    @pl.when(i == 0)
    def _():
        def body(g, carry):
            for u in range(DMA_UNROLL):
                issue(0, 0, g * DMA_UNROLL + u, u % 2)
            return carry
        lax.fori_loop(0, rows // DMA_UNROLL, body, 0)

    slot = i % 2

    @pl.when(i < n_used)
    def _():
        wait_rows(slot)
        xs = xbuf.at[slot]
        x = jnp.concatenate([xs[pl.ds(c, rows, stride=pieces), :] for c in range(pieces)], axis=1).astype(BF16)
        for r in range(rows):
            issue(i + 1, 1 - slot, r, r % 2)
        a = _dot(x, wgb[...])
        u = _dot(x, wub[...])
        act = (a * jax.nn.sigmoid(a) * u).astype(BF16)
        y = _dot(act, wdb[...])
        for c in range(pieces):
            o_ref[pl.ds(c, rows, stride=pieces), :] = y[:, c * LANES:(c + 1) * LANES]

    @pl.when(i == n_used)
    def _():
        wait_rows(slot)

    @pl.when(i >= n_used)
    def _():
        o_ref[...] = jnp.zeros_like(o_ref)


def _experts(block_expert, slot_tok, n_used, hn, wg, wu, wd):
    n_blocks = block_expert.shape[0]
    rows = EXPERT_ROWS
    d = wg.shape[1]
    f = wg.shape[2]
    pieces = d // LANES
    grid_spec = pltpu.PrefetchScalarGridSpec(
        num_scalar_prefetch=3,
        grid=(n_blocks,),
        in_specs=[
            pl.BlockSpec(memory_space=pl.ANY),
            pl.BlockSpec((1, d, f), lambda i, be, tk, nu: (be[i], 0, 0)),
            pl.BlockSpec((1, d, f), lambda i, be, tk, nu: (be[i], 0, 0)),
            pl.BlockSpec((1, f, d), lambda i, be, tk, nu: (be[i], 0, 0)),
        ],
        out_specs=pl.BlockSpec((rows * pieces, LANES), lambda i, be, tk, nu: (i, 0)),
        scratch_shapes=[
            pltpu.VMEM((2, rows * pieces, LANES), F32),
            pltpu.VMEM((d, f), BF16),
            pltpu.VMEM((d, f), BF16),
            pltpu.VMEM((f, d), BF16),
            pltpu.SemaphoreType.DMA((2,)),
        ],
    )
    return pl.pallas_call(
        functools.partial(_expert_kernel, rows=rows, n_blocks=n_blocks),
        grid_spec=grid_spec,
        out_shape=jax.ShapeDtypeStruct((n_blocks * rows * pieces, LANES), F32),
        compiler_params=_params("arbitrary"),
        name="experts",
    )(block_expert, slot_tok, n_used, hn, wg, wu, wd)


def _combine_kernel(pos_ref, h_ref, rg_ref, ys_hbm, y_ref, buf, sem, *, rows, n_steps, tok0):
    i = pl.program_id(0)
    pieces = h_ref.shape[1] // LANES

    def start(step, slot):
        def body(g, carry):
            for u in range(DMA_UNROLL):
                r = g * DMA_UNROLL + u
                for k in range(EXPERT_TOPK):
                    p = pl.multiple_of(pos_ref[(tok0 + step * rows + r) * EXPERT_TOPK + k] * pieces, pieces)
                    pltpu.make_async_copy(ys_hbm.at[pl.ds(p, pieces), :],
                                          buf.at[slot, k, pl.ds(r * pieces, pieces), :],
                                          sem.at[slot]).start(priority=k % 2)
            return carry
        lax.fori_loop(0, rows // DMA_UNROLL, body, 0)

    @pl.when(i == 0)
    def _():
        start(0, 0)

    @pl.when(i + 1 < n_steps)
    def _():
        start(i + 1, (i + 1) % 2)

    slot = i % 2

    for k in range(EXPERT_TOPK):
        pltpu.make_async_copy(ys_hbm.at[pl.ds(0, rows * pieces), :], buf.at[slot, k], sem.at[slot]).wait()

    rg = rg_ref[...]
    y = h_ref[...]
    for k in range(EXPERT_TOPK):
        bk = buf.at[slot, k]
        yk = jnp.concatenate([bk[pl.ds(c, rows, stride=pieces), :] for c in range(pieces)], axis=1)
        y = y + yk * rg[:, k:k + 1]
    y_ref[...] = y


def _combine(pos, h, rg, ys, tok0, rows):
    n_tok, d = h.shape
    n_steps = n_tok // rows
    grid_spec = pltpu.PrefetchScalarGridSpec(
        num_scalar_prefetch=1,
        grid=(n_steps,),
        in_specs=[
            pl.BlockSpec((rows, d), lambda i, p: (i, 0)),
            pl.BlockSpec((rows, LANES), lambda i, p: (i, 0)),
            pl.BlockSpec(memory_space=pl.ANY),
        ],
        out_specs=pl.BlockSpec((rows, d), lambda i, p: (i, 0)),
        scratch_shapes=[
            pltpu.VMEM((2, EXPERT_TOPK, rows * (d // LANES), LANES), F32),
            pltpu.SemaphoreType.DMA((2,)),
        ],
    )
    return pl.pallas_call(
        functools.partial(_combine_kernel, rows=rows, n_steps=n_steps, tok0=tok0),
        grid_spec=grid_spec,
        out_shape=jax.ShapeDtypeStruct((n_tok, d), F32),
        compiler_params=_params("arbitrary"),
        name="combine",
    )(pos, h, rg, ys)


def _dispatch(expert):
    a = expert.shape[0]
    rows = EXPERT_ROWS
    onehot = (expert[:, None] == jnp.arange(N_EXPERTS, dtype=I32)[None, :]).astype(I32)
    csum = jnp.cumsum(onehot, axis=0)
    counts = csum[-1]
    rank = jnp.take_along_axis(csum, expert[:, None], axis=1)[:, 0] - 1
    padded = (counts + rows - 1) // rows * rows
    ends_p = jnp.cumsum(padded)
    pstart = ends_p - padded
    pos = (pstart[expert] + rank).astype(I32)
    n_blocks = -(-a // rows) + N_EXPERTS + 1
    slot_tok = jnp.zeros((n_blocks * rows,), I32).at[pos].set(jnp.arange(a, dtype=I32) // EXPERT_TOPK)
    block_start = jnp.arange(n_blocks, dtype=I32) * rows
    block_expert = jnp.minimum(
        jnp.sum((ends_p[None, :] <= block_start[:, None]).astype(I32), axis=1), N_EXPERTS - 1).astype(I32)
    n_used = (ends_p[-1:] // rows).astype(I32)
    return pos, slot_tok, block_expert, n_used


def kernel(x_prompt, x_sample, cache_k, cache_v, state_gla, page_table, norm1_w, w_in, q_norm_w, k_norm_w,
           w_gla_a2, b_gla_a, gla_norm_w, w_out, norm2_w, w_r1, b_r1, w_r2, b_r2, w_e_gate, w_e_up, w_e_down):
    depth = w_in.shape[0]
    assert depth == 1
    batch, seq, d = x_prompt.shape
    db, nq, _ = x_sample.shape
    n_pages = page_table.shape[1]
    past = n_pages * PAGE_SIZE
    assert past % MOBA_BLOCK == 0 and nq <= GLA_CHUNK and seq % (8 * MOBA_BLOCK) == 0
    n_past_blocks = past // MOBA_BLOCK
    assert n_past_blocks + 8 <= LANES
    n_p, n_s = batch * seq, db * nq
    l = 0

    w = w_in[l]
    o_gq = 3 * ATT_WIDTH
    o_gv = o_gq + 2 * GLA_KW
    o_ga = o_gv + GLA_VW
    o_gg = o_ga + GLA_GATE_RANK
    w_main = jnp.concatenate([w[:, :o_ga], w[:, o_gg:]], axis=1).astype(BF16)
    w_ga = jnp.pad(w[:, o_ga:o_gg], ((0, 0), (0, LANES - GLA_GATE_RANK))).astype(BF16)
    n1 = norm1_w[l][None, :]
    qn = q_norm_w[l][None, :]
    kn = k_norm_w[l][None, :]
    wa = jnp.pad(w_gla_a2[l], ((0, LANES - GLA_GATE_RANK), (0, 0)))
    wa_hi = wa.astype(BF16)
    wa_lo = (wa - wa_hi.astype(F32)).astype(BF16)
    ba = b_gla_a[l][None, :]
    gnw = gla_norm_w[l][None, :]
    wo = w_out[l].astype(BF16)
    wo_att, wo_gla = wo[:ATT_WIDTH], wo[ATT_WIDTH:]
    n2 = norm2_w[l][None, :]
    wr = jnp.pad(jnp.concatenate([w_r1[l], w_r2[l]], axis=1), ((0, 0), (0, LANES - N_GROUPS - N_EXPERTS)))
    br = jnp.pad(jnp.concatenate([b_r1[l], b_r2[l]]), (0, LANES - N_GROUPS - N_EXPERTS))[None, :]
    slopes = 2.0 ** (-8.0 * jnp.arange(1, N_ATT_HEADS + 1, dtype=F32) / N_ATT_HEADS)

    xp = x_prompt.reshape(n_p, d)
    q_p, k_p, kb_p, v_p, vt_p, gqk_p, gv_p, gg_p, ga_p = _project(xp, n1, w_main, w_ga, qn, kn, 512)
    kmeans = _block_means(k_p)
    idx_p = _prompt_select(q_p, kmeans, seq)
    att_p, block_sums = _moba_prompt(page_table, slopes, q_p, kb_p, vt_p, idx_p, cache_k[l], batch, seq)
    s0_p = jnp.zeros((batch, N_GLA_HEADS, GLA_DK, GLA_DV), F32)
    gla_p, s_p = _gla(gqk_p, gv_p, gg_p, ga_p, wa_hi, wa_lo, ba, gnw, s0_p, batch, seq, seq)

    xs = x_sample.reshape(n_s, d)
    q_s, k_s, _, v_s, _, gqk_s, gv_s, gg_s, ga_s = _project(xs, n1, w_main, w_ga, qn, kn, n_s)
    idx = _sample_select(q_s, block_sums.reshape(db * n_past_blocks, ATT_WIDTH), k_s, db, nq, n_past_blocks)
    att_s = _moba_sample(page_table, idx[..., :MOBA_TOPK], slopes, q_s, k_s, v_s, cache_k[l], cache_v[l], db, nq)

    def pad_seq(a):
        return jnp.pad(a.reshape(db, nq, -1), ((0, 0), (0, GLA_CHUNK - nq), (0, 0))).reshape(db * GLA_CHUNK, -1)

    gla_s, s_s = _gla(pad_seq(gqk_s), pad_seq(gv_s), pad_seq(gg_s), pad_seq(ga_s), wa_hi, wa_lo, ba, gnw,
                      state_gla[l], db, GLA_CHUNK, nq)
    gla_s = gla_s.reshape(db, GLA_CHUNK, GLA_VW)[:, :nq].reshape(n_s, GLA_VW)

    wr_hi = wr.astype(BF16)
    wr_lo = (wr - wr_hi.astype(F32)).astype(BF16)
    n_all = -(-(n_p + n_s) // OUT_ROWS) * OUT_ROWS
    h_p, hn, re_p, rg_p = _out_router(att_p, gla_p, xp, wo_att, wo_gla, n2, wr_hi, wr_lo, br, OUT_ROWS, n_all, 0)
    h_s, hn, re_s, rg_s = _out_router(att_s.astype(BF16), gla_s, xs, wo_att, wo_gla, n2, wr_hi, wr_lo, br,
                                      n_s, n_all, n_p, hn)
    expert = jnp.concatenate([re_p[:, :EXPERT_TOPK], re_s[:, :EXPERT_TOPK]], axis=0).reshape(-1)
    pos, slot_tok, block_expert, n_used = _dispatch(expert)
    ys = _experts(block_expert, slot_tok, n_used, hn, w_e_gate[l], w_e_up[l], w_e_down[l])
    y_p = _combine(pos, h_p, rg_p, ys, 0, 128)
    y_s = _combine(pos, h_s, rg_s, ys, n_p, 128)

    hd = (N_ATT_HEADS, HEAD_DIM)
    return (y_p.reshape(batch, seq, d), y_s.reshape(db, nq, d),
            k_p.reshape(1, batch, seq, *hd), v_p.reshape(1, batch, seq, *hd), s_p[None],
            k_s.reshape(1, db, nq, *hd), v_s.reshape(1, db, nq, *hd), s_s[None])
```

```python
import functools

import jax
import jax.numpy as jnp
from jax import lax
from jax.experimental import pallas as pl
from jax.experimental.pallas import tpu as pltpu

F32 = jnp.float32
BF16 = jnp.bfloat16
I32 = jnp.int32

HEAD_DIM = 128
N_ATT_HEADS = 8
ATT_WIDTH = N_ATT_HEADS * HEAD_DIM
MOBA_BLOCK = 256
MOBA_TOPK = 3
MOBA_HALF = 2
HEADS_PER_STEP = 2
SELECT_ROWS = 2048
SUM_ROWS = 16
LOG2E = 1.4426950408889634
PAGE_SIZE = 128
PAGES_PER_BLOCK = MOBA_BLOCK // PAGE_SIZE
N_GLA_HEADS = 4
GLA_DK = 128
GLA_DV = 256
GLA_KW = N_GLA_HEADS * GLA_DK
GLA_VW = N_GLA_HEADS * GLA_DV
GLA_GATE_RANK = 16
GLA_TAU = 16.0
GLA_CHUNK = 64
GLA_HEADS_PROMPT = 2
N_GROUPS = 4
EXPERTS_PER_GROUP = 8
N_EXPERTS = N_GROUPS * EXPERTS_PER_GROUP
EXPERT_TOPK = 2
RMS_EPS = 1e-6
NEG_INF = -1e30
LANES = 128
EXPERT_ROWS = 256
OUT_ROWS = 512
DMA_UNROLL = 8
VMEM_LIMIT = 56 * 1024 * 1024

_NT = (((1,), (1,)), ((), ()))
_TN = (((0,), (0,)), ((), ()))
_HI = lax.Precision.HIGHEST


def _dot(a, b):
    return jnp.dot(a, b, preferred_element_type=F32)


def _dot_nt(a, b):
    return lax.dot_general(a, b, _NT, preferred_element_type=F32)


def _dot_tn(a, b):
    return lax.dot_general(a, b, _TN, preferred_element_type=F32)


def _params(*sem):
    return pltpu.CompilerParams(dimension_semantics=sem, vmem_limit_bytes=VMEM_LIMIT)


def _top3(gate, col, n_valid, axis=-1):
    picks = []
    g = gate
    for r in range(MOBA_TOPK):
        m = jnp.max(g, axis=axis, keepdims=True)
        idx = jnp.min(jnp.where(g == m, col, jnp.int32(2 ** 30)), axis=axis, keepdims=True)
        g = jnp.where(col == idx, -jnp.inf, g)
        picks.append(jnp.where(r < n_valid, idx, -1))
    return picks


def _proj_kernel(x_ref, n1_ref, w_ref, wga_ref, qn_ref, kn_ref,
                 q_ref, k_ref, kb_ref, v_ref, vt_ref, gqk_ref, gv_ref, gg_ref, ga_ref, xn_ref):
    j = pl.program_id(1)

    @pl.when(j == 0)
    def _():
        x = x_ref[...]
        y = x * lax.rsqrt(jnp.mean(x * x, axis=-1, keepdims=True) + RMS_EPS) * n1_ref[...]
        xn_ref[...] = y.astype(BF16)
        ga_ref[...] = _dot(xn_ref[...], wga_ref[...])

    z = _dot(xn_ref[...], w_ref[...])

    def head_norm(w):
        outs = []
        for h in range(N_ATT_HEADS):
            zh = z[:, h * HEAD_DIM:(h + 1) * HEAD_DIM]
            outs.append(zh * lax.rsqrt(jnp.mean(zh * zh, axis=-1, keepdims=True) + RMS_EPS) * w)
        return outs

    @pl.when(j == 0)
    def _():
        for h, y in enumerate(head_norm(qn_ref[...])):
            q_ref[:, h * HEAD_DIM:(h + 1) * HEAD_DIM] = y

    @pl.when(j == 1)
    def _():
        for h, y in enumerate(head_norm(kn_ref[...])):
            k_ref[:, h * HEAD_DIM:(h + 1) * HEAD_DIM] = y
            kb_ref[:, h * HEAD_DIM:(h + 1) * HEAD_DIM] = y.astype(BF16)

    @pl.when(j == 2)
    def _():
        v_ref[...] = z
        zt = z.T.astype(BF16)
        for c in range(vt_ref.shape[0]):
            vt_ref[c] = zt[:, c * MOBA_BLOCK:(c + 1) * MOBA_BLOCK]

    @pl.when(j == 3)
    def _():
        gqk_ref[...] = z

    @pl.when(j == 4)
    def _():
        gv_ref[...] = z

    @pl.when(j == 5)
    def _():
        gg_ref[...] = z


def _project(x, n1, w_main, w_ga, qn, kn, tm):
    n, d = x.shape
    wide = ATT_WIDTH
    row = lambda i, j: (i, 0)
    out_shape = [
        jax.ShapeDtypeStruct((n, wide), F32),
        jax.ShapeDtypeStruct((n, wide), F32),
        jax.ShapeDtypeStruct((n, wide), BF16),
        jax.ShapeDtypeStruct((n, wide), F32),
        jax.ShapeDtypeStruct((n // MOBA_BLOCK, wide, MOBA_BLOCK), BF16),
        jax.ShapeDtypeStruct((n, wide), F32),
        jax.ShapeDtypeStruct((n, wide), F32),
        jax.ShapeDtypeStruct((n, wide), F32),
        jax.ShapeDtypeStruct((n, LANES), F32),
    ]
    out_specs = [pl.BlockSpec((tm, s.shape[1]), row) if len(s.shape) == 2
                 else pl.BlockSpec((tm // MOBA_BLOCK, wide, MOBA_BLOCK), lambda i, j: (i, 0, 0)) for s in out_shape]
    return pl.pallas_call(
        _proj_kernel,
        grid=(n // tm, 6),
        in_specs=[
            pl.BlockSpec((tm, d), row),
            pl.BlockSpec((1, d), lambda i, j: (0, 0)),
            pl.BlockSpec((d, wide), lambda i, j: (0, j)),
            pl.BlockSpec((d, LANES), lambda i, j: (0, 0)),
            pl.BlockSpec((1, HEAD_DIM), lambda i, j: (0, 0)),
            pl.BlockSpec((1, HEAD_DIM), lambda i, j: (0, 0)),
        ],
        out_specs=out_specs,
        out_shape=out_shape,
        scratch_shapes=[pltpu.VMEM((tm, d), BF16)],
        compiler_params=_params("parallel", "arbitrary"),
        name="projection",
    )(x, n1, w_main, w_ga, qn, kn)


def _kmeans_kernel(k_ref, o_ref):
    rows = k_ref.shape[0]
    k = k_ref[...].reshape(rows // MOBA_BLOCK, MOBA_BLOCK, k_ref.shape[1])
    o_ref[...] = jnp.sum(k, axis=1) * (1.0 / MOBA_BLOCK)


def _block_means(k):
    n, w = k.shape
    rows = 8 * MOBA_BLOCK
    return pl.pallas_call(
        _kmeans_kernel,
        grid=(n // rows,),
        in_specs=[pl.BlockSpec((rows, w), lambda i: (i, 0))],
        out_specs=pl.BlockSpec((8, w), lambda i: (i, 0)),
        out_shape=jax.ShapeDtypeStruct((n // MOBA_BLOCK, w), F32),
        compiler_params=_params("parallel"),
        name="block_means",
    )(k)


def _prompt_select_kernel(q_ref, km_ref, o_ref, *, tiles_per_seq):
    c = pl.program_id(0)
    rows = q_ref.shape[0]
    nb = km_ref.shape[0]
    q = q_ref[...]
    km = km_ref[...]
    q_hi = q.astype(BF16)
    q_lo = (q - q_hi.astype(F32)).astype(BF16)
    km_hi = km.astype(BF16)
    km_lo = (km - km_hi.astype(F32)).astype(BF16)
    gate = _dot_nt(km_hi, q_hi) + _dot_nt(km_lo, q_hi) + _dot_nt(km_hi, q_lo)
    blk = lax.broadcasted_iota(I32, (nb, rows), 0)
    n_full = (c % tiles_per_seq) * (rows // MOBA_BLOCK) + lax.broadcasted_iota(I32, (1, rows), 1) // MOBA_BLOCK
    gate = jnp.where(blk < n_full, gate, NEG_INF)
    i0, i1, i2 = _top3(gate, blk, n_full, axis=0)
    row = lax.broadcasted_iota(I32, (8, rows), 0)
    o_ref[...] = jnp.where(row == 0, i0, jnp.where(row == 1, i1, jnp.where(row == 2, i2, -1)))


def _prompt_select(q, kmeans, seq):
    n = q.shape[0]
    rows = SELECT_ROWS
    nb = seq // MOBA_BLOCK
    tiles_per_seq = seq // rows
    return pl.pallas_call(
        functools.partial(_prompt_select_kernel, tiles_per_seq=tiles_per_seq),
        grid=(n // rows, N_ATT_HEADS),
        in_specs=[
            pl.BlockSpec((rows, HEAD_DIM), lambda c, h: (c, h)),
            pl.BlockSpec((nb, HEAD_DIM), lambda c, h: (c // tiles_per_seq, h)),
        ],
        out_specs=pl.BlockSpec((8, rows), lambda c, h: (h, c)),
        out_shape=jax.ShapeDtypeStruct((N_ATT_HEADS * 8, n), I32),
        compiler_params=_params("parallel", "parallel"),
        name="prompt_select",
    )(q, kmeans)


def _moba_prompt_kernel(pt_ref, slopes_ref, q_ref, k_ref, vt_ref, idx_ref, kc_hbm, o_ref, ps_ref,
                        sa_scr, sb_scr, bias_scr, pbuf, psem, *, group, n_steps):
    hg = pl.program_id(1)
    i = pl.program_id(2)
    bs = MOBA_BLOCK
    nb = vt_ref.shape[0]
    t = (pl.program_id(0) * pl.num_programs(1) + hg) * nb + i

    def page_copy(tt, slot, g):
        return pltpu.make_async_copy(kc_hbm.at[pt_ref[tt * group + g]], pbuf.at[slot, g], psem.at[slot])

    @pl.when(t == 0)
    def _():
        for g in range(group):
            page_copy(0, 0, g).start()

    @pl.when(t + 1 < n_steps)
    def _():
        for g in range(group):
            page_copy(t + 1, (t + 1) % 2, g).start()

    pslot = t % 2
    for g in range(group):
        page_copy(t, pslot, g).wait()
    for bk in range(group // PAGES_PER_BLOCK):
        acc = jnp.sum(pbuf[pslot, PAGES_PER_BLOCK * bk], axis=0)
        for p in range(1, PAGES_PER_BLOCK):
            acc = acc + jnp.sum(pbuf[pslot, PAGES_PER_BLOCK * bk + p], axis=0)
        ps_ref[bk] = acc

    krow = lax.broadcasted_iota(I32, (bs, bs), 0)
    qcol = lax.broadcasted_iota(I32, (bs, bs), 1)
    ones_rows = jnp.ones((SUM_ROWS, bs), BF16)
    hb = MOBA_HALF

    def head_job(hh):
        lanes = slice(hh * HEAD_DIM, (hh + 1) * HEAD_DIM)
        slope2 = slopes_ref[hg * HEADS_PER_STEP + hh] * LOG2E
        q2 = (q_ref[:, lanes] * (HEAD_DIM ** -0.5 * LOG2E)).astype(BF16)
        i0 = idx_ref[8 * hh:8 * hh + 1, :]
        i1 = idx_ref[8 * hh + 1:8 * hh + 2, :]
        i2 = idx_ref[8 * hh + 2:8 * hh + 3, :]
        bias_ref, sa_ref, sb_ref = bias_scr.at[hh], sa_scr.at[hh], sb_scr.at[hh]
        bias_ref[...] = slope2 * krow.astype(F32)

        def pv_dot(jc, p):
            return _dot(jnp.concatenate([vt_ref[jc, lanes, :], ones_rows], axis=0), p.astype(BF16))

        def selected(j):
            return (i0 == j) | (i1 == j) | (i2 == j)

        def block_offset(j):
            return slope2 * ((j - i) * bs).astype(F32)

        def sweep1(jb, s_ref):
            r0 = pl.multiple_of(jnp.minimum(jb, nb - hb) * bs, bs)
            s = _dot_nt(k_ref[pl.ds(r0, hb * bs), lanes], q2)
            cmax = jnp.full((1, bs), NEG_INF, F32)
            for u in range(hb):
                su = s[u * bs:(u + 1) * bs] + bias_ref[...]
                s_ref[u] = su
                cmax = jnp.maximum(cmax, jnp.where(selected(jb + u),
                                                   jnp.max(su, axis=0, keepdims=True) + block_offset(jb + u),
                                                   NEG_INF))
            return cmax

        def sweep2(jb, s_ref, m_prev, m_cur, l, acc):
            pv = jnp.zeros((HEAD_DIM + SUM_ROWS, bs), F32)
            for u in range(hb):
                j = jb + u
                ref = jnp.where(selected(j), m_cur - block_offset(j), -NEG_INF)
                pv = pv + pv_dot(jnp.minimum(j, nb - 1), jnp.exp2(s_ref[u] - ref))
            alpha = jnp.exp2(m_prev - m_cur)
            return alpha * l + pv[HEAD_DIM:HEAD_DIM + 1], alpha * acc + pv[:HEAD_DIM]

        def prologue():
            r_own = pl.multiple_of(i * bs, bs)
            s = jnp.where(krow <= qcol, _dot_nt(k_ref[pl.ds(r_own, bs), lanes], q2) + bias_ref[...], NEG_INF)
            m = jnp.max(s, axis=0, keepdims=True)
            pv = pv_dot(i, jnp.exp2(s - m))
            m_a = jnp.maximum(m, sweep1(0, sa_ref))
            m_b = jnp.maximum(m_a, sweep1(hb, sb_ref))
            return m, m_a, m_b, pv[HEAD_DIM:HEAD_DIM + 1], pv[:HEAD_DIM]

        def chunk(c, carry):
            m_prev, m_a, m_b, l, acc = carry
            jb = c * (2 * hb)
            l, acc = sweep2(jb, sa_ref, m_prev, m_a, l, acc)
            m_a2 = jnp.maximum(m_b, sweep1(jb + 2 * hb, sa_ref))
            l, acc = sweep2(jb + hb, sb_ref, m_a, m_b, l, acc)
            m_b2 = jnp.maximum(m_a2, sweep1(jb + 3 * hb, sb_ref))
            return m_b, m_a2, m_b2, l, acc

        def finish(carry):
            _, _, _, l, acc = carry
            o_ref[:, lanes] = (acc / l).T.astype(o_ref.dtype)

        return prologue, chunk, finish

    jobs = [head_job(hh) for hh in range(HEADS_PER_STEP)]
    carries = tuple(job[0]() for job in jobs)
    n_chunks = (i + 2 * hb - 1) // (2 * hb)
    carries = lax.fori_loop(0, n_chunks, lambda c, cs: tuple(job[1](c, cr) for job, cr in zip(jobs, cs)), carries)
    for job, cr in zip(jobs, carries):
        job[2](cr)


def _moba_prompt(page_table, slopes, q, kb, vt, idx, cache_k, batch, seq):
    n = q.shape[0]
    nb = seq // MOBA_BLOCK
    bs = MOBA_BLOCK
    hps = HEADS_PER_STEP
    n_hg = N_ATT_HEADS // hps
    n_steps = batch * n_hg * nb
    n_pages_total = page_table.size
    assert n_pages_total % (n_steps * PAGES_PER_BLOCK) == 0
    group = n_pages_total // n_steps
    bpc = group // PAGES_PER_BLOCK
    grid_spec = pltpu.PrefetchScalarGridSpec(
        num_scalar_prefetch=1,
        grid=(batch, n_hg, nb),
        in_specs=[
            pl.BlockSpec(memory_space=pltpu.SMEM),
            pl.BlockSpec((bs, hps * HEAD_DIM), lambda b, h, i, pt: (b * nb + i, h)),
            pl.BlockSpec((seq, hps * HEAD_DIM), lambda b, h, i, pt: (b, h)),
            pl.BlockSpec((nb, hps * HEAD_DIM, bs), lambda b, h, i, pt: (b, h, 0)),
            pl.BlockSpec((8 * hps, bs), lambda b, h, i, pt: (h, b * nb + i)),
            pl.BlockSpec(memory_space=pl.ANY),
        ],
        out_specs=[
            pl.BlockSpec((bs, hps * HEAD_DIM), lambda b, h, i, pt: (b * nb + i, h)),
            pl.BlockSpec((bpc, N_ATT_HEADS, HEAD_DIM), lambda b, h, i, pt: ((b * n_hg + h) * nb + i, 0, 0)),
        ],
        scratch_shapes=[
            pltpu.VMEM((hps, MOBA_HALF, bs, bs), F32),
            pltpu.VMEM((hps, MOBA_HALF, bs, bs), F32),
            pltpu.VMEM((hps, bs, bs), F32),
            pltpu.VMEM((2, group, PAGE_SIZE, N_ATT_HEADS, HEAD_DIM), F32),
            pltpu.SemaphoreType.DMA((2,)),
        ],
    )
    return pl.pallas_call(
        functools.partial(_moba_prompt_kernel, group=group, n_steps=n_steps),
        grid_spec=grid_spec,
        out_shape=[
            jax.ShapeDtypeStruct((n, ATT_WIDTH), BF16),
            jax.ShapeDtypeStruct((n_steps * bpc, N_ATT_HEADS, HEAD_DIM), F32),
        ],
        compiler_params=_params("arbitrary", "arbitrary", "arbitrary"),
        name="moba_prompt",
    )(page_table.reshape(-1), slopes, q, kb, vt, idx, cache_k)


def _sample_select_kernel(q_ref, bs_ref, kn_ref, o_ref, *, n_past_blocks, past):
    nq = q_ref.shape[0]
    col = lax.broadcasted_iota(I32, (nq, LANES), 1)
    lane = col
    n_full = (past + lax.broadcasted_iota(I32, (nq, 1), 0)) // MOBA_BLOCK
    for h in range(N_ATT_HEADS):
        sl = slice(h * HEAD_DIM, (h + 1) * HEAD_DIM)
        m_past = bs_ref[:, sl] * (1.0 / MOBA_BLOCK)
        m_new = jnp.sum(kn_ref[:, sl], axis=0, keepdims=True) * (1.0 / MOBA_BLOCK)
        row8 = lax.broadcasted_iota(I32, (8, HEAD_DIM), 0)
        new_rows = jnp.where(row8 == 0, jnp.broadcast_to(m_new, (8, HEAD_DIM)), 0.0)
        means = jnp.concatenate(
            [m_past, new_rows, jnp.zeros((LANES - n_past_blocks - 8, HEAD_DIM), F32)], axis=0)
        gate = lax.dot_general(q_ref[:, sl], means, _NT, precision=_HI, preferred_element_type=F32)
        gate = jnp.where(col < n_full, gate, NEG_INF)
        i0, i1, i2 = _top3(gate, col, n_full)
        o_ref[0, h] = jnp.where(lane == 0, i0, jnp.where(lane == 1, i1, jnp.where(lane == 2, i2, 0)))


def _sample_select(q, block_sums, k_new, db, nq, n_past_blocks):
    return pl.pallas_call(
        functools.partial(_sample_select_kernel, n_past_blocks=n_past_blocks,
                          past=n_past_blocks * MOBA_BLOCK),
        grid=(db,),
        in_specs=[
            pl.BlockSpec((nq, ATT_WIDTH), lambda s: (s, 0)),
            pl.BlockSpec((n_past_blocks, ATT_WIDTH), lambda s: (s, 0)),
            pl.BlockSpec((nq, ATT_WIDTH), lambda s: (s, 0)),
        ],
        out_specs=pl.BlockSpec((1, N_ATT_HEADS, nq, LANES), lambda s: (s, 0, 0, 0)),
        out_shape=jax.ShapeDtypeStruct((db, N_ATT_HEADS, nq, LANES), I32),
        compiler_params=_params("parallel"),
        name="sample_select",
    )(q, block_sums, k_new)


def _moba_sample_kernel(pt_ref, idx_ref, slopes_ref, q_ref, kn_ref, vn_ref, kc_hbm, vc_hbm, o_ref,
                        kbuf, vbuf, kown, vown, sem, *, n_steps, n_pages, nq, past):
    t = pl.program_id(0)
    nh = N_ATT_HEADS
    n_sel = nq * MOBA_TOPK
    n_slots = n_sel * PAGES_PER_BLOCK

    def copies(tt, slot, g):
        s = tt // nh
        h = tt % nh
        blk = idx_ref[tt * n_sel + g // PAGES_PER_BLOCK]
        page = pt_ref[s * n_pages + blk * PAGES_PER_BLOCK + g % PAGES_PER_BLOCK]
        return (pltpu.make_async_copy(kc_hbm.at[page, :, h, :], kbuf.at[slot, g], sem.at[0, slot]),
                pltpu.make_async_copy(vc_hbm.at[page, :, h, :], vbuf.at[slot, g], sem.at[1, slot]))

    def start(tt, slot):
        for g in range(n_slots):
            ck, cv = copies(tt, slot, g)
            ck.start()
            cv.start()

    @pl.when(t == 0)
    def _():
        start(0, 0)
        kown[...] = jnp.zeros_like(kown)
        vown[...] = jnp.zeros_like(vown)

    @pl.when(t + 1 < n_steps)
    def _():
        start(t + 1, (t + 1) % 2)

    slot = t % 2
    h = t % nh
    slope = slopes_ref[h]
    scale = HEAD_DIM ** -0.5
    qb = q_ref[...].astype(BF16)
    n_keys = n_slots * PAGE_SIZE
    keys_per_q = MOBA_TOPK * MOBA_BLOCK

    kown[0:nq, :] = kn_ref[...]
    vown[0:nq, :] = vn_ref[...]
    row = lax.broadcasted_iota(I32, (nq, LANES), 0)
    colo = lax.broadcasted_iota(I32, (nq, LANES), 1)
    s_own = _dot_nt(qb, kown[...].astype(BF16)) * scale - slope * (row - colo).astype(F32)
    s_own = jnp.where(colo <= row, s_own, NEG_INF)

    colk = lax.broadcasted_iota(I32, (1, n_keys), 1)
    grp = colk // MOBA_BLOCK
    blk_of_col = jnp.zeros((1, n_keys), I32)
    for g in range(n_sel):
        blk_of_col = jnp.where(grp == g, idx_ref[t * n_sel + g], blk_of_col)
    pos = blk_of_col * MOBA_BLOCK + colk % MOBA_BLOCK
    rowk = lax.broadcasted_iota(I32, (nq, n_keys), 0)
    mine = (lax.broadcasted_iota(I32, (nq, n_keys), 1) // keys_per_q) == rowk
    n_full = (past + rowk) // MOBA_BLOCK
    pick = (lax.broadcasted_iota(I32, (nq, n_keys), 1) // MOBA_BLOCK) % MOBA_TOPK
    valid = mine & (pick < n_full)

    for g in range(n_slots):
        ck, cv = copies(t, slot, g)
        ck.wait()
        cv.wait()

    kall = kbuf[slot].reshape(n_keys, HEAD_DIM).astype(BF16)
    s_sel = _dot_nt(qb, kall) * scale - slope * ((past + rowk) - pos).astype(F32)
    s_sel = jnp.where(valid, s_sel, NEG_INF)
    m = jnp.maximum(jnp.max(s_sel, axis=-1, keepdims=True), jnp.max(s_own, axis=-1, keepdims=True))
    p_sel = jnp.exp(s_sel - m)
    p_own = jnp.exp(s_own - m)
    l = jnp.sum(p_sel, axis=-1, keepdims=True) + jnp.sum(p_own, axis=-1, keepdims=True)
    vall = vbuf[slot].reshape(n_keys, HEAD_DIM).astype(BF16)
    o = _dot(p_sel.astype(BF16), vall) + _dot(p_own.astype(BF16), vown[...].astype(BF16))
    o_ref[...] = o / l


def _moba_sample(page_table, idx, slopes, q, k_new, v_new, cache_k, cache_v, db, nq):
    n_pages = page_table.shape[1]
    n_steps = db * N_ATT_HEADS
    n_slots = nq * MOBA_TOPK * PAGES_PER_BLOCK
    qspec = pl.BlockSpec((nq, HEAD_DIM), lambda t, pt, ix: (t // N_ATT_HEADS, t % N_ATT_HEADS))
    grid_spec = pltpu.PrefetchScalarGridSpec(
        num_scalar_prefetch=2,
        grid=(n_steps,),
        in_specs=[
            pl.BlockSpec(memory_space=pltpu.SMEM),
            qspec, qspec, qspec,
            pl.BlockSpec(memory_space=pl.ANY),
            pl.BlockSpec(memory_space=pl.ANY),
        ],
        out_specs=qspec,
        scratch_shapes=[
            pltpu.VMEM((2, n_slots, PAGE_SIZE, HEAD_DIM), F32),
            pltpu.VMEM((2, n_slots, PAGE_SIZE, HEAD_DIM), F32),
            pltpu.VMEM((LANES, HEAD_DIM), F32),
            pltpu.VMEM((LANES, HEAD_DIM), F32),
            pltpu.SemaphoreType.DMA((2, 2)),
        ],
    )
    return pl.pallas_call(
        functools.partial(_moba_sample_kernel, n_steps=n_steps, n_pages=n_pages, nq=nq,
                          past=n_pages * PAGE_SIZE),
        grid_spec=grid_spec,
        out_shape=jax.ShapeDtypeStruct((db * nq, ATT_WIDTH), F32),
        compiler_params=_params("arbitrary"),
        name="moba_sample",
    )(page_table.reshape(-1), idx.reshape(-1), slopes, q, k_new, v_new, cache_k, cache_v)


def _gla_kernel(gq_ref, gk_ref, gv_ref, gg_ref, ga_ref, wah_ref, wal_ref, ba_ref, nw_ref, s0_ref,
                o_ref, sfin_ref, s_scr, qd_scr, oi_scr, ds_scr, dec_scr, *, chunk, n_sub, t_valid, hps):
    t = pl.program_id(2)

    @pl.when(t == 0)
    def _():
        s_scr[...] = s0_ref[0]

    c = chunk
    tri = lax.broadcasted_iota(I32, (c, c), 0) >= lax.broadcasted_iota(I32, (c, c), 1)
    tri_b = jnp.where(tri, 1.0, 0.0).astype(BF16)
    ones_b = jnp.ones((c, GLA_DK), BF16)
    nw = nw_ref[...]

    def split3(x):
        hi = x.astype(BF16)
        r = x - hi.astype(F32)
        mid = r.astype(BF16)
        return hi, mid, (r - mid.astype(F32)).astype(BF16)

    ga = ga_ref[...]
    ga_hi = ga.astype(BF16)
    ga_lo = (ga - ga_hi.astype(F32)).astype(BF16)

    for hh in range(hps):
        kk = slice(hh * GLA_DK, (hh + 1) * GLA_DK)
        vv = slice(hh * GLA_DV, (hh + 1) * GLA_DV)
        wah, wal = wah_ref[:, kk], wal_ref[:, kk]
        pre = _dot(ga_hi, wah) + _dot(ga_hi, wal) + _dot(ga_lo, wah) + ba_ref[:, kk]
        la = (jnp.minimum(pre, 0.0) - jnp.log1p(jnp.exp(-jnp.abs(pre)))) * (1.0 / GLA_TAU)
        if t_valid < c * n_sub:
            la = jnp.where(lax.broadcasted_iota(I32, la.shape, 0) < t_valid, la, 0.0)

        for ci in range(n_sub):
            sl = slice(ci * c, (ci + 1) * c)
            parts = split3(la[sl])
            b = _dot(tri_b, parts[0]) + _dot(tri_b, parts[1]) + _dot(tri_b, parts[2])
            b_end = b[c - 1:c, :]
            b_end_col = _dot_tn(parts[0], ones_b) + _dot_tn(parts[1], ones_b) + _dot_tn(parts[2], ones_b)
            q = gq_ref[sl, kk] * (GLA_DK ** -0.5)
            k = gk_ref[sl, kk]
            v = gv_ref[sl, vv].astype(BF16)
            q_dec = (q * jnp.exp(b)).astype(BF16)
            k_inv = (k * jnp.exp(-b)).astype(BF16)
            k_end = (k * jnp.exp(b_end - b)).astype(BF16)
            a = jnp.where(tri, _dot_nt(q_dec, k_inv), 0.0)
            qd_scr[hh, sl, :] = q_dec
            oi_scr[hh, sl, :] = _dot(a.astype(BF16), v)
            ds_scr[hh, ci] = _dot_tn(k_end, v)
            dec_scr[hh, ci] = jnp.exp(b_end_col)

    for hh in range(hps):
        vv = slice(hh * GLA_DV, (hh + 1) * GLA_DV)
        state = s_scr[hh]
        for ci in range(n_sub):
            sl = slice(ci * c, (ci + 1) * c)
            o = oi_scr[hh, sl, :] + _dot(qd_scr[hh, sl, :], state.astype(BF16))
            state = jnp.concatenate([dec_scr[hh, ci]] * (GLA_DV // GLA_DK), axis=1) * state + ds_scr[hh, ci]
            on = o * lax.rsqrt(jnp.mean(o * o, axis=-1, keepdims=True) + RMS_EPS) * nw
            g = gg_ref[sl, vv]
            o_ref[sl, vv] = (on * (g * jax.nn.sigmoid(g))).astype(o_ref.dtype)
        s_scr[hh] = state

    @pl.when(t == pl.num_programs(2) - 1)
    def _():
        sfin_ref[0] = s_scr[...]


def _gla(gqk, gv, gg, ga, wa_hi, wa_lo, ba, nw, s0, batch, seq, t_valid, hps):
    n = gqk.shape[0]
    c = GLA_CHUNK
    n_sub = min(8, seq // c)
    rows = c * n_sub
    steps = seq // rows
    ng = N_GLA_HEADS // hps
    return pl.pallas_call(
        functools.partial(_gla_kernel, chunk=c, n_sub=n_sub, t_valid=t_valid, hps=hps),
        grid=(batch, ng, steps),
        in_specs=[
            pl.BlockSpec((rows, hps * GLA_DK), lambda b, h, t: (b * steps + t, h)),
            pl.BlockSpec((rows, hps * GLA_DK), lambda b, h, t: (b * steps + t, ng + h)),
            pl.BlockSpec((rows, hps * GLA_DV), lambda b, h, t: (b * steps + t, h)),
            pl.BlockSpec((rows, hps * GLA_DV), lambda b, h, t: (b * steps + t, h)),
            pl.BlockSpec((rows, LANES), lambda b, h, t: (b * steps + t, 0)),
            pl.BlockSpec((LANES, hps * GLA_DK), lambda b, h, t: (0, h)),
            pl.BlockSpec((LANES, hps * GLA_DK), lambda b, h, t: (0, h)),
            pl.BlockSpec((1, hps * GLA_DK), lambda b, h, t: (0, h)),
            pl.BlockSpec((1, GLA_DV), lambda b, h, t: (0, 0)),
            pl.BlockSpec((1, hps, GLA_DK, GLA_DV), lambda b, h, t: (b, h, 0, 0)),
        ],
        out_specs=[
            pl.BlockSpec((rows, hps * GLA_DV), lambda b, h, t: (b * steps + t, h)),
            pl.BlockSpec((1, hps, GLA_DK, GLA_DV), lambda b, h, t: (b, h, 0, 0)),
        ],
        out_shape=[
            jax.ShapeDtypeStruct((n, GLA_VW), BF16),
            jax.ShapeDtypeStruct(s0.shape, F32),
        ],
        scratch_shapes=[
            pltpu.VMEM((hps, GLA_DK, GLA_DV), F32),
            pltpu.VMEM((hps, rows, GLA_DK), BF16),
            pltpu.VMEM((hps, rows, GLA_DV), F32),
            pltpu.VMEM((hps, n_sub, GLA_DK, GLA_DV), F32),
            pltpu.VMEM((hps, n_sub, GLA_DK, GLA_DK), F32),
        ],
        compiler_params=_params("parallel", "parallel", "arbitrary"),
        name="gla",
    )(gqk, gqk, gv, gg, ga, wa_hi, wa_lo, ba, nw, s0)


def _out_router_kernel(att_ref, gla_ref, x_ref, woa_ref, wog_ref, n2_ref, wrh_ref, wrl_ref, br_ref,
                       h_ref, hn_ref, re_ref, rg_ref):
    h = x_ref[...] + _dot(att_ref[...], woa_ref[...]) + _dot(gla_ref[...], wog_ref[...])
    h_ref[...] = h
    hn = h * lax.rsqrt(jnp.mean(h * h, axis=-1, keepdims=True) + RMS_EPS) * n2_ref[...]
    pieces = hn.shape[1] // LANES
    for c in range(pieces):
        hn_ref[pl.ds(c, hn.shape[0], stride=pieces), :] = hn[:, c * LANES:(c + 1) * LANES]
    hn_hi = hn.astype(BF16)
    hn_lo = (hn - hn_hi.astype(F32)).astype(BF16)
    logits = (_dot(hn_hi, wrh_ref[...]) + _dot(hn_hi, wrl_ref[...]) + _dot(hn_lo, wrh_ref[...])) + br_ref[...]
    col = lax.broadcasted_iota(I32, logits.shape, 1)
    big = jnp.int32(2 ** 30)
    in_g = col < N_GROUPS
    lg = jnp.where(in_g, logits, -jnp.inf)
    m1 = jnp.max(lg, axis=-1, keepdims=True)
    g_idx = jnp.min(jnp.where(lg == m1, col, big), axis=-1, keepdims=True)
    g_top = 1.0 / jnp.sum(jnp.exp(lg - m1), axis=-1, keepdims=True)
    ecol = col - N_GROUPS
    in_e = (ecol >= g_idx * EXPERTS_PER_GROUP) & (ecol < (g_idx + 1) * EXPERTS_PER_GROUP)
    le = jnp.where(in_e, logits, -jnp.inf)
    l1 = jnp.max(le, axis=-1, keepdims=True)
    e1 = jnp.min(jnp.where(le == l1, ecol, big), axis=-1, keepdims=True)
    le2 = jnp.where(ecol == e1, -jnp.inf, le)
    l2 = jnp.max(le2, axis=-1, keepdims=True)
    e2 = jnp.min(jnp.where(le2 == l2, ecol, big), axis=-1, keepdims=True)
    w2 = jnp.exp(l2 - l1)
    den = 1.0 + w2
    re_ref[...] = jnp.where(col == 0, e1, jnp.where(col == 1, e2, 0))
    rg_ref[...] = jnp.where(col == 0, g_top / den, jnp.where(col == 1, g_top * w2 / den, 0.0))


def _out_router_into_kernel(*refs):
    _out_router_kernel(*refs[:9], *refs[10:])


def _out_router_alloc_kernel(*refs, n_real):
    i = pl.program_id(0)

    @pl.when(i < n_real)
    def _():
        _out_router_kernel(*refs)

    @pl.when(i >= n_real)
    def _():
        refs[10][...] = jnp.zeros_like(refs[10])


def _out_router(att, gla, x, wo_att, wo_gla, n2, wr_hi, wr_lo, br, tm, n_all, row0, hn_all=None):
    n, d = x.shape
    n_real = n // tm
    row = lambda i: (jnp.minimum(i, n_real - 1), 0)
    fix = lambda i: (0, 0)
    b0 = row0 // tm
    if hn_all is None:
        assert row0 == 0 and n_all % tm == 0
        body, steps = functools.partial(_out_router_alloc_kernel, n_real=n_real), n_all // tm
        extra_in, extra_args, alias = [], [], {}
    else:
        body, steps = _out_router_into_kernel, n_real
        extra_in, extra_args, alias = [pl.BlockSpec(memory_space=pl.ANY)], [hn_all], {9: 1}
    return pl.pallas_call(
        body,
        grid=(steps,),
        input_output_aliases=alias,
        in_specs=[
            pl.BlockSpec((tm, ATT_WIDTH), row),
            pl.BlockSpec((tm, GLA_VW), row),
            pl.BlockSpec((tm, d), row),
            pl.BlockSpec((ATT_WIDTH, d), fix),
            pl.BlockSpec((GLA_VW, d), fix),
            pl.BlockSpec((1, d), fix),
            pl.BlockSpec((d, LANES), fix),
            pl.BlockSpec((d, LANES), fix),
            pl.BlockSpec((1, LANES), fix),
        ] + extra_in,
        out_specs=[
            pl.BlockSpec((tm, d), row),
            pl.BlockSpec((tm * (d // LANES), LANES), lambda i: (b0 + i, 0)),
            pl.BlockSpec((tm, LANES), row),
            pl.BlockSpec((tm, LANES), row),
        ],
        out_shape=[
            jax.ShapeDtypeStruct((n, d), F32),
            jax.ShapeDtypeStruct((n_all * (d // LANES), LANES), F32),
            jax.ShapeDtypeStruct((n, LANES), I32),
            jax.ShapeDtypeStruct((n, LANES), F32),
        ],
        compiler_params=_params("arbitrary"),
        name="out_router",
    )(att, gla, x, wo_att, wo_gla, n2, wr_hi, wr_lo, br, *extra_args)


def _expert_kernel(be_ref, tok_ref, nused_ref, hn_hbm, wg_ref, wu_ref, wd_ref, o_ref,
                   xbuf, wgb, wub, wdb, sem, *, rows, n_blocks):
    i = pl.program_id(0)

    @pl.when((i == 0) | (be_ref[i] != be_ref[jnp.maximum(i - 1, 0)]))
    def _():
        wgb[...] = wg_ref[0].astype(BF16)
        wub[...] = wu_ref[0].astype(BF16)
        wdb[...] = wd_ref[0].astype(BF16)

    n_used = nused_ref[0]

    pieces = wgb.shape[0] // LANES

    def issue(blk, slot, r, priority):
        src = pl.multiple_of(tok_ref[blk * rows + r] * pieces, pieces)
        pltpu.make_async_copy(hn_hbm.at[pl.ds(src, pieces), :], xbuf.at[slot, pl.ds(r * pieces, pieces), :],
                              sem.at[slot]).start(priority=priority)

    def wait_rows(slot):
        pltpu.make_async_copy(hn_hbm.at[pl.ds(0, rows * pieces), :], xbuf.at[slot], sem.at[slot]).wait()

    @pl.when(i == 0)
    def _():
        def body(g, carry):
            for u in range(DMA_UNROLL):
                issue(0, 0, g * DMA_UNROLL + u, u % 2)
            return carry
        lax.fori_loop(0, rows // DMA_UNROLL, body, 0)

    slot = i % 2

    @pl.when(i < n_used)
    def _():
        wait_rows(slot)
        xs = xbuf.at[slot]
        x = jnp.concatenate([xs[pl.ds(c, rows, stride=pieces), :] for c in range(pieces)], axis=1).astype(BF16)
        for r in range(rows):
            issue(i + 1, 1 - slot, r, r % 2)
        a = _dot(x, wgb[...])
        u = _dot(x, wub[...])
        act = (a * jax.nn.sigmoid(a) * u).astype(BF16)
        o_ref[...] = _dot(act, wdb[...])

    @pl.when(i == n_used)
    def _():
        wait_rows(slot)

    @pl.when(i >= n_used)
    def _():
        o_ref[...] = jnp.zeros_like(o_ref)


def _experts(block_expert, slot_tok, n_used, hn, wg, wu, wd):
    n_blocks = block_expert.shape[0]
    rows = EXPERT_ROWS
    d = wg.shape[1]
    f = wg.shape[2]
    pieces = d // LANES
    grid_spec = pltpu.PrefetchScalarGridSpec(
        num_scalar_prefetch=3,
        grid=(n_blocks,),
        in_specs=[
            pl.BlockSpec(memory_space=pl.ANY),
            pl.BlockSpec((1, d, f), lambda i, be, tk, nu: (be[i], 0, 0)),
            pl.BlockSpec((1, d, f), lambda i, be, tk, nu: (be[i], 0, 0)),
            pl.BlockSpec((1, f, d), lambda i, be, tk, nu: (be[i], 0, 0)),
        ],
        out_specs=pl.BlockSpec((rows, d), lambda i, be, tk, nu: (i, 0)),
        scratch_shapes=[
            pltpu.VMEM((2, rows * pieces, LANES), F32),
            pltpu.VMEM((d, f), BF16),
            pltpu.VMEM((d, f), BF16),
            pltpu.VMEM((f, d), BF16),
            pltpu.SemaphoreType.DMA((2,)),
        ],
    )
    return pl.pallas_call(
        functools.partial(_expert_kernel, rows=rows, n_blocks=n_blocks),
        grid_spec=grid_spec,
        out_shape=jax.ShapeDtypeStruct((n_blocks * rows, d), F32),
        compiler_params=_params("arbitrary"),
        name="experts",
    )(block_expert, slot_tok, n_used, hn, wg, wu, wd)


def _combine_kernel(pos_ref, h_ref, rg_ref, ys_hbm, y_ref, buf, sem, *, rows, n_steps, tok0):
    i = pl.program_id(0)

    def start(step, slot):
        def body(g, carry):
            for u in range(DMA_UNROLL):
                r = g * DMA_UNROLL + u
                for k in range(EXPERT_TOPK):
                    p = pos_ref[(tok0 + step * rows + r) * EXPERT_TOPK + k]
                    pltpu.make_async_copy(ys_hbm.at[pl.ds(p, 1), :], buf.at[slot, k, pl.ds(r, 1), :],
                                          sem.at[slot]).start(priority=k % 2)
            return carry
        lax.fori_loop(0, rows // DMA_UNROLL, body, 0)

    @pl.when(i == 0)
    def _():
        start(0, 0)

    @pl.when(i + 1 < n_steps)
    def _():
        start(i + 1, (i + 1) % 2)

    slot = i % 2

    for k in range(EXPERT_TOPK):
        pltpu.make_async_copy(ys_hbm.at[pl.ds(0, rows), :], buf.at[slot, k], sem.at[slot]).wait()

    rg = rg_ref[...]
    y = h_ref[...]
    for k in range(EXPERT_TOPK):
        y = y + buf[slot, k] * rg[:, k:k + 1]
    y_ref[...] = y


def _combine(pos, h, rg, ys, tok0, rows):
    n_tok, d = h.shape
    n_steps = n_tok // rows
    grid_spec = pltpu.PrefetchScalarGridSpec(
        num_scalar_prefetch=1,
        grid=(n_steps,),
        in_specs=[
            pl.BlockSpec((rows, d), lambda i, p: (i, 0)),
            pl.BlockSpec((rows, LANES), lambda i, p: (i, 0)),
            pl.BlockSpec(memory_space=pl.ANY),
        ],
        out_specs=pl.BlockSpec((rows, d), lambda i, p: (i, 0)),
        scratch_shapes=[
            pltpu.VMEM((2, EXPERT_TOPK, rows, d), F32),
            pltpu.SemaphoreType.DMA((2,)),
        ],
    )
    return pl.pallas_call(
        functools.partial(_combine_kernel, rows=rows, n_steps=n_steps, tok0=tok0),
        grid_spec=grid_spec,
        out_shape=jax.ShapeDtypeStruct((n_tok, d), F32),
        compiler_params=_params("arbitrary"),
        name="combine",
    )(pos, h, rg, ys)


def _dispatch(expert):
    a = expert.shape[0]
    rows = EXPERT_ROWS
    onehot = (expert[:, None] == jnp.arange(N_EXPERTS, dtype=I32)[None, :]).astype(I32)
    csum = jnp.cumsum(onehot, axis=0)
    counts = csum[-1]
    rank = jnp.take_along_axis(csum, expert[:, None], axis=1)[:, 0] - 1
    padded = (counts + rows - 1) // rows * rows
    ends_p = jnp.cumsum(padded)
    pstart = ends_p - padded
    pos = (pstart[expert] + rank).astype(I32)
    n_blocks = -(-a // rows) + N_EXPERTS + 1
    slot_tok = jnp.zeros((n_blocks * rows,), I32).at[pos].set(jnp.arange(a, dtype=I32) // EXPERT_TOPK)
    block_start = jnp.arange(n_blocks, dtype=I32) * rows
    block_expert = jnp.minimum(
        jnp.sum((ends_p[None, :] <= block_start[:, None]).astype(I32), axis=1), N_EXPERTS - 1).astype(I32)
    n_used = (ends_p[-1:] // rows).astype(I32)
    return pos, slot_tok, block_expert, n_used


def kernel(x_prompt, x_sample, cache_k, cache_v, state_gla, page_table, norm1_w, w_in, q_norm_w, k_norm_w,
           w_gla_a2, b_gla_a, gla_norm_w, w_out, norm2_w, w_r1, b_r1, w_r2, b_r2, w_e_gate, w_e_up, w_e_down):
    depth = w_in.shape[0]
    assert depth == 1
    batch, seq, d = x_prompt.shape
    db, nq, _ = x_sample.shape
    n_pages = page_table.shape[1]
    past = n_pages * PAGE_SIZE
    assert past % MOBA_BLOCK == 0 and nq <= GLA_CHUNK and seq % (8 * MOBA_BLOCK) == 0
    n_past_blocks = past // MOBA_BLOCK
    assert n_past_blocks + 8 <= LANES
    n_p, n_s = batch * seq, db * nq
    l = 0

    w = w_in[l]
    o_gq = 3 * ATT_WIDTH
    o_gv = o_gq + 2 * GLA_KW
    o_ga = o_gv + GLA_VW
    o_gg = o_ga + GLA_GATE_RANK
    w_main = jnp.concatenate([w[:, :o_ga], w[:, o_gg:]], axis=1).astype(BF16)
    w_ga = jnp.pad(w[:, o_ga:o_gg], ((0, 0), (0, LANES - GLA_GATE_RANK))).astype(BF16)
    n1 = norm1_w[l][None, :]
    qn = q_norm_w[l][None, :]
    kn = k_norm_w[l][None, :]
    wa = jnp.pad(w_gla_a2[l], ((0, LANES - GLA_GATE_RANK), (0, 0)))
    wa_hi = wa.astype(BF16)
    wa_lo = (wa - wa_hi.astype(F32)).astype(BF16)
    ba = b_gla_a[l][None, :]
    gnw = gla_norm_w[l][None, :]
    wo = w_out[l].astype(BF16)
    wo_att, wo_gla = wo[:ATT_WIDTH], wo[ATT_WIDTH:]
    n2 = norm2_w[l][None, :]
    wr = jnp.pad(jnp.concatenate([w_r1[l], w_r2[l]], axis=1), ((0, 0), (0, LANES - N_GROUPS - N_EXPERTS)))
    br = jnp.pad(jnp.concatenate([b_r1[l], b_r2[l]]), (0, LANES - N_GROUPS - N_EXPERTS))[None, :]
    slopes = 2.0 ** (-8.0 * jnp.arange(1, N_ATT_HEADS + 1, dtype=F32) / N_ATT_HEADS)

    xp = x_prompt.reshape(n_p, d)
    q_p, k_p, kb_p, v_p, vt_p, gqk_p, gv_p, gg_p, ga_p = _project(xp, n1, w_main, w_ga, qn, kn, 512)
    kmeans = _block_means(k_p)
    idx_p = _prompt_select(q_p, kmeans, seq)
    att_p, block_sums = _moba_prompt(page_table, slopes, q_p, kb_p, vt_p, idx_p, cache_k[l], batch, seq)
    s0_p = jnp.zeros((batch, N_GLA_HEADS, GLA_DK, GLA_DV), F32)
    gla_p, s_p = _gla(gqk_p, gv_p, gg_p, ga_p, wa_hi, wa_lo, ba, gnw, s0_p, batch, seq, seq, GLA_HEADS_PROMPT)

    xs = x_sample.reshape(n_s, d)
    q_s, k_s, _, v_s, _, gqk_s, gv_s, gg_s, ga_s = _project(xs, n1, w_main, w_ga, qn, kn, n_s)
    idx = _sample_select(q_s, block_sums.reshape(db * n_past_blocks, ATT_WIDTH), k_s, db, nq, n_past_blocks)
    att_s = _moba_sample(page_table, idx[..., :MOBA_TOPK], slopes, q_s, k_s, v_s, cache_k[l], cache_v[l], db, nq)

    def pad_seq(a):
        return jnp.pad(a.reshape(db, nq, -1), ((0, 0), (0, GLA_CHUNK - nq), (0, 0))).reshape(db * GLA_CHUNK, -1)

    gla_s, s_s = _gla(pad_seq(gqk_s), pad_seq(gv_s), pad_seq(gg_s), pad_seq(ga_s), wa_hi, wa_lo, ba, gnw,
                      state_gla[l], db, GLA_CHUNK, nq, N_GLA_HEADS)
    gla_s = gla_s.reshape(db, GLA_CHUNK, GLA_VW)[:, :nq].reshape(n_s, GLA_VW)

    wr_hi = wr.astype(BF16)
    wr_lo = (wr - wr_hi.astype(F32)).astype(BF16)
    n_all = -(-(n_p + n_s) // OUT_ROWS) * OUT_ROWS
    h_p, hn, re_p, rg_p = _out_router(att_p, gla_p, xp, wo_att, wo_gla, n2, wr_hi, wr_lo, br, OUT_ROWS, n_all, 0)
    h_s, hn, re_s, rg_s = _out_router(att_s.astype(BF16), gla_s, xs, wo_att, wo_gla, n2, wr_hi, wr_lo, br,
                                      n_s, n_all, n_p, hn)
    expert = jnp.concatenate([re_p[:, :EXPERT_TOPK], re_s[:, :EXPERT_TOPK]], axis=0).reshape(-1)
    pos, slot_tok, block_expert, n_used = _dispatch(expert)
    ys = _experts(block_expert, slot_tok, n_used, hn, w_e_gate[l], w_e_up[l], w_e_down[l])
    y_p = _combine(pos, h_p, rg_p, ys, 0, 128)
    y_s = _combine(pos, h_s, rg_s, ys, n_p, 128)

    hd = (N_ATT_HEADS, HEAD_DIM)
    return (y_p.reshape(batch, seq, d), y_s.reshape(db, nq, d),
            k_p.reshape(1, batch, seq, *hd), v_p.reshape(1, batch, seq, *hd), s_p[None],
            k_s.reshape(1, db, nq, *hd), v_s.reshape(1, db, nq, *hd), s_s[None])
```

```python
import functools

import jax
import jax.numpy as jnp
from jax import lax
from jax.experimental import pallas as pl
from jax.experimental.pallas import tpu as pltpu

F32 = jnp.float32
BF16 = jnp.bfloat16
I32 = jnp.int32

HEAD_DIM = 128
N_ATT_HEADS = 8
ATT_WIDTH = N_ATT_HEADS * HEAD_DIM
MOBA_BLOCK = 256
MOBA_TOPK = 3
MOBA_HALF = 2
HEADS_PER_STEP = 2
SELECT_ROWS = 2048
SUM_ROWS = 16
LOG2E = 1.4426950408889634
PAGE_SIZE = 128
PAGES_PER_BLOCK = MOBA_BLOCK // PAGE_SIZE
N_GLA_HEADS = 4
GLA_DK = 128
GLA_DV = 256
GLA_KW = N_GLA_HEADS * GLA_DK
GLA_VW = N_GLA_HEADS * GLA_DV
GLA_GATE_RANK = 16
GLA_TAU = 16.0
GLA_CHUNK = 64
GLA_HEADS_PROMPT = 2
N_GROUPS = 4
EXPERTS_PER_GROUP = 8
N_EXPERTS = N_GROUPS * EXPERTS_PER_GROUP
EXPERT_TOPK = 2
RMS_EPS = 1e-6
NEG_INF = -1e30
LANES = 128
EXPERT_ROWS = 256
OUT_ROWS = 512
DMA_UNROLL = 8
VMEM_LIMIT = 56 * 1024 * 1024

_NT = (((1,), (1,)), ((), ()))
_TN = (((0,), (0,)), ((), ()))
_HI = lax.Precision.HIGHEST


def _dot(a, b):
    return jnp.dot(a, b, preferred_element_type=F32)


def _dot_nt(a, b):
    return lax.dot_general(a, b, _NT, preferred_element_type=F32)


def _dot_tn(a, b):
    return lax.dot_general(a, b, _TN, preferred_element_type=F32)


def _params(*sem):
    return pltpu.CompilerParams(dimension_semantics=sem, vmem_limit_bytes=VMEM_LIMIT)


def _top3(gate, col, n_valid, axis=-1):
    picks = []
    g = gate
    for r in range(MOBA_TOPK):
        m = jnp.max(g, axis=axis, keepdims=True)
        idx = jnp.min(jnp.where(g == m, col, jnp.int32(2 ** 30)), axis=axis, keepdims=True)
        g = jnp.where(col == idx, -jnp.inf, g)
        picks.append(jnp.where(r < n_valid, idx, -1))
    return picks


def _proj_kernel(x_ref, n1_ref, w_ref, wga_ref, qn_ref, kn_ref,
                 q_ref, k_ref, kb_ref, v_ref, vt_ref, gqk_ref, gv_ref, gg_ref, ga_ref, xn_ref):
    j = pl.program_id(1)

    @pl.when(j == 0)
    def _():
        x = x_ref[...]
        y = x * lax.rsqrt(jnp.mean(x * x, axis=-1, keepdims=True) + RMS_EPS) * n1_ref[...]
        xn_ref[...] = y.astype(BF16)
        ga_ref[...] = _dot(xn_ref[...], wga_ref[...])

    z = _dot(xn_ref[...], w_ref[...])

    def head_norm(w):
        outs = []
        for h in range(N_ATT_HEADS):
            zh = z[:, h * HEAD_DIM:(h + 1) * HEAD_DIM]
            outs.append(zh * lax.rsqrt(jnp.mean(zh * zh, axis=-1, keepdims=True) + RMS_EPS) * w)
        return outs

    @pl.when(j == 0)
    def _():
        for h, y in enumerate(head_norm(qn_ref[...])):
            q_ref[:, h * HEAD_DIM:(h + 1) * HEAD_DIM] = y

    @pl.when(j == 1)
    def _():
        for h, y in enumerate(head_norm(kn_ref[...])):
            k_ref[:, h * HEAD_DIM:(h + 1) * HEAD_DIM] = y
            kb_ref[:, h * HEAD_DIM:(h + 1) * HEAD_DIM] = y.astype(BF16)

    @pl.when(j == 2)
    def _():
        v_ref[...] = z
        zt = z.T.astype(BF16)
        for c in range(vt_ref.shape[0]):
            vt_ref[c] = zt[:, c * MOBA_BLOCK:(c + 1) * MOBA_BLOCK]

    @pl.when(j == 3)
    def _():
        gqk_ref[...] = z

    @pl.when(j == 4)
    def _():
        gv_ref[...] = z

    @pl.when(j == 5)
    def _():
        gg_ref[...] = z


def _project(x, n1, w_main, w_ga, qn, kn, tm):
    n, d = x.shape
    wide = ATT_WIDTH
    row = lambda i, j: (i, 0)
    out_shape = [
        jax.ShapeDtypeStruct((n, wide), F32),
        jax.ShapeDtypeStruct((n, wide), F32),
        jax.ShapeDtypeStruct((n, wide), BF16),
        jax.ShapeDtypeStruct((n, wide), F32),
        jax.ShapeDtypeStruct((n // MOBA_BLOCK, wide, MOBA_BLOCK), BF16),
        jax.ShapeDtypeStruct((n, wide), F32),
        jax.ShapeDtypeStruct((n, wide), F32),
        jax.ShapeDtypeStruct((n, wide), F32),
        jax.ShapeDtypeStruct((n, LANES), F32),
    ]
    out_specs = [pl.BlockSpec((tm, s.shape[1]), row) if len(s.shape) == 2
                 else pl.BlockSpec((tm // MOBA_BLOCK, wide, MOBA_BLOCK), lambda i, j: (i, 0, 0)) for s in out_shape]
    return pl.pallas_call(
        _proj_kernel,
        grid=(n // tm, 6),
        in_specs=[
            pl.BlockSpec((tm, d), row),
            pl.BlockSpec((1, d), lambda i, j: (0, 0)),
            pl.BlockSpec((d, wide), lambda i, j: (0, j)),
            pl.BlockSpec((d, LANES), lambda i, j: (0, 0)),
            pl.BlockSpec((1, HEAD_DIM), lambda i, j: (0, 0)),
            pl.BlockSpec((1, HEAD_DIM), lambda i, j: (0, 0)),
        ],
        out_specs=out_specs,
        out_shape=out_shape,
        scratch_shapes=[pltpu.VMEM((tm, d), BF16)],
        compiler_params=_params("parallel", "arbitrary"),
        name="projection",
    )(x, n1, w_main, w_ga, qn, kn)


def _kmeans_kernel(k_ref, o_ref):
    rows = k_ref.shape[0]
    k = k_ref[...].reshape(rows // MOBA_BLOCK, MOBA_BLOCK, k_ref.shape[1])
    o_ref[...] = jnp.sum(k, axis=1) * (1.0 / MOBA_BLOCK)


def _block_means(k):
    n, w = k.shape
    rows = 8 * MOBA_BLOCK
    return pl.pallas_call(
        _kmeans_kernel,
        grid=(n // rows,),
        in_specs=[pl.BlockSpec((rows, w), lambda i: (i, 0))],
        out_specs=pl.BlockSpec((8, w), lambda i: (i, 0)),
        out_shape=jax.ShapeDtypeStruct((n // MOBA_BLOCK, w), F32),
        compiler_params=_params("parallel"),
        name="block_means",
    )(k)


def _prompt_select_kernel(q_ref, km_ref, o_ref, *, tiles_per_seq):
    c = pl.program_id(0)
    rows = q_ref.shape[0]
    nb = km_ref.shape[0]
    q = q_ref[...]
    km = km_ref[...]
    q_hi = q.astype(BF16)
    q_lo = (q - q_hi.astype(F32)).astype(BF16)
    km_hi = km.astype(BF16)
    km_lo = (km - km_hi.astype(F32)).astype(BF16)
    gate = _dot_nt(km_hi, q_hi) + _dot_nt(km_lo, q_hi) + _dot_nt(km_hi, q_lo)
    blk = lax.broadcasted_iota(I32, (nb, rows), 0)
    n_full = (c % tiles_per_seq) * (rows // MOBA_BLOCK) + lax.broadcasted_iota(I32, (1, rows), 1) // MOBA_BLOCK
    gate = jnp.where(blk < n_full, gate, NEG_INF)
    i0, i1, i2 = _top3(gate, blk, n_full, axis=0)
    row = lax.broadcasted_iota(I32, (8, rows), 0)
    o_ref[...] = jnp.where(row == 0, i0, jnp.where(row == 1, i1, jnp.where(row == 2, i2, -1)))


def _prompt_select(q, kmeans, seq):
    n = q.shape[0]
    rows = SELECT_ROWS
    nb = seq // MOBA_BLOCK
    tiles_per_seq = seq // rows
    return pl.pallas_call(
        functools.partial(_prompt_select_kernel, tiles_per_seq=tiles_per_seq),
        grid=(n // rows, N_ATT_HEADS),
        in_specs=[
            pl.BlockSpec((rows, HEAD_DIM), lambda c, h: (c, h)),
            pl.BlockSpec((nb, HEAD_DIM), lambda c, h: (c // tiles_per_seq, h)),
        ],
        out_specs=pl.BlockSpec((8, rows), lambda c, h: (h, c)),
        out_shape=jax.ShapeDtypeStruct((N_ATT_HEADS * 8, n), I32),
        compiler_params=_params("parallel", "parallel"),
        name="prompt_select",
    )(q, kmeans)


def _moba_prompt_kernel(pt_ref, slopes_ref, q_ref, k_ref, vt_ref, idx_ref, kc_hbm, o_ref, ps_ref,
                        sa_scr, sb_scr, bias_scr, pbuf, psem, *, group, n_steps):
    hg = pl.program_id(1)
    i = pl.program_id(2)
    bs = MOBA_BLOCK
    nb = vt_ref.shape[0]
    t = (pl.program_id(0) * pl.num_programs(1) + hg) * nb + i

    def page_copy(tt, slot, g):
        return pltpu.make_async_copy(kc_hbm.at[pt_ref[tt * group + g]], pbuf.at[slot, g], psem.at[slot])

    @pl.when(t == 0)
    def _():
        for g in range(group):
            page_copy(0, 0, g).start()

    @pl.when(t + 1 < n_steps)
    def _():
        for g in range(group):
            page_copy(t + 1, (t + 1) % 2, g).start()

    pslot = t % 2
    for g in range(group):
        page_copy(t, pslot, g).wait()
    for bk in range(group // PAGES_PER_BLOCK):
        acc = jnp.sum(pbuf[pslot, PAGES_PER_BLOCK * bk], axis=0)
        for p in range(1, PAGES_PER_BLOCK):
            acc = acc + jnp.sum(pbuf[pslot, PAGES_PER_BLOCK * bk + p], axis=0)
        ps_ref[bk] = acc

    krow = lax.broadcasted_iota(I32, (bs, bs), 0)
    qcol = lax.broadcasted_iota(I32, (bs, bs), 1)
    ones_rows = jnp.ones((SUM_ROWS, bs), BF16)
    hb = MOBA_HALF

    def head_job(hh):
        lanes = slice(hh * HEAD_DIM, (hh + 1) * HEAD_DIM)
        slope2 = slopes_ref[hg * HEADS_PER_STEP + hh] * LOG2E
        q2 = (q_ref[:, lanes] * (HEAD_DIM ** -0.5 * LOG2E)).astype(BF16)
        i0 = idx_ref[8 * hh:8 * hh + 1, :]
        i1 = idx_ref[8 * hh + 1:8 * hh + 2, :]
        i2 = idx_ref[8 * hh + 2:8 * hh + 3, :]
        bias_ref, sa_ref, sb_ref = bias_scr.at[hh], sa_scr.at[hh], sb_scr.at[hh]
        bias_ref[...] = slope2 * krow.astype(F32)

        def pv_dot(jc, p):
            return _dot(jnp.concatenate([vt_ref[jc, lanes, :], ones_rows], axis=0), p.astype(BF16))

        def selected(j):
            return (i0 == j) | (i1 == j) | (i2 == j)

        def block_offset(j):
            return slope2 * ((j - i) * bs).astype(F32)

        def sweep1(jb, s_ref):
            r0 = pl.multiple_of(jnp.minimum(jb, nb - hb) * bs, bs)
            s = _dot_nt(k_ref[pl.ds(r0, hb * bs), lanes], q2)
            cmax = jnp.full((1, bs), NEG_INF, F32)
            for u in range(hb):
                su = s[u * bs:(u + 1) * bs] + bias_ref[...]
                s_ref[u] = su
                cmax = jnp.maximum(cmax, jnp.where(selected(jb + u),
                                                   jnp.max(su, axis=0, keepdims=True) + block_offset(jb + u),
                                                   NEG_INF))
            return cmax

        def sweep2(jb, s_ref, m_prev, m_cur, l, acc):
            pv = jnp.zeros((HEAD_DIM + SUM_ROWS, bs), F32)
            for u in range(hb):
                j = jb + u
                ref = jnp.where(selected(j), m_cur - block_offset(j), -NEG_INF)
                pv = pv + pv_dot(jnp.minimum(j, nb - 1), jnp.exp2(s_ref[u] - ref))
            alpha = jnp.exp2(m_prev - m_cur)
            return alpha * l + pv[HEAD_DIM:HEAD_DIM + 1], alpha * acc + pv[:HEAD_DIM]

        def prologue():
            r_own = pl.multiple_of(i * bs, bs)
            s = jnp.where(krow <= qcol, _dot_nt(k_ref[pl.ds(r_own, bs), lanes], q2) + bias_ref[...], NEG_INF)
            m = jnp.max(s, axis=0, keepdims=True)
            pv = pv_dot(i, jnp.exp2(s - m))
            m_a = jnp.maximum(m, sweep1(0, sa_ref))
            m_b = jnp.maximum(m_a, sweep1(hb, sb_ref))
            return m, m_a, m_b, pv[HEAD_DIM:HEAD_DIM + 1], pv[:HEAD_DIM]

        def chunk(c, carry):
            m_prev, m_a, m_b, l, acc = carry
            jb = c * (2 * hb)
            l, acc = sweep2(jb, sa_ref, m_prev, m_a, l, acc)
            m_a2 = jnp.maximum(m_b, sweep1(jb + 2 * hb, sa_ref))
            l, acc = sweep2(jb + hb, sb_ref, m_a, m_b, l, acc)
            m_b2 = jnp.maximum(m_a2, sweep1(jb + 3 * hb, sb_ref))
            return m_b, m_a2, m_b2, l, acc

        def finish(carry):
            _, _, _, l, acc = carry
            o_ref[:, lanes] = (acc / l).T.astype(o_ref.dtype)

        return prologue, chunk, finish

    jobs = [head_job(hh) for hh in range(HEADS_PER_STEP)]
    carries = tuple(job[0]() for job in jobs)
    n_chunks = (i + 2 * hb - 1) // (2 * hb)
    carries = lax.fori_loop(0, n_chunks, lambda c, cs: tuple(job[1](c, cr) for job, cr in zip(jobs, cs)), carries)
    for job, cr in zip(jobs, carries):
        job[2](cr)


def _moba_prompt(page_table, slopes, q, kb, vt, idx, cache_k, batch, seq):
    n = q.shape[0]
    nb = seq // MOBA_BLOCK
    bs = MOBA_BLOCK
    hps = HEADS_PER_STEP
    n_hg = N_ATT_HEADS // hps
    n_steps = batch * n_hg * nb
    n_pages_total = page_table.size
    assert n_pages_total % (n_steps * PAGES_PER_BLOCK) == 0
    group = n_pages_total // n_steps
    bpc = group // PAGES_PER_BLOCK
    grid_spec = pltpu.PrefetchScalarGridSpec(
        num_scalar_prefetch=1,
        grid=(batch, n_hg, nb),
        in_specs=[
            pl.BlockSpec(memory_space=pltpu.SMEM),
            pl.BlockSpec((bs, hps * HEAD_DIM), lambda b, h, i, pt: (b * nb + i, h)),
            pl.BlockSpec((seq, hps * HEAD_DIM), lambda b, h, i, pt: (b, h)),
            pl.BlockSpec((nb, hps * HEAD_DIM, bs), lambda b, h, i, pt: (b, h, 0)),
            pl.BlockSpec((8 * hps, bs), lambda b, h, i, pt: (h, b * nb + i)),
            pl.BlockSpec(memory_space=pl.ANY),
        ],
        out_specs=[
            pl.BlockSpec((bs, hps * HEAD_DIM), lambda b, h, i, pt: (b * nb + i, h)),
            pl.BlockSpec((bpc, N_ATT_HEADS, HEAD_DIM), lambda b, h, i, pt: ((b * n_hg + h) * nb + i, 0, 0)),
        ],
        scratch_shapes=[
            pltpu.VMEM((hps, MOBA_HALF, bs, bs), F32),
            pltpu.VMEM((hps, MOBA_HALF, bs, bs), F32),
            pltpu.VMEM((hps, bs, bs), F32),
            pltpu.VMEM((2, group, PAGE_SIZE, N_ATT_HEADS, HEAD_DIM), F32),
            pltpu.SemaphoreType.DMA((2,)),
        ],
    )
    return pl.pallas_call(
        functools.partial(_moba_prompt_kernel, group=group, n_steps=n_steps),
        grid_spec=grid_spec,
        out_shape=[
            jax.ShapeDtypeStruct((n, ATT_WIDTH), BF16),
            jax.ShapeDtypeStruct((n_steps * bpc, N_ATT_HEADS, HEAD_DIM), F32),
        ],
        compiler_params=_params("arbitrary", "arbitrary", "arbitrary"),
        name="moba_prompt",
    )(page_table.reshape(-1), slopes, q, kb, vt, idx, cache_k)


def _sample_select_kernel(q_ref, bs_ref, kn_ref, o_ref, *, n_past_blocks, past):
    nq = q_ref.shape[0]
    col = lax.broadcasted_iota(I32, (nq, LANES), 1)
    lane = col
    n_full = (past + lax.broadcasted_iota(I32, (nq, 1), 0)) // MOBA_BLOCK
    for h in range(N_ATT_HEADS):
        sl = slice(h * HEAD_DIM, (h + 1) * HEAD_DIM)
        m_past = bs_ref[:, sl] * (1.0 / MOBA_BLOCK)
        m_new = jnp.sum(kn_ref[:, sl], axis=0, keepdims=True) * (1.0 / MOBA_BLOCK)
        row8 = lax.broadcasted_iota(I32, (8, HEAD_DIM), 0)
        new_rows = jnp.where(row8 == 0, jnp.broadcast_to(m_new, (8, HEAD_DIM)), 0.0)
        means = jnp.concatenate(
            [m_past, new_rows, jnp.zeros((LANES - n_past_blocks - 8, HEAD_DIM), F32)], axis=0)
        gate = lax.dot_general(q_ref[:, sl], means, _NT, precision=_HI, preferred_element_type=F32)
        gate = jnp.where(col < n_full, gate, NEG_INF)
        i0, i1, i2 = _top3(gate, col, n_full)
        o_ref[0, h] = jnp.where(lane == 0, i0, jnp.where(lane == 1, i1, jnp.where(lane == 2, i2, 0)))


def _sample_select(q, block_sums, k_new, db, nq, n_past_blocks):
    return pl.pallas_call(
        functools.partial(_sample_select_kernel, n_past_blocks=n_past_blocks,
                          past=n_past_blocks * MOBA_BLOCK),
        grid=(db,),
        in_specs=[
            pl.BlockSpec((nq, ATT_WIDTH), lambda s: (s, 0)),
            pl.BlockSpec((n_past_blocks, ATT_WIDTH), lambda s: (s, 0)),
            pl.BlockSpec((nq, ATT_WIDTH), lambda s: (s, 0)),
        ],
        out_specs=pl.BlockSpec((1, N_ATT_HEADS, nq, LANES), lambda s: (s, 0, 0, 0)),
        out_shape=jax.ShapeDtypeStruct((db, N_ATT_HEADS, nq, LANES), I32),
        compiler_params=_params("parallel"),
        name="sample_select",
    )(q, block_sums, k_new)


def _moba_sample_kernel(pt_ref, idx_ref, slopes_ref, q_ref, kn_ref, vn_ref, kc_hbm, vc_hbm, o_ref,
                        kbuf, vbuf, kown, vown, sem, *, n_steps, n_pages, nq, past):
    t = pl.program_id(0)
    nh = N_ATT_HEADS
    n_sel = nq * MOBA_TOPK
    n_slots = n_sel * PAGES_PER_BLOCK

    def copies(tt, slot, g):
        s = tt // nh
        h = tt % nh
        blk = idx_ref[tt * n_sel + g // PAGES_PER_BLOCK]
        page = pt_ref[s * n_pages + blk * PAGES_PER_BLOCK + g % PAGES_PER_BLOCK]
        return (pltpu.make_async_copy(kc_hbm.at[page, :, h, :], kbuf.at[slot, g], sem.at[0, slot]),
                pltpu.make_async_copy(vc_hbm.at[page, :, h, :], vbuf.at[slot, g], sem.at[1, slot]))

    def start(tt, slot):
        for g in range(n_slots):
            ck, cv = copies(tt, slot, g)
            ck.start()
            cv.start()

    @pl.when(t == 0)
    def _():
        start(0, 0)
        kown[...] = jnp.zeros_like(kown)
        vown[...] = jnp.zeros_like(vown)

    @pl.when(t + 1 < n_steps)
    def _():
        start(t + 1, (t + 1) % 2)

    slot = t % 2
    h = t % nh
    slope = slopes_ref[h]
    scale = HEAD_DIM ** -0.5
    qb = q_ref[...].astype(BF16)
    n_keys = n_slots * PAGE_SIZE
    keys_per_q = MOBA_TOPK * MOBA_BLOCK

    kown[0:nq, :] = kn_ref[...]
    vown[0:nq, :] = vn_ref[...]
    row = lax.broadcasted_iota(I32, (nq, LANES), 0)
    colo = lax.broadcasted_iota(I32, (nq, LANES), 1)
    s_own = _dot_nt(qb, kown[...].astype(BF16)) * scale - slope * (row - colo).astype(F32)
    s_own = jnp.where(colo <= row, s_own, NEG_INF)

    colk = lax.broadcasted_iota(I32, (1, n_keys), 1)
    grp = colk // MOBA_BLOCK
    blk_of_col = jnp.zeros((1, n_keys), I32)
    for g in range(n_sel):
        blk_of_col = jnp.where(grp == g, idx_ref[t * n_sel + g], blk_of_col)
    pos = blk_of_col * MOBA_BLOCK + colk % MOBA_BLOCK
    rowk = lax.broadcasted_iota(I32, (nq, n_keys), 0)
    mine = (lax.broadcasted_iota(I32, (nq, n_keys), 1) // keys_per_q) == rowk
    n_full = (past + rowk) // MOBA_BLOCK
    pick = (lax.broadcasted_iota(I32, (nq, n_keys), 1) // MOBA_BLOCK) % MOBA_TOPK
    valid = mine & (pick < n_full)

    for g in range(n_slots):
        ck, cv = copies(t, slot, g)
        ck.wait()
        cv.wait()

    kall = kbuf[slot].reshape(n_keys, HEAD_DIM).astype(BF16)
    s_sel = _dot_nt(qb, kall) * scale - slope * ((past + rowk) - pos).astype(F32)
    s_sel = jnp.where(valid, s_sel, NEG_INF)
    m = jnp.maximum(jnp.max(s_sel, axis=-1, keepdims=True), jnp.max(s_own, axis=-1, keepdims=True))
    p_sel = jnp.exp(s_sel - m)
    p_own = jnp.exp(s_own - m)
    l = jnp.sum(p_sel, axis=-1, keepdims=True) + jnp.sum(p_own, axis=-1, keepdims=True)
    vall = vbuf[slot].reshape(n_keys, HEAD_DIM).astype(BF16)
    o = _dot(p_sel.astype(BF16), vall) + _dot(p_own.astype(BF16), vown[...].astype(BF16))
    o_ref[...] = o / l


def _moba_sample(page_table, idx, slopes, q, k_new, v_new, cache_k, cache_v, db, nq):
    n_pages = page_table.shape[1]
    n_steps = db * N_ATT_HEADS
    n_slots = nq * MOBA_TOPK * PAGES_PER_BLOCK
    qspec = pl.BlockSpec((nq, HEAD_DIM), lambda t, pt, ix: (t // N_ATT_HEADS, t % N_ATT_HEADS))
    grid_spec = pltpu.PrefetchScalarGridSpec(
        num_scalar_prefetch=2,
        grid=(n_steps,),
        in_specs=[
            pl.BlockSpec(memory_space=pltpu.SMEM),
            qspec, qspec, qspec,
            pl.BlockSpec(memory_space=pl.ANY),
            pl.BlockSpec(memory_space=pl.ANY),
        ],
        out_specs=qspec,
        scratch_shapes=[
            pltpu.VMEM((2, n_slots, PAGE_SIZE, HEAD_DIM), F32),
            pltpu.VMEM((2, n_slots, PAGE_SIZE, HEAD_DIM), F32),
            pltpu.VMEM((LANES, HEAD_DIM), F32),
            pltpu.VMEM((LANES, HEAD_DIM), F32),
            pltpu.SemaphoreType.DMA((2, 2)),
        ],
    )
    return pl.pallas_call(
        functools.partial(_moba_sample_kernel, n_steps=n_steps, n_pages=n_pages, nq=nq,
                          past=n_pages * PAGE_SIZE),
        grid_spec=grid_spec,
        out_shape=jax.ShapeDtypeStruct((db * nq, ATT_WIDTH), F32),
        compiler_params=_params("arbitrary"),
        name="moba_sample",
    )(page_table.reshape(-1), idx.reshape(-1), slopes, q, k_new, v_new, cache_k, cache_v)


def _gla_kernel(gq_ref, gk_ref, gv_ref, gg_ref, ga_ref, wah_ref, wal_ref, ba_ref, nw_ref, s0_ref,
                o_ref, sfin_ref, s_scr, qd_scr, oi_scr, ds_scr, dec_scr, *, chunk, n_sub, t_valid, hps):
    t = pl.program_id(2)

    @pl.when(t == 0)
    def _():
        s_scr[...] = s0_ref[0]

    c = chunk
    tri = lax.broadcasted_iota(I32, (c, c), 0) >= lax.broadcasted_iota(I32, (c, c), 1)
    tri_b = jnp.where(tri, 1.0, 0.0).astype(BF16)
    ones_b = jnp.ones((c, GLA_DK), BF16)
    nw = nw_ref[...]

    def split3(x):
        hi = x.astype(BF16)
        r = x - hi.astype(F32)
        mid = r.astype(BF16)
        return hi, mid, (r - mid.astype(F32)).astype(BF16)

    ga = ga_ref[...]
    ga_hi = ga.astype(BF16)
    ga_lo = (ga - ga_hi.astype(F32)).astype(BF16)

    for hh in range(hps):
        kk = slice(hh * GLA_DK, (hh + 1) * GLA_DK)
        vv = slice(hh * GLA_DV, (hh + 1) * GLA_DV)
        wah, wal = wah_ref[:, kk], wal_ref[:, kk]
        pre = _dot(ga_hi, wah) + _dot(ga_hi, wal) + _dot(ga_lo, wah) + ba_ref[:, kk]
        la = (jnp.minimum(pre, 0.0) - jnp.log1p(jnp.exp(-jnp.abs(pre)))) * (1.0 / GLA_TAU)
        if t_valid < c * n_sub:
            la = jnp.where(lax.broadcasted_iota(I32, la.shape, 0) < t_valid, la, 0.0)

        for ci in range(n_sub):
            sl = slice(ci * c, (ci + 1) * c)
            parts = split3(la[sl])
            b = _dot(tri_b, parts[0]) + _dot(tri_b, parts[1]) + _dot(tri_b, parts[2])
            b_end = b[c - 1:c, :]
            b_end_col = _dot_tn(parts[0], ones_b) + _dot_tn(parts[1], ones_b) + _dot_tn(parts[2], ones_b)
            q = gq_ref[sl, kk] * (GLA_DK ** -0.5)
            k = gk_ref[sl, kk]
            v = gv_ref[sl, vv].astype(BF16)
            q_dec = (q * jnp.exp(b)).astype(BF16)
            k_inv = (k * jnp.exp(-b)).astype(BF16)
            k_end = (k * jnp.exp(b_end - b)).astype(BF16)
            a = jnp.where(tri, _dot_nt(q_dec, k_inv), 0.0)
            qd_scr[hh, sl, :] = q_dec
            oi_scr[hh, sl, :] = _dot(a.astype(BF16), v)
            ds_scr[hh, ci] = _dot_tn(k_end, v)
            dec_scr[hh, ci] = jnp.exp(b_end_col)

    for hh in range(hps):
        vv = slice(hh * GLA_DV, (hh + 1) * GLA_DV)
        state = s_scr[hh]
        for ci in range(n_sub):
            sl = slice(ci * c, (ci + 1) * c)
            o = oi_scr[hh, sl, :] + _dot(qd_scr[hh, sl, :], state.astype(BF16))
            state = jnp.concatenate([dec_scr[hh, ci]] * (GLA_DV // GLA_DK), axis=1) * state + ds_scr[hh, ci]
            on = o * lax.rsqrt(jnp.mean(o * o, axis=-1, keepdims=True) + RMS_EPS) * nw
            g = gg_ref[sl, vv]
            o_ref[sl, vv] = (on * (g * jax.nn.sigmoid(g))).astype(o_ref.dtype)
        s_scr[hh] = state

    @pl.when(t == pl.num_programs(2) - 1)
    def _():
        sfin_ref[0] = s_scr[...]


def _gla(gqk, gv, gg, ga, wa_hi, wa_lo, ba, nw, s0, batch, seq, t_valid, hps):
    n = gqk.shape[0]
    c = GLA_CHUNK
    n_sub = min(8, seq // c)
    rows = c * n_sub
    steps = seq // rows
    ng = N_GLA_HEADS // hps
    return pl.pallas_call(
        functools.partial(_gla_kernel, chunk=c, n_sub=n_sub, t_valid=t_valid, hps=hps),
        grid=(batch, ng, steps),
        in_specs=[
            pl.BlockSpec((rows, hps * GLA_DK), lambda b, h, t: (b * steps + t, h)),
            pl.BlockSpec((rows, hps * GLA_DK), lambda b, h, t: (b * steps + t, ng + h)),
            pl.BlockSpec((rows, hps * GLA_DV), lambda b, h, t: (b * steps + t, h)),
            pl.BlockSpec((rows, hps * GLA_DV), lambda b, h, t: (b * steps + t, h)),
            pl.BlockSpec((rows, LANES), lambda b, h, t: (b * steps + t, 0)),
            pl.BlockSpec((LANES, hps * GLA_DK), lambda b, h, t: (0, h)),
            pl.BlockSpec((LANES, hps * GLA_DK), lambda b, h, t: (0, h)),
            pl.BlockSpec((1, hps * GLA_DK), lambda b, h, t: (0, h)),
            pl.BlockSpec((1, GLA_DV), lambda b, h, t: (0, 0)),
            pl.BlockSpec((1, hps, GLA_DK, GLA_DV), lambda b, h, t: (b, h, 0, 0)),
        ],
        out_specs=[
            pl.BlockSpec((rows, hps * GLA_DV), lambda b, h, t: (b * steps + t, h)),
            pl.BlockSpec((1, hps, GLA_DK, GLA_DV), lambda b, h, t: (b, h, 0, 0)),
        ],
        out_shape=[
            jax.ShapeDtypeStruct((n, GLA_VW), BF16),
            jax.ShapeDtypeStruct(s0.shape, F32),
        ],
        scratch_shapes=[
            pltpu.VMEM((hps, GLA_DK, GLA_DV), F32),
            pltpu.VMEM((hps, rows, GLA_DK), BF16),
            pltpu.VMEM((hps, rows, GLA_DV), F32),
            pltpu.VMEM((hps, n_sub, GLA_DK, GLA_DV), F32),
            pltpu.VMEM((hps, n_sub, GLA_DK, GLA_DK), F32),
        ],
        compiler_params=_params("parallel", "parallel", "arbitrary"),
        name="gla",
    )(gqk, gqk, gv, gg, ga, wa_hi, wa_lo, ba, nw, s0)


def _out_router_kernel(att_ref, gla_ref, x_ref, woa_ref, wog_ref, n2_ref, wrh_ref, wrl_ref, br_ref,
                       h_ref, hn_ref, re_ref, rg_ref):
    h = x_ref[...] + _dot(att_ref[...], woa_ref[...]) + _dot(gla_ref[...], wog_ref[...])
    h_ref[...] = h
    hn = h * lax.rsqrt(jnp.mean(h * h, axis=-1, keepdims=True) + RMS_EPS) * n2_ref[...]
    pieces = hn.shape[1] // LANES
    for c in range(pieces):
        hn_ref[pl.ds(c, hn.shape[0], stride=pieces), :] = hn[:, c * LANES:(c + 1) * LANES]
    hn_hi = hn.astype(BF16)
    hn_lo = (hn - hn_hi.astype(F32)).astype(BF16)
    logits = (_dot(hn_hi, wrh_ref[...]) + _dot(hn_hi, wrl_ref[...]) + _dot(hn_lo, wrh_ref[...])) + br_ref[...]
    col = lax.broadcasted_iota(I32, logits.shape, 1)
    big = jnp.int32(2 ** 30)
    in_g = col < N_GROUPS
    lg = jnp.where(in_g, logits, -jnp.inf)
    m1 = jnp.max(lg, axis=-1, keepdims=True)
    g_idx = jnp.min(jnp.where(lg == m1, col, big), axis=-1, keepdims=True)
    g_top = 1.0 / jnp.sum(jnp.exp(lg - m1), axis=-1, keepdims=True)
    ecol = col - N_GROUPS
    in_e = (ecol >= g_idx * EXPERTS_PER_GROUP) & (ecol < (g_idx + 1) * EXPERTS_PER_GROUP)
    le = jnp.where(in_e, logits, -jnp.inf)
    l1 = jnp.max(le, axis=-1, keepdims=True)
    e1 = jnp.min(jnp.where(le == l1, ecol, big), axis=-1, keepdims=True)
    le2 = jnp.where(ecol == e1, -jnp.inf, le)
    l2 = jnp.max(le2, axis=-1, keepdims=True)
    e2 = jnp.min(jnp.where(le2 == l2, ecol, big), axis=-1, keepdims=True)
    w2 = jnp.exp(l2 - l1)
    den = 1.0 + w2
    re_ref[...] = jnp.where(col == 0, e1, jnp.where(col == 1, e2, 0))
    rg_ref[...] = jnp.where(col == 0, g_top / den, jnp.where(col == 1, g_top * w2 / den, 0.0))


def _out_router_into_kernel(*refs):
    _out_router_kernel(*refs[:9], *refs[10:])


def _out_router_alloc_kernel(*refs, n_real):
    i = pl.program_id(0)

    @pl.when(i < n_real)
    def _():
        _out_router_kernel(*refs)

    @pl.when(i >= n_real)
    def _():
        refs[10][...] = jnp.zeros_like(refs[10])


def _out_router(att, gla, x, wo_att, wo_gla, n2, wr_hi, wr_lo, br, tm, n_all, row0, hn_all=None):
    n, d = x.shape
    n_real = n // tm
    row = lambda i: (jnp.minimum(i, n_real - 1), 0)
    fix = lambda i: (0, 0)
    b0 = row0 // tm
    if hn_all is None:
        assert row0 == 0 and n_all % tm == 0
        body, steps = functools.partial(_out_router_alloc_kernel, n_real=n_real), n_all // tm
        extra_in, extra_args, alias = [], [], {}
    else:
        body, steps = _out_router_into_kernel, n_real
        extra_in, extra_args, alias = [pl.BlockSpec(memory_space=pl.ANY)], [hn_all], {9: 1}
    return pl.pallas_call(
        body,
        grid=(steps,),
        input_output_aliases=alias,
        in_specs=[
            pl.BlockSpec((tm, ATT_WIDTH), row),
            pl.BlockSpec((tm, GLA_VW), row),
            pl.BlockSpec((tm, d), row),
            pl.BlockSpec((ATT_WIDTH, d), fix),
            pl.BlockSpec((GLA_VW, d), fix),
            pl.BlockSpec((1, d), fix),
            pl.BlockSpec((d, LANES), fix),
            pl.BlockSpec((d, LANES), fix),
            pl.BlockSpec((1, LANES), fix),
        ] + extra_in,
        out_specs=[
            pl.BlockSpec((tm, d), row),
            pl.BlockSpec((tm * (d // LANES), LANES), lambda i: (b0 + i, 0)),
            pl.BlockSpec((tm, LANES), row),
            pl.BlockSpec((tm, LANES), row),
        ],
        out_shape=[
            jax.ShapeDtypeStruct((n, d), F32),
            jax.ShapeDtypeStruct((n_all * (d // LANES), LANES), F32),
            jax.ShapeDtypeStruct((n, LANES), I32),
            jax.ShapeDtypeStruct((n, LANES), F32),
        ],
        compiler_params=_params("arbitrary"),
        name="out_router",
    )(att, gla, x, wo_att, wo_gla, n2, wr_hi, wr_lo, br, *extra_args)


def _expert_kernel(be_ref, tok_ref, nused_ref, hn_hbm, wg_ref, wu_ref, wd_ref, o_ref,
                   xbuf, wgb, wub, wdb, sem, *, rows, n_blocks):
    i = pl.program_id(0)

    @pl.when((i == 0) | (be_ref[i] != be_ref[jnp.maximum(i - 1, 0)]))
    def _():
        wgb[...] = wg_ref[0].astype(BF16)
        wub[...] = wu_ref[0].astype(BF16)
        wdb[...] = wd_ref[0].astype(BF16)

    n_used = nused_ref[0]

    pieces = wgb.shape[0] // LANES

    def issue(blk, slot, r):
        src = pl.multiple_of(tok_ref[blk * rows + r] * pieces, pieces)
        pltpu.make_async_copy(hn_hbm.at[pl.ds(src, pieces), :], xbuf.at[slot, pl.ds(r * pieces, pieces), :],
                              sem.at[slot]).start(priority=1)

    def wait_rows(slot):
        pltpu.make_async_copy(hn_hbm.at[pl.ds(0, rows * pieces), :], xbuf.at[slot], sem.at[slot]).wait()

    @pl.when(i == 0)
    def _():
        def body(g, carry):
            for u in range(DMA_UNROLL):
                issue(0, 0, g * DMA_UNROLL + u)
            return carry
        lax.fori_loop(0, rows // DMA_UNROLL, body, 0)

    slot = i % 2

    @pl.when(i < n_used)
    def _():
        wait_rows(slot)
        xs = xbuf.at[slot]
        x = jnp.concatenate([xs[pl.ds(c, rows, stride=pieces), :] for c in range(pieces)], axis=1).astype(BF16)
        for r in range(rows):
            issue(i + 1, 1 - slot, r)
        a = _dot(x, wgb[...])
        u = _dot(x, wub[...])
        act = (a * jax.nn.sigmoid(a) * u).astype(BF16)
        o_ref[...] = _dot(act, wdb[...])

    @pl.when(i == n_used)
    def _():
        wait_rows(slot)

    @pl.when(i >= n_used)
    def _():
        o_ref[...] = jnp.zeros_like(o_ref)


def _experts(block_expert, slot_tok, n_used, hn, wg, wu, wd):
    n_blocks = block_expert.shape[0]
    rows = EXPERT_ROWS
    d = wg.shape[1]
    f = wg.shape[2]
    pieces = d // LANES
    grid_spec = pltpu.PrefetchScalarGridSpec(
        num_scalar_prefetch=3,
        grid=(n_blocks,),
        in_specs=[
            pl.BlockSpec(memory_space=pl.ANY),
            pl.BlockSpec((1, d, f), lambda i, be, tk, nu: (be[i], 0, 0)),
            pl.BlockSpec((1, d, f), lambda i, be, tk, nu: (be[i], 0, 0)),
            pl.BlockSpec((1, f, d), lambda i, be, tk, nu: (be[i], 0, 0)),
        ],
        out_specs=pl.BlockSpec((rows, d), lambda i, be, tk, nu: (i, 0)),
        scratch_shapes=[
            pltpu.VMEM((2, rows * pieces, LANES), F32),
            pltpu.VMEM((d, f), BF16),
            pltpu.VMEM((d, f), BF16),
            pltpu.VMEM((f, d), BF16),
            pltpu.SemaphoreType.DMA((2,)),
        ],
    )
    return pl.pallas_call(
        functools.partial(_expert_kernel, rows=rows, n_blocks=n_blocks),
        grid_spec=grid_spec,
        out_shape=jax.ShapeDtypeStruct((n_blocks * rows, d), F32),
        compiler_params=_params("arbitrary"),
        name="experts",
    )(block_expert, slot_tok, n_used, hn, wg, wu, wd)


def _combine_kernel(pos_ref, h_ref, rg_ref, ys_hbm, y_ref, buf, sem, *, rows, n_steps, tok0):
    i = pl.program_id(0)

    def start(step, slot):
        def body(g, carry):
            for u in range(DMA_UNROLL):
                r = g * DMA_UNROLL + u
                for k in range(EXPERT_TOPK):
                    p = pos_ref[(tok0 + step * rows + r) * EXPERT_TOPK + k]
                    pltpu.make_async_copy(ys_hbm.at[pl.ds(p, 1), :], buf.at[slot, k, pl.ds(r, 1), :],
                                          sem.at[slot]).start(priority=k % 2)
            return carry
        lax.fori_loop(0, rows // DMA_UNROLL, body, 0)

    @pl.when(i == 0)
    def _():
        start(0, 0)

    @pl.when(i + 1 < n_steps)
    def _():
        start(i + 1, (i + 1) % 2)

    slot = i % 2

    for k in range(EXPERT_TOPK):
        pltpu.make_async_copy(ys_hbm.at[pl.ds(0, rows), :], buf.at[slot, k], sem.at[slot]).wait()

    rg = rg_ref[...]
    y = h_ref[...]
    for k in range(EXPERT_TOPK):
        y = y + buf[slot, k] * rg[:, k:k + 1]
    y_ref[...] = y


def _combine(pos, h, rg, ys, tok0, rows):
    n_tok, d = h.shape
    n_steps = n_tok // rows
    grid_spec = pltpu.PrefetchScalarGridSpec(
        num_scalar_prefetch=1,
        grid=(n_steps,),
        in_specs=[
            pl.BlockSpec((rows, d), lambda i, p: (i, 0)),
            pl.BlockSpec((rows, LANES), lambda i, p: (i, 0)),
            pl.BlockSpec(memory_space=pl.ANY),
        ],
        out_specs=pl.BlockSpec((rows, d), lambda i, p: (i, 0)),
        scratch_shapes=[
            pltpu.VMEM((2, EXPERT_TOPK, rows, d), F32),
            pltpu.SemaphoreType.DMA((2,)),
        ],
    )
    return pl.pallas_call(
        functools.partial(_combine_kernel, rows=rows, n_steps=n_steps, tok0=tok0),
        grid_spec=grid_spec,
        out_shape=jax.ShapeDtypeStruct((n_tok, d), F32),
        compiler_params=_params("arbitrary"),
        name="combine",
    )(pos, h, rg, ys)


def _dispatch(expert):
    a = expert.shape[0]
    rows = EXPERT_ROWS
    onehot = (expert[:, None] == jnp.arange(N_EXPERTS, dtype=I32)[None, :]).astype(I32)
    csum = jnp.cumsum(onehot, axis=0)
    counts = csum[-1]
    rank = jnp.take_along_axis(csum, expert[:, None], axis=1)[:, 0] - 1
    padded = (counts + rows - 1) // rows * rows
    ends_p = jnp.cumsum(padded)
    pstart = ends_p - padded
    pos = (pstart[expert] + rank).astype(I32)
    n_blocks = -(-a // rows) + N_EXPERTS + 1
    slot_tok = jnp.zeros((n_blocks * rows,), I32).at[pos].set(jnp.arange(a, dtype=I32) // EXPERT_TOPK)
    block_start = jnp.arange(n_blocks, dtype=I32) * rows
    block_expert = jnp.minimum(
        jnp.sum((ends_p[None, :] <= block_start[:, None]).astype(I32), axis=1), N_EXPERTS - 1).astype(I32)
    n_used = (ends_p[-1:] // rows).astype(I32)
    return pos, slot_tok, block_expert, n_used


def kernel(x_prompt, x_sample, cache_k, cache_v, state_gla, page_table, norm1_w, w_in, q_norm_w, k_norm_w,
           w_gla_a2, b_gla_a, gla_norm_w, w_out, norm2_w, w_r1, b_r1, w_r2, b_r2, w_e_gate, w_e_up, w_e_down):
    depth = w_in.shape[0]
    assert depth == 1
    batch, seq, d = x_prompt.shape
    db, nq, _ = x_sample.shape
    n_pages = page_table.shape[1]
    past = n_pages * PAGE_SIZE
    assert past % MOBA_BLOCK == 0 and nq <= GLA_CHUNK and seq % (8 * MOBA_BLOCK) == 0
    n_past_blocks = past // MOBA_BLOCK
    assert n_past_blocks + 8 <= LANES
    n_p, n_s = batch * seq, db * nq
    l = 0

    w = w_in[l]
    o_gq = 3 * ATT_WIDTH
    o_gv = o_gq + 2 * GLA_KW
    o_ga = o_gv + GLA_VW
    o_gg = o_ga + GLA_GATE_RANK
    w_main = jnp.concatenate([w[:, :o_ga], w[:, o_gg:]], axis=1).astype(BF16)
    w_ga = jnp.pad(w[:, o_ga:o_gg], ((0, 0), (0, LANES - GLA_GATE_RANK))).astype(BF16)
    n1 = norm1_w[l][None, :]
    qn = q_norm_w[l][None, :]
    kn = k_norm_w[l][None, :]
    wa = jnp.pad(w_gla_a2[l], ((0, LANES - GLA_GATE_RANK), (0, 0)))
    wa_hi = wa.astype(BF16)
    wa_lo = (wa - wa_hi.astype(F32)).astype(BF16)
    ba = b_gla_a[l][None, :]
    gnw = gla_norm_w[l][None, :]
    wo = w_out[l].astype(BF16)
    wo_att, wo_gla = wo[:ATT_WIDTH], wo[ATT_WIDTH:]
    n2 = norm2_w[l][None, :]
    wr = jnp.pad(jnp.concatenate([w_r1[l], w_r2[l]], axis=1), ((0, 0), (0, LANES - N_GROUPS - N_EXPERTS)))
    br = jnp.pad(jnp.concatenate([b_r1[l], b_r2[l]]), (0, LANES - N_GROUPS - N_EXPERTS))[None, :]
    slopes = 2.0 ** (-8.0 * jnp.arange(1, N_ATT_HEADS + 1, dtype=F32) / N_ATT_HEADS)

    xp = x_prompt.reshape(n_p, d)
    q_p, k_p, kb_p, v_p, vt_p, gqk_p, gv_p, gg_p, ga_p = _project(xp, n1, w_main, w_ga, qn, kn, 512)
    kmeans = _block_means(k_p)
    idx_p = _prompt_select(q_p, kmeans, seq)
    att_p, block_sums = _moba_prompt(page_table, slopes, q_p, kb_p, vt_p, idx_p, cache_k[l], batch, seq)
    s0_p = jnp.zeros((batch, N_GLA_HEADS, GLA_DK, GLA_DV), F32)
    gla_p, s_p = _gla(gqk_p, gv_p, gg_p, ga_p, wa_hi, wa_lo, ba, gnw, s0_p, batch, seq, seq, GLA_HEADS_PROMPT)

    xs = x_sample.reshape(n_s, d)
    q_s, k_s, _, v_s, _, gqk_s, gv_s, gg_s, ga_s = _project(xs, n1, w_main, w_ga, qn, kn, n_s)
    idx = _sample_select(q_s, block_sums.reshape(db * n_past_blocks, ATT_WIDTH), k_s, db, nq, n_past_blocks)
    att_s = _moba_sample(page_table, idx[..., :MOBA_TOPK], slopes, q_s, k_s, v_s, cache_k[l], cache_v[l], db, nq)

    def pad_seq(a):
        return jnp.pad(a.reshape(db, nq, -1), ((0, 0), (0, GLA_CHUNK - nq), (0, 0))).reshape(db * GLA_CHUNK, -1)

    gla_s, s_s = _gla(pad_seq(gqk_s), pad_seq(gv_s), pad_seq(gg_s), pad_seq(ga_s), wa_hi, wa_lo, ba, gnw,
                      state_gla[l], db, GLA_CHUNK, nq, N_GLA_HEADS)
    gla_s = gla_s.reshape(db, GLA_CHUNK, GLA_VW)[:, :nq].reshape(n_s, GLA_VW)

    wr_hi = wr.astype(BF16)
    wr_lo = (wr - wr_hi.astype(F32)).astype(BF16)
    n_all = -(-(n_p + n_s) // OUT_ROWS) * OUT_ROWS
    h_p, hn, re_p, rg_p = _out_router(att_p, gla_p, xp, wo_att, wo_gla, n2, wr_hi, wr_lo, br, OUT_ROWS, n_all, 0)
    h_s, hn, re_s, rg_s = _out_router(att_s.astype(BF16), gla_s, xs, wo_att, wo_gla, n2, wr_hi, wr_lo, br,
                                      n_s, n_all, n_p, hn)
    expert = jnp.concatenate([re_p[:, :EXPERT_TOPK], re_s[:, :EXPERT_TOPK]], axis=0).reshape(-1)
    pos, slot_tok, block_expert, n_used = _dispatch(expert)
    ys = _experts(block_expert, slot_tok, n_used, hn, w_e_gate[l], w_e_up[l], w_e_down[l])
    y_p = _combine(pos, h_p, rg_p, ys, 0, 128)
    y_s = _combine(pos, h_s, rg_s, ys, n_p, 128)

    hd = (N_ATT_HEADS, HEAD_DIM)
    return (y_p.reshape(batch, seq, d), y_s.reshape(db, nq, d),
            k_p.reshape(1, batch, seq, *hd), v_p.reshape(1, batch, seq, *hd), s_p[None],
            k_s.reshape(1, db, nq, *hd), v_s.reshape(1, db, nq, *hd), s_s[None])
```
